```python
import jax, jax.numpy as jnp
from jax import lax
import numpy as np

D_MODEL = 2048
BATCH = 4
SEQ = 2048
DEPTH = 4
DEC_BATCH = 128
DEC_SEQ = 4
PAST_LEN = 16384
PAGE_SIZE = 128

F32 = jnp.float32
N_MIXER_TYPES = 3
EXPAND = 2
D_INNER = EXPAND * D_MODEL
CHUNK = 64
EPS = 1e-6
PLE_DIM = 256

RET_HEADS = 8
RET_DK = D_MODEL // RET_HEADS
RET_DV = D_INNER // RET_HEADS
ROPE_BASE = 10000.0

ML_HEADS = 8
ML_DK = D_MODEL // ML_HEADS
ML_DV = D_INNER // ML_HEADS

GDN_DK = 128
GDN_DV = 128
GDN_K_HEADS = D_MODEL // GDN_DK
GDN_V_HEADS = D_INNER // GDN_DV
CONV_W = 4
GDN_CONV_DIM = 2 * GDN_K_HEADS * GDN_DK + GDN_V_HEADS * GDN_DV

RET_IN = 2 * RET_HEADS * RET_DK + 2 * D_INNER
ML_IN = 2 * ML_HEADS * ML_DK + 3 * D_INNER + 2 * ML_HEADS
GDN_IN = GDN_CONV_DIM + D_INNER + 2 * GDN_V_HEADS

N_RET = (DEPTH + 2) // 3
N_MLSTM = (DEPTH + 1) // 3
N_GDN = DEPTH // 3

kernel_name = 'hybrid_retention_mlstm_gdn_step'


def _rmsnorm(x, g):
    xf = x.astype(F32)
    y = xf * lax.rsqrt(jnp.mean(xf * xf, axis=-1, keepdims=True) + EPS)
    return (y * g.astype(F32)).astype(x.dtype)


def _heads(x, n):
    b, t, _ = x.shape
    return x.reshape(b, t, n, -1).transpose(0, 2, 1, 3)


def _merge(x):
    b, n, t, d = x.shape
    return x.transpose(0, 2, 1, 3).reshape(b, t, n * d)


def _head_norm(x, n, gain, center):
    b, t, _ = x.shape
    xh = x.astype(F32).reshape(b, t, n, -1)
    if center:
        xh = xh - jnp.mean(xh, axis=-1, keepdims=True)
    xh = xh * lax.rsqrt(jnp.mean(xh * xh, axis=-1, keepdims=True) + EPS)
    return (xh * gain.astype(F32).reshape(-1, xh.shape[-1])).reshape(b, t, -1)


def _l2norm(x):
    return x * lax.rsqrt(jnp.sum(x * x, axis=-1, keepdims=True) + EPS)


def _rotary(x, pos):
    half = x.shape[-1] // 2
    inv = ROPE_BASE ** (-jnp.arange(half, dtype=F32) / half)
    ang = pos.astype(F32)[:, None] * inv[None, :]
    cos, sin = jnp.cos(ang), jnp.sin(ang)
    x1, x2 = x[..., :half], x[..., half:]
    return jnp.concatenate([x1 * cos - x2 * sin, x2 * cos + x1 * sin], axis=-1)


def _scan_chunks(step, carry, xs, t):
    size = CHUNK if t % CHUNK == 0 else t
    n = t // size

    def split(a):
        return jnp.moveaxis(a.reshape(a.shape[:2] + (n, size) + a.shape[3:]), 2, 0)

    carry, ys = lax.scan(step, carry, tuple(split(a) for a in xs))
    ys = jnp.moveaxis(ys, 0, 2)
    return carry, ys.reshape(ys.shape[:2] + (t,) + ys.shape[4:])


def _retention(h, s0, pos, w_in, gain, w_out):
    t = h.shape[1]
    qk = RET_HEADS * RET_DK
    q, k, v, z = jnp.split(h @ w_in, [qk, 2 * qk, 2 * qk + D_INNER], axis=-1)
    q = _rotary(_heads(q, RET_HEADS).astype(F32), pos)
    k = _rotary(_heads(k, RET_HEADS).astype(F32), pos) * RET_DK ** -0.5
    v = _heads(v, RET_HEADS).astype(F32)
    lg = jnp.log1p(-jnp.exp2(-5.0 - jnp.arange(RET_HEADS, dtype=F32)))

    def step(s, c):
        qc, kc, vc = c
        size = qc.shape[2]
        idx = jnp.arange(size, dtype=F32)
        rel = idx[:, None] - idx[None, :]
        decay = jnp.exp(jnp.where(rel >= 0, lg[:, None, None] * rel, -jnp.inf))
        scores = jnp.einsum('bhld,bhsd->bhls', qc, kc) * decay
        q_in = qc * jnp.exp(lg[:, None] * (idx + 1.0))[..., None]
        o = jnp.einsum('bhls,bhsv->bhlv', scores, vc) + jnp.einsum('bhld,bhdv->bhlv', q_in, s)
        k_out = kc * jnp.exp(lg[:, None] * (size - 1.0 - idx))[..., None]
        s = jnp.exp(lg * size)[:, None, None] * s + jnp.einsum('bhsd,bhsv->bhdv', k_out, vc)
        return s, o

    s, o = _scan_chunks(step, s0.astype(F32), (q, k, v), t)
    y = _head_norm(_merge(o), RET_HEADS, gain, True)
    out = (y * jax.nn.silu(z.astype(F32))).astype(h.dtype) @ w_out
    return out, s


def _mlstm(h, c0, n0, m0, w_in, b_gate, gain, w_out):
    t = h.shape[1]
    qk = ML_HEADS * ML_DK
    cuts = [qk, 2 * qk, 2 * qk + D_INNER, 2 * qk + 2 * D_INNER, 2 * qk + 3 * D_INNER,
            2 * qk + 3 * D_INNER + ML_HEADS]
    q, k, v, og, z, ig, fg = jnp.split(h @ w_in, cuts, axis=-1)
    q = _heads(q, ML_HEADS).astype(F32) * ML_DK ** -0.5
    k = _heads(k, ML_HEADS).astype(F32)
    v = _heads(v, ML_HEADS).astype(F32)
    bg = b_gate.astype(F32)
    ig = (ig.astype(F32) + bg[:ML_HEADS]).transpose(0, 2, 1)
    lf = jax.nn.log_sigmoid(fg.astype(F32) + bg[ML_HEADS:]).transpose(0, 2, 1)

    def step(carry, c):
        cm, nv, m = carry
        qc, kc, vc, ic, fc = c
        size = qc.shape[2]
        causal = jnp.tril(jnp.ones((size, size), dtype=bool))
        b = jnp.cumsum(fc, axis=-1)
        dlog = jnp.where(causal, b[..., :, None] - b[..., None, :] + ic[..., None, :], -jnp.inf)
        inter = b + m[..., None]
        mt = jnp.maximum(inter, jnp.max(dlog, axis=-1))
        s = jnp.einsum('bhld,bhsd->bhls', qc, kc) * jnp.exp(dlog - mt[..., None])
        wi = jnp.exp(inter - mt)
        num = jnp.einsum('bhls,bhsv->bhlv', s, vc) + wi[..., None] * jnp.einsum('bhld,bhdv->bhlv', qc, cm)
        den = jnp.sum(s, axis=-1) + wi * jnp.einsum('bhld,bhd->bhl', qc, nv)
        ht = num / jnp.maximum(jnp.abs(den), jnp.exp(-mt))[..., None]
        m_new = mt[..., -1]
        w_last = jnp.exp(b[..., -1:] - b + ic - m_new[..., None])
        dec = jnp.exp(b[..., -1] + m - m_new)
        kw = kc * w_last[..., None]
        cm = dec[..., None, None] * cm + jnp.einsum('bhsd,bhsv->bhdv', kw, vc)
        nv = dec[..., None] * nv + jnp.sum(kw, axis=2)
        return (cm, nv, m_new), ht

    (cm, nv, m), ht = _scan_chunks(step, (c0.astype(F32), n0.astype(F32), m0.astype(F32)),
                                   (q, k, v, ig, lf), t)
    hcell = _merge(ht) * jax.nn.sigmoid(og.astype(F32))
    y = _head_norm(hcell, ML_HEADS, gain, True)
    out = (y * jax.nn.silu(z.astype(F32))).astype(h.dtype) @ w_out
    return out, cm, nv, m


def _gated_deltanet(h, s0, conv0, w_in, conv_w, a_log, dt_bias, gain, w_out):
    t = h.shape[1]
    mixed, z, bb, aa = jnp.split(h @ w_in, [GDN_CONV_DIM, GDN_CONV_DIM + D_INNER,
                                            GDN_CONV_DIM + D_INNER + GDN_V_HEADS], axis=-1)
    xpad = jnp.concatenate([conv0.astype(mixed.dtype), mixed], axis=1)
    conv = xpad[:, 0:t] * conv_w[0]
    for w in range(1, CONV_W):
        conv = conv + xpad[:, w:w + t] * conv_w[w]
    new_conv = xpad[:, t:]
    conv = jax.nn.silu(conv.astype(F32))
    qk = GDN_K_HEADS * GDN_DK
    q, k, v = jnp.split(conv, [qk, 2 * qk], axis=-1)
    rep = GDN_V_HEADS // GDN_K_HEADS
    q = jnp.repeat(_l2norm(_heads(q, GDN_K_HEADS)), rep, axis=1) * GDN_DK ** -0.5
    k = jnp.repeat(_l2norm(_heads(k, GDN_K_HEADS)), rep, axis=1)
    v = _heads(v, GDN_V_HEADS)
    beta = jax.nn.sigmoid(bb.astype(F32)).transpose(0, 2, 1)
    g = (-jnp.exp(a_log.astype(F32)) * jax.nn.softplus(aa.astype(F32) + dt_bias.astype(F32))).transpose(0, 2, 1)

    def step(s, c):
        qc, kc, vc, bc, gc = c
        size = qc.shape[2]
        idx = jnp.arange(size)
        incl = idx[:, None] >= idx[None, :]
        strict = idx[:, None] > idx[None, :]
        gcum = jnp.cumsum(gc, axis=-1)
        decay = jnp.exp(jnp.where(incl, gcum[..., :, None] - gcum[..., None, :], -jnp.inf))
        kk = jnp.einsum('bhld,bhsd->bhls', kc, kc)
        a = jnp.where(strict, bc[..., :, None] * kk * decay, 0.0) + jnp.eye(size, dtype=F32)
        eg = jnp.exp(gcum)
        rhs = bc[..., None] * vc - (bc * eg)[..., None] * jnp.einsum('bhld,bhdv->bhlv', kc, s)
        u = lax.linalg.triangular_solve(a, rhs, left_side=True, lower=True, unit_diagonal=True)
        qk_d = jnp.einsum('bhld,bhsd->bhls', qc, kc) * decay
        o = eg[..., None] * jnp.einsum('bhld,bhdv->bhlv', qc, s) + jnp.einsum('bhls,bhsv->bhlv', qk_d, u)
        g_last = gcum[..., -1]
        kw = kc * jnp.exp(g_last[..., None] - gcum)[..., None]
        s = jnp.exp(g_last)[..., None, None] * s + jnp.einsum('bhsd,bhsv->bhdv', kw, u)
        return s, o

    s, o = _scan_chunks(step, s0.astype(F32), (q, k, v, beta, g), t)
    y = _head_norm(_merge(o), GDN_V_HEADS, gain, False)
    out = (y * jax.nn.silu(z.astype(F32))).astype(h.dtype) @ w_out
    return out, s, new_conv


def _trunk(x, p, ret_s, ml_c, ml_n, ml_m, gdn_s, gdn_conv, pos,
           norm_pre, norm_post, ple_proj, ple_gate,
           ret_w_in, ret_head_norm, ret_w_out,
           mlstm_w_in, mlstm_b_gate, mlstm_head_norm, mlstm_w_out,
           gdn_w_in, gdn_conv_w, gdn_a_log, gdn_dt_bias, gdn_head_norm, gdn_w_out):
    new_ret, new_c, new_n, new_m, new_gs, new_gc = [], [], [], [], [], []
    r = x
    for i in range(DEPTH):
        kind, j = i % N_MIXER_TYPES, i // N_MIXER_TYPES
        h = _rmsnorm(r, norm_pre[i])
        if kind == 0:
            mix, s = _retention(h, ret_s[j], pos, ret_w_in[j], ret_head_norm[j], ret_w_out[j])
            new_ret.append(s)
        elif kind == 1:
            mix, cm, nv, m = _mlstm(h, ml_c[j], ml_n[j], ml_m[j], mlstm_w_in[j], mlstm_b_gate[j],
                                    mlstm_head_norm[j], mlstm_w_out[j])
            new_c.append(cm)
            new_n.append(nv)
            new_m.append(m)
        else:
            mix, s, cv = _gated_deltanet(h, gdn_s[j], gdn_conv[j], gdn_w_in[j], gdn_conv_w[j],
                                         gdn_a_log[j], gdn_dt_bias[j], gdn_head_norm[j], gdn_w_out[j])
            new_gs.append(s)
            new_gc.append(cv)
        r = r + _rmsnorm(mix, norm_post[i])
        r = r + (p[i] @ ple_proj[i]) * jax.nn.sigmoid(r @ ple_gate[i])
    return (r, jnp.stack(new_ret), jnp.stack(new_c), jnp.stack(new_n), jnp.stack(new_m),
            jnp.stack(new_gs), jnp.stack(new_gc))


def setup_inputs(seed: int = 0) -> dict:
    key = jax.random.key(seed)
    ks = iter(jax.random.split(key, 32))

    def nrm(shape, scale):
        return scale * jax.random.normal(next(ks), shape, F32)

    d, di = D_MODEL, D_INNER
    x_prompt = nrm((BATCH, SEQ, d), 1.0)
    x_sample = nrm((DEC_BATCH, DEC_SEQ, d), 1.0)
    state_ret_S = nrm((N_RET, DEC_BATCH, RET_HEADS, RET_DK, RET_DV), 0.1)
    state_mlstm_C = nrm((N_MLSTM, DEC_BATCH, ML_HEADS, ML_DK, ML_DV), 0.1)
    state_mlstm_n = nrm((N_MLSTM, DEC_BATCH, ML_HEADS, ML_DK), 0.1)
    state_mlstm_m = nrm((N_MLSTM, DEC_BATCH, ML_HEADS), 1.0)
    state_gdn_S = nrm((N_GDN, DEC_BATCH, GDN_V_HEADS, GDN_DK, GDN_DV), 0.1)
    state_gdn_conv = nrm((N_GDN, DEC_BATCH, CONV_W - 1, GDN_CONV_DIM), 1.0)
    p_prompt = nrm((DEPTH, BATCH, SEQ, PLE_DIM), 1.0)
    p_sample = nrm((DEPTH, DEC_BATCH, DEC_SEQ, PLE_DIM), 1.0)
    norm_pre = 1.0 + nrm((DEPTH, d), 0.02)
    norm_post = 1.0 + nrm((DEPTH, d), 0.02)
    ple_proj = nrm((DEPTH, PLE_DIM, d), PLE_DIM ** -0.5)
    ple_gate = nrm((DEPTH, d, d), d ** -0.5)
    ret_w_in = nrm((N_RET, d, RET_IN), d ** -0.5)
    ret_head_norm = 1.0 + nrm((N_RET, di), 0.02)
    ret_w_out = nrm((N_RET, di, d), di ** -0.5)
    mlstm_w_in = nrm((N_MLSTM, d, ML_IN), d ** -0.5)
    mlstm_b_gate = jnp.concatenate(
        [nrm((N_MLSTM, ML_HEADS), 0.1),
         jnp.linspace(3.0, 6.0, ML_HEADS, dtype=F32)[None, :] + nrm((N_MLSTM, ML_HEADS), 0.1)], axis=-1)
    mlstm_head_norm = 1.0 + nrm((N_MLSTM, di), 0.02)
    mlstm_w_out = nrm((N_MLSTM, di, d), di ** -0.5)
    gdn_w_in = nrm((N_GDN, d, GDN_IN), d ** -0.5)
    gdn_conv_w = nrm((N_GDN, CONV_W, GDN_CONV_DIM), CONV_W ** -0.5)
    gdn_a_log = jnp.log(jax.random.uniform(next(ks), (N_GDN, GDN_V_HEADS), F32, 1.0, 16.0))
    gdn_dt_bias = nrm((N_GDN, GDN_V_HEADS), 0.1)
    gdn_head_norm = 1.0 + nrm((N_GDN, GDN_DV), 0.02)
    gdn_w_out = nrm((N_GDN, di, d), di ** -0.5)
    return {'x_prompt': x_prompt, 'x_sample': x_sample,
            'state_ret_S': state_ret_S, 'state_mlstm_C': state_mlstm_C,
            'state_mlstm_n': state_mlstm_n, 'state_mlstm_m': state_mlstm_m,
            'state_gdn_S': state_gdn_S, 'state_gdn_conv': state_gdn_conv,
            'p_prompt': p_prompt, 'p_sample': p_sample,
            'norm_pre': norm_pre, 'norm_post': norm_post, 'ple_proj': ple_proj, 'ple_gate': ple_gate,
            'ret_w_in': ret_w_in, 'ret_head_norm': ret_head_norm, 'ret_w_out': ret_w_out,
            'mlstm_w_in': mlstm_w_in, 'mlstm_b_gate': mlstm_b_gate,
            'mlstm_head_norm': mlstm_head_norm, 'mlstm_w_out': mlstm_w_out,
            'gdn_w_in': gdn_w_in, 'gdn_conv_w': gdn_conv_w, 'gdn_a_log': gdn_a_log,
            'gdn_dt_bias': gdn_dt_bias, 'gdn_head_norm': gdn_head_norm, 'gdn_w_out': gdn_w_out}


def reference(x_prompt, x_sample, state_ret_S, state_mlstm_C, state_mlstm_n, state_mlstm_m,
              state_gdn_S, state_gdn_conv, p_prompt, p_sample,
              norm_pre, norm_post, ple_proj, ple_gate,
              ret_w_in, ret_head_norm, ret_w_out,
              mlstm_w_in, mlstm_b_gate, mlstm_head_norm, mlstm_w_out,
              gdn_w_in, gdn_conv_w, gdn_a_log, gdn_dt_bias, gdn_head_norm, gdn_w_out):
    weights = (norm_pre, norm_post, ple_proj, ple_gate,
               ret_w_in, ret_head_norm, ret_w_out,
               mlstm_w_in, mlstm_b_gate, mlstm_head_norm, mlstm_w_out,
               gdn_w_in, gdn_conv_w, gdn_a_log, gdn_dt_bias, gdn_head_norm, gdn_w_out)
    bp, tp = x_prompt.shape[0], x_prompt.shape[1]
    z_ret = jnp.zeros((N_RET, bp, RET_HEADS, RET_DK, RET_DV), F32)
    z_c = jnp.zeros((N_MLSTM, bp, ML_HEADS, ML_DK, ML_DV), F32)
    z_n = jnp.zeros((N_MLSTM, bp, ML_HEADS, ML_DK), F32)
    z_m = jnp.zeros((N_MLSTM, bp, ML_HEADS), F32)
    z_gs = jnp.zeros((N_GDN, bp, GDN_V_HEADS, GDN_DK, GDN_DV), F32)
    z_gc = jnp.zeros((N_GDN, bp, CONV_W - 1, GDN_CONV_DIM), x_prompt.dtype)
    pos_p = jnp.arange(tp)
    y_prompt, ret_p, mc_p, mn_p, mm_p, gs_p, gc_p = _trunk(
        x_prompt, p_prompt, z_ret, z_c, z_n, z_m, z_gs, z_gc, pos_p, *weights)
    pos_s = PAST_LEN + jnp.arange(x_sample.shape[1])
    y_sample, ret_s, mc_s, mn_s, mm_s, gs_s, gc_s = _trunk(
        x_sample, p_sample, state_ret_S, state_mlstm_C, state_mlstm_n, state_mlstm_m,
        state_gdn_S, state_gdn_conv, pos_s, *weights)
    return (y_prompt, y_sample, ret_p, mc_p, mn_p, mm_p, gs_p, gc_p,
            ret_s, mc_s, mn_s, mm_s, gs_s, gc_s)
```

```python
import functools
import math

import jax
import jax.numpy as jnp
from jax import lax
from jax.experimental import pallas as pl
from jax.experimental.pallas import tpu as pltpu

F32 = jnp.float32
BF16 = jnp.bfloat16
EPS = 1e-6
ROPE_BASE = 10000.0
CONV_W = 4
PAST_LEN = 16384
MIB = 1024 * 1024
NEG_INF = float("-inf")


def _params(sem, vmem_mib):
    return pltpu.CompilerParams(dimension_semantics=sem, vmem_limit_bytes=vmem_mib * MIB)


def _bdot(a, b):
    return jnp.dot(a.astype(BF16), b.astype(BF16), preferred_element_type=F32)


def _bdot_nt(a, b):
    return lax.dot_general(a.astype(BF16), b.astype(BF16), (((1,), (1,)), ((), ())),
                           preferred_element_type=F32)


def _bdot_tn(a, b):
    return lax.dot_general(a.astype(BF16), b.astype(BF16), (((0,), (0,)), ((), ())),
                           preferred_element_type=F32)


def _fdot(a, b):
    return jnp.dot(a, b, preferred_element_type=F32, precision=lax.Precision.HIGHEST)


def _silu(x):
    return x * jax.nn.sigmoid(x)


def _rows3(table):
    return table.reshape(table.shape[0], 1, table.shape[1])


def _inproj_body(x_ref, g_ref, w_ref, o_ref, h_ref):
    @pl.when(pl.program_id(1) == 0)
    def _():
        x = x_ref[...]
        ms = jnp.mean(x * x, axis=-1, keepdims=True)
        h_ref[...] = (x * lax.rsqrt(ms + EPS) * g_ref[...]).astype(BF16)

    o_ref[...] = jnp.dot(h_ref[...], w_ref[...].astype(BF16), preferred_element_type=F32)


def _inproj(x2d, g_all, layer, w_all, j, tm, tn):
    m, d = x2d.shape
    n = w_all.shape[-1]
    return pl.pallas_call(
        _inproj_body,
        grid=(m // tm, pl.cdiv(n, tn)),
        in_specs=[pl.BlockSpec((tm, d), lambda i, c: (i, 0)),
                  pl.BlockSpec((None, 1, d), lambda i, c: (layer, 0, 0)),
                  pl.BlockSpec((None, d, tn), lambda i, c: (j, 0, c))],
        out_specs=pl.BlockSpec((tm, tn), lambda i, c: (i, c)),
        out_shape=jax.ShapeDtypeStruct((m, n), F32),
        scratch_shapes=[pltpu.VMEM((tm, d), BF16)],
        compiler_params=_params(("arbitrary", "arbitrary"), 48),
        name="inproj",
    )(x2d, _rows3(g_all), w_all)


def _outproj_body(y_ref, w_ref, o_ref):
    o_ref[...] = jnp.dot(y_ref[...], w_ref[...].astype(BF16), preferred_element_type=F32)


def _outproj(y2d, w_all, j, tm, tn):
    m, k = y2d.shape
    n = w_all.shape[-1]
    return pl.pallas_call(
        _outproj_body,
        grid=(m // tm, n // tn),
        in_specs=[pl.BlockSpec((tm, k), lambda i, c: (i, 0)),
                  pl.BlockSpec((None, k, tn), lambda i, c: (j, 0, c))],
        out_specs=pl.BlockSpec((tm, tn), lambda i, c: (i, c)),
        out_shape=jax.ShapeDtypeStruct((m, n), F32),
        compiler_params=_params(("arbitrary", "arbitrary"), 48),
        name="outproj",
    )(y2d, w_all)


def _post_ple_body(mix_ref, r_ref, g_ref, p_ref, proj_ref, gate_ref, o_ref, r1_ref, r1b_ref, *, tn):
    c = pl.program_id(1)

    @pl.when(c == 0)
    def _():
        mix = mix_ref[...]
        ms = jnp.mean(mix * mix, axis=-1, keepdims=True)
        r1 = r_ref[...] + mix * lax.rsqrt(ms + EPS) * g_ref[...]
        r1b_ref[...] = r1.astype(BF16)
        for cc in range(r1.shape[1] // tn):
            r1_ref[cc] = r1[:, cc * tn:(cc + 1) * tn]

    gate = jnp.dot(r1b_ref[...], gate_ref[...].astype(BF16), preferred_element_type=F32)
    emb = _bdot(p_ref[...], proj_ref[...])
    o_ref[...] = r1_ref[c] + emb * jax.nn.sigmoid(gate)


def _post_ple(mix, r, g_all, p2d, proj_all, gate_all, layer, tm, tn):
    m, d = r.shape
    pd = p2d.shape[-1]
    return pl.pallas_call(
        functools.partial(_post_ple_body, tn=tn),
        grid=(m // tm, d // tn),
        in_specs=[pl.BlockSpec((tm, d), lambda i, c: (i, 0)),
                  pl.BlockSpec((tm, d), lambda i, c: (i, 0)),
                  pl.BlockSpec((None, 1, d), lambda i, c: (layer, 0, 0)),
                  pl.BlockSpec((tm, pd), lambda i, c: (i, 0)),
                  pl.BlockSpec((None, pd, tn), lambda i, c: (layer, 0, c)),
                  pl.BlockSpec((None, d, tn), lambda i, c: (layer, 0, c))],
        out_specs=pl.BlockSpec((tm, tn), lambda i, c: (i, c)),
        out_shape=jax.ShapeDtypeStruct((m, d), F32),
        scratch_shapes=[pltpu.VMEM((d // tn, tm, tn), F32), pltpu.VMEM((tm, d), BF16)],
        compiler_params=_params(("arbitrary", "arbitrary"), 48),
        name="post_ple",
    )(mix, r, _rows3(g_all), p2d, proj_all, gate_all)


def _iota2(n, m, dim):
    return lax.broadcasted_iota(jnp.int32, (n, m), dim)


def _col_to_row(col, eye):
    return jnp.sum(jnp.where(eye, col, 0.0), axis=0, keepdims=True)


def _cumsum_col_row(col, row_i, col_i, eye):
    row = _col_to_row(col, eye)
    c_col = jnp.sum(jnp.where(col_i <= row_i, row, 0.0), axis=1, keepdims=True)
    c_row = jnp.sum(jnp.where(row_i <= col_i, col, 0.0), axis=0, keepdims=True)
    return c_col, c_row


def _pick_lane(blk, lane_iota, idx):
    return jnp.sum(jnp.where(lane_iota == idx, blk, 0.0), axis=1, keepdims=True)


def _head_norm_gate(o, gain, z, center):
    if center:
        o = o - jnp.mean(o, axis=-1, keepdims=True)
    y = o * lax.rsqrt(jnp.mean(o * o, axis=-1, keepdims=True) + EPS) * gain
    return (y * _silu(z)).astype(BF16)


def _ret_body(*refs, L, HB, NC, DK, DV, has_s0, has_prev):
    lg_ref, q_ref, k_ref, v_ref, z_ref, cos_ref, sin_ref, gain_ref = refs[:8]
    s0_ref = refs[8] if has_s0 else None
    y_ref, so_ref, s_scr = refs[8 + has_s0 + has_prev:]
    hb = pl.program_id(1)
    c = pl.program_id(2)

    @pl.when(c == 0)
    def _():
        if has_s0:
            s_scr[...] = s0_ref[...]
        else:
            s_scr[...] = jnp.zeros_like(s_scr)

    cos = cos_ref[...]
    sin = sin_ref[...]
    half = DK // 2

    def rot(x):
        x1, x2 = x[:, :half], x[:, half:]
        return jnp.concatenate([x1 * cos - x2 * sin, x2 * cos + x1 * sin], axis=-1)

    row_i = _iota2(L, L, 0)
    col_i = _iota2(L, L, 1)
    rel = (row_i - col_i).astype(F32)
    idx = _iota2(L, 1, 0).astype(F32)
    for hh in range(HB):
        lg = lg_ref[hb * HB + hh]
        q = rot(q_ref[0, :, hh * DK:(hh + 1) * DK])
        k = rot(k_ref[0, :, hh * DK:(hh + 1) * DK]) * DK ** -0.5
        v = v_ref[0, :, hh * DV:(hh + 1) * DV]
        s = s_scr[hh]
        decay = jnp.exp(jnp.where(rel >= 0, lg * rel, NEG_INF))
        scores = _bdot_nt(q, k) * decay
        q_in = q * jnp.exp(lg * (idx + 1.0))
        o = _bdot(scores, v) + _bdot(q_in, s)
        k_out = k * jnp.exp(lg * (L - 1.0 - idx))
        s_new = jnp.exp(jnp.full((1, 1), L, F32) * lg) * s + _bdot_tn(k_out, v)
        s_scr[hh] = s_new
        y_ref[0, :, hh * DV:(hh + 1) * DV] = _head_norm_gate(
            o, gain_ref[:, hh * DV:(hh + 1) * DV], z_ref[0, :, hh * DV:(hh + 1) * DV], True)

    @pl.when(c == NC - 1)
    def _():
        so_ref[...] = s_scr[...]


def _retention(proj3, s0_all, j, cos, sin, gain_all, so_prev, n_layers, L, HB):
    b, t, _ = proj3.shape
    h, dv = 8, gain_all.shape[-1] // 8
    dk = 256
    nc = t // L
    hg = h // HB
    lg = jnp.log1p(-jnp.exp2(-5.0 - jnp.arange(h, dtype=F32)))
    has_s0 = s0_all is not None
    in_specs = [pl.BlockSpec(memory_space=pltpu.SMEM),
                pl.BlockSpec((1, L, HB * dk), lambda bi, hi, ci: (bi, ci, hi)),
                pl.BlockSpec((1, L, HB * dk), lambda bi, hi, ci: (bi, ci, hg + hi)),
                pl.BlockSpec((1, L, HB * dv), lambda bi, hi, ci: (bi, ci, (2 * h * dk) // (HB * dv) + hi)),
                pl.BlockSpec((1, L, HB * dv), lambda bi, hi, ci: (bi, ci, (2 * h * dk + h * dv) // (HB * dv) + hi)),
                pl.BlockSpec((L, dk // 2), lambda bi, hi, ci: (ci, 0)),
                pl.BlockSpec((L, dk // 2), lambda bi, hi, ci: (ci, 0)),
                pl.BlockSpec((None, 1, HB * dv), lambda bi, hi, ci: (j, 0, hi))]
    args = [lg, proj3, proj3, proj3, proj3, cos, sin, _rows3(gain_all)]
    st_spec = pl.BlockSpec((None, None, HB, dk, dv), lambda bi, hi, ci: (j, bi, hi, 0, 0))
    if has_s0:
        in_specs.append(st_spec)
        args.append(s0_all)
    aliases = {}
    if so_prev is not None:
        in_specs.append(pl.BlockSpec(memory_space=pl.ANY))
        args.append(so_prev)
        aliases = {len(args) - 1: 1}
    return pl.pallas_call(
        functools.partial(_ret_body, L=L, HB=HB, NC=nc, DK=dk, DV=dv, has_s0=has_s0,
                          has_prev=so_prev is not None),
        grid=(b, hg, nc),
        in_specs=in_specs,
        out_specs=[pl.BlockSpec((1, L, HB * dv), lambda bi, hi, ci: (bi, ci, hi)), st_spec],
        out_shape=[jax.ShapeDtypeStruct((b, t, h * dv), BF16),
                   jax.ShapeDtypeStruct((n_layers, b, h, dk, dv), F32)],
        scratch_shapes=[pltpu.VMEM((HB, dk, dv), F32)],
        input_output_aliases=aliases,
        compiler_params=_params(("arbitrary", "arbitrary", "arbitrary"), 48),
        name="retention",
    )(*args)


def _mlstm_body(*refs, L, HB, NC, DK, DV, H, has_s0):
    if has_s0:
        (bg_ref, q_ref, k_ref, v_ref, og_ref, z_ref, gt_ref, gain_ref, c0_ref, n0_ref, m0_ref,
         y_ref, co_ref, no_ref, mo_ref, c_scr, n_scr, m_scr) = refs
    else:
        (bg_ref, q_ref, k_ref, v_ref, og_ref, z_ref, gt_ref, gain_ref,
         y_ref, co_ref, no_ref, mo_ref, c_scr, n_scr, m_scr) = refs
    hb = pl.program_id(1)
    c = pl.program_id(2)

    @pl.when(c == 0)
    def _():
        if has_s0:
            c_scr[...] = c0_ref[...]
            n_scr[...] = n0_ref[...]
            m_scr[...] = m0_ref[...]
        else:
            c_scr[...] = jnp.zeros_like(c_scr)
            n_scr[...] = jnp.zeros_like(n_scr)
            m_scr[...] = jnp.zeros_like(m_scr)

    row_i = _iota2(L, L, 0)
    col_i = _iota2(L, L, 1)
    eye = row_i == col_i
    causal = row_i >= col_i
    lane = _iota2(L, 128, 1)
    gt = gt_ref[0]
    for hh in range(HB):
        head = hb * HB + hh
        ig = _pick_lane(gt, lane, head) + bg_ref[head]
        fg = _pick_lane(gt, lane, H + head) + bg_ref[H + head]
        lf = jax.nn.log_sigmoid(fg)
        b_col, b_row = _cumsum_col_row(lf, row_i, col_i, eye)
        i_row = _col_to_row(ig, eye)
        q = q_ref[0, :, hh * DK:(hh + 1) * DK] * DK ** -0.5
        k = k_ref[0, :, hh * DK:(hh + 1) * DK]
        v = v_ref[0, :, hh * DV:(hh + 1) * DV]
        cm = c_scr[hh]
        nv = n_scr[hh]
        m_prev = m_scr[hh]
        dlog = jnp.where(causal, b_col - b_row + i_row, NEG_INF)
        inter = b_col + m_prev
        mt = jnp.maximum(inter, jnp.max(dlog, axis=1, keepdims=True))
        s = _bdot_nt(q, k) * jnp.exp(dlog - mt)
        wi = jnp.exp(inter - mt)
        num = _bdot(s, v) + wi * _bdot(q, cm)
        den = jnp.sum(s, axis=1, keepdims=True) + wi * jnp.sum(q * nv, axis=1, keepdims=True)
        ht = num / jnp.maximum(jnp.abs(den), jnp.exp(-mt))
        m_new = mt[L - 1:L, :]
        b_last = b_col[L - 1:L, :]
        w_last = jnp.exp(b_last - b_col + ig - m_new)
        dec = jnp.exp(b_last + m_prev - m_new)
        kw = k * w_last
        c_scr[hh] = dec * cm + _bdot_tn(kw, v)
        n_scr[hh] = dec * nv + jnp.sum(kw, axis=0, keepdims=True)
        m_scr[hh] = m_new
        hcell = ht * jax.nn.sigmoid(og_ref[0, :, hh * DV:(hh + 1) * DV])
        y_ref[0, :, hh * DV:(hh + 1) * DV] = _head_norm_gate(
            hcell, gain_ref[:, hh * DV:(hh + 1) * DV], z_ref[0, :, hh * DV:(hh + 1) * DV], True)

    @pl.when(c == NC - 1)
    def _():
        co_ref[...] = c_scr[...]
        no_ref[...] = n_scr[...]
        mo_ref[...] = m_scr[...]


def _mlstm(proj3, c0_all, n0_all, m0_all, j, bgate_all, gain_all, L, HB):
    b, t, n_in = proj3.shape
    h, dk = 8, 256
    dv = gain_all.shape[-1] // h
    nc = t // L
    hg = h // HB
    has_s0 = c0_all is not None
    v_off = 2 * h * dk
    gt_blk = (v_off + 3 * h * dv) // 128
    in_specs = [pl.BlockSpec(memory_space=pltpu.SMEM),
                pl.BlockSpec((1, L, HB * dk), lambda bi, hi, ci: (bi, ci, hi)),
                pl.BlockSpec((1, L, HB * dk), lambda bi, hi, ci: (bi, ci, hg + hi)),
                pl.BlockSpec((1, L, HB * dv), lambda bi, hi, ci: (bi, ci, v_off // (HB * dv) + hi)),
                pl.BlockSpec((1, L, HB * dv), lambda bi, hi, ci: (bi, ci, (v_off + h * dv) // (HB * dv) + hi)),
                pl.BlockSpec((1, L, HB * dv), lambda bi, hi, ci: (bi, ci, (v_off + 2 * h * dv) // (HB * dv) + hi)),
                pl.BlockSpec((1, L, 128), lambda bi, hi, ci: (bi, ci, gt_blk)),
                pl.BlockSpec((None, 1, HB * dv), lambda bi, hi, ci: (j, 0, hi))]
    args = [bgate_all[j], proj3, proj3, proj3, proj3, proj3, proj3, _rows3(gain_all)]
    c_spec = pl.BlockSpec((None, None, HB, dk, dv), lambda bi, hi, ci: (j, bi, hi, 0, 0))
    n_spec = pl.BlockSpec((None, None, HB, 1, dk), lambda bi, hi, ci: (j, bi, hi, 0, 0))
    m_spec = pl.BlockSpec((None, None, HB, 1, 1), lambda bi, hi, ci: (j, bi, hi, 0, 0))
    nl = 1
    if has_s0:
        assert c0_all.shape[0] == nl
        in_specs += [c_spec, n_spec, m_spec]
        args += [c0_all, n0_all.reshape(nl, b, h, 1, dk), m0_all.reshape(nl, b, h, 1, 1)]
    y, co, no, mo = pl.pallas_call(
        functools.partial(_mlstm_body, L=L, HB=HB, NC=nc, DK=dk, DV=dv, H=h, has_s0=has_s0),
        grid=(b, hg, nc),
        in_specs=in_specs,
        out_specs=[pl.BlockSpec((1, L, HB * dv), lambda bi, hi, ci: (bi, ci, hi)), c_spec, n_spec, m_spec],
        out_shape=[jax.ShapeDtypeStruct((b, t, h * dv), BF16),
                   jax.ShapeDtypeStruct((nl, b, h, dk, dv), F32),
                   jax.ShapeDtypeStruct((nl, b, h, 1, dk), F32),
                   jax.ShapeDtypeStruct((nl, b, h, 1, 1), F32)],
        scratch_shapes=[pltpu.VMEM((HB, dk, dv), F32), pltpu.VMEM((HB, 1, dk), F32),
                        pltpu.VMEM((HB, 1, 1), F32)],
        compiler_params=_params(("arbitrary", "arbitrary", "arbitrary"), 48),
        name="mlstm",
    )(*args)
    return y, co, no.reshape(nl, b, h, dk), mo.reshape(nl, b, h)


def _merge_masks(row_i, col_i, L):
    masks = []
    s = 1
    while s < L:
        masks.append(((row_i // (2 * s)) == (col_i // (2 * s))) & ((row_i // s) > (col_i // s)))
        s *= 2
    return masks


def _unit_lower_inverse(a_strict, eye, masks):
    x = jnp.where(eye, 1.0, 0.0)
    for mask in masks:
        e = jnp.where(mask, a_strict, 0.0)
        x = x - _fdot(x, _fdot(e, x))
    return x


def _gdn_body(*refs, L, HBK, NC, DK, DV, REP, HV, has_s0):
    if has_s0:
        (al_ref, dt_ref, qx_ref, kx_ref, vx_ref, z_ref, gt_ref, cwq_ref, cwk_ref, cwv_ref, gain_ref,
         cq0_ref, ck0_ref, cv0_ref, s0_ref, y_ref, so_ref, s_scr, eq_scr, ek_scr, ev_scr) = refs
    else:
        (al_ref, dt_ref, qx_ref, kx_ref, vx_ref, z_ref, gt_ref, cwq_ref, cwk_ref, cwv_ref, gain_ref,
         y_ref, so_ref, s_scr, eq_scr, ek_scr, ev_scr) = refs
    hb = pl.program_id(1)
    c = pl.program_id(2)
    pad = 8
    taps = CONV_W - 1

    @pl.when(c == 0)
    def _():
        for scr, c0 in ((eq_scr, cq0_ref if has_s0 else None), (ek_scr, ck0_ref if has_s0 else None),
                        (ev_scr, cv0_ref if has_s0 else None)):
            scr[0:pad, :] = jnp.zeros((pad, scr.shape[1]), F32)
            if has_s0:
                scr[pad - taps:pad, :] = c0[...]
        if has_s0:
            s_scr[...] = s0_ref[...]
        else:
            s_scr[...] = jnp.zeros_like(s_scr)

    def conv(x_ref, scr, cw_ref):
        scr[pad:pad + L, :] = x_ref[0]
        acc = scr[pl.ds(pad - taps, L), :] * cw_ref[0:1, :]
        for w in range(1, CONV_W):
            acc = acc + scr[pl.ds(pad - taps + w, L), :] * cw_ref[w:w + 1, :]
        if NC > 1:
            scr[0:pad, :] = scr[L:L + pad, :]
        return _silu(acc)

    cq = conv(qx_ref, eq_scr, cwq_ref)
    ck = conv(kx_ref, ek_scr, cwk_ref)
    cv = conv(vx_ref, ev_scr, cwv_ref)

    row_i = _iota2(L, L, 0)
    col_i = _iota2(L, L, 1)
    eye = row_i == col_i
    incl = row_i >= col_i
    strict = row_i > col_i
    lane = _iota2(L, 128, 1)
    masks = _merge_masks(row_i, col_i, L)
    gt = gt_ref[0]
    for kh in range(HBK):
        q = cq[:, kh * DK:(kh + 1) * DK]
        k = ck[:, kh * DK:(kh + 1) * DK]
        q = q * lax.rsqrt(jnp.sum(q * q, axis=-1, keepdims=True) + EPS) * DK ** -0.5
        k = k * lax.rsqrt(jnp.sum(k * k, axis=-1, keepdims=True) + EPS)
        kk = _bdot_nt(k, k)
        qk = _bdot_nt(q, k)
        for r in range(REP):
            vh_local = kh * REP + r
            head = (hb * HBK + kh) * REP + r
            v = cv[:, vh_local * DV:(vh_local + 1) * DV]
            beta = jax.nn.sigmoid(_pick_lane(gt, lane, head))
            a_neg = -jnp.exp(jnp.full((1, 1), al_ref[head], F32))
            g = a_neg * jax.nn.softplus(_pick_lane(gt, lane, HV + head) + dt_ref[head])
            g_col, g_row = _cumsum_col_row(g, row_i, col_i, eye)
            decay = jnp.exp(jnp.where(incl, g_col - g_row, NEG_INF))
            a = jnp.where(strict, beta * kk * decay, 0.0)
            x = _unit_lower_inverse(a, eye, masks)
            s = s_scr[vh_local]
            eg = jnp.exp(g_col)
            rhs = beta * v - (beta * eg) * _bdot(k, s)
            u = _fdot(x, rhs)
            o = eg * _bdot(q, s) + _bdot(qk * decay, u)
            g_last = g_col[L - 1:L, :]
            kw = k * jnp.exp(g_last - g_col)
            s_scr[vh_local] = jnp.exp(g_last) * s + _bdot_tn(kw, u)
            y_ref[0, :, vh_local * DV:(vh_local + 1) * DV] = _head_norm_gate(
                o, gain_ref[...], z_ref[0, :, vh_local * DV:(vh_local + 1) * DV], False)

    @pl.when(c == NC - 1)
    def _():
        so_ref[...] = s_scr[...]


def _gdn(proj3, s0_all, conv0_all, j, conv_w_all, a_log_all, dt_bias_all, gain_all, L, HBK):
    b, t, n_in = proj3.shape
    dk = dv = gain_all.shape[-1]
    hv = a_log_all.shape[-1]
    cdim = conv_w_all.shape[-1]
    hk = (cdim - hv * dv) // (2 * dk)
    rep = hv // hk
    assert 2 * hv <= 128 and rep * hk == hv
    nc = t // L
    hg = hk // HBK
    has_s0 = s0_all is not None
    wq, wv = HBK * dk, HBK * rep * dv
    k_blk = (hk * dk) // wq
    v_blk = (2 * hk * dk) // wv
    z_blk = cdim // wv
    gt_blk = (cdim + hv * dv) // 128

    def cspec(width, off):
        return pl.BlockSpec((None, CONV_W, width), lambda bi, hi, ci: (j, 0, off + hi))

    def c0spec(width, off):
        return pl.BlockSpec((None, None, CONV_W - 1, width), lambda bi, hi, ci: (j, bi, 0, off + hi))

    in_specs = [pl.BlockSpec(memory_space=pltpu.SMEM),
                pl.BlockSpec(memory_space=pltpu.SMEM),
                pl.BlockSpec((1, L, wq), lambda bi, hi, ci: (bi, ci, hi)),
                pl.BlockSpec((1, L, wq), lambda bi, hi, ci: (bi, ci, k_blk + hi)),
                pl.BlockSpec((1, L, wv), lambda bi, hi, ci: (bi, ci, v_blk + hi)),
                pl.BlockSpec((1, L, wv), lambda bi, hi, ci: (bi, ci, z_blk + hi)),
                pl.BlockSpec((1, L, 128), lambda bi, hi, ci: (bi, ci, gt_blk)),
                cspec(wq, 0), cspec(wq, k_blk), cspec(wv, v_blk),
                pl.BlockSpec((None, 1, dv), lambda bi, hi, ci: (j, 0, 0))]
    args = [a_log_all[j], dt_bias_all[j], proj3, proj3, proj3, proj3, proj3,
            conv_w_all, conv_w_all, conv_w_all, _rows3(gain_all)]
    st_spec = pl.BlockSpec((None, None, HBK * rep, dk, dv), lambda bi, hi, ci: (j, bi, hi, 0, 0))
    if has_s0:
        in_specs += [c0spec(wq, 0), c0spec(wq, k_blk), c0spec(wv, v_blk), st_spec]
        args += [conv0_all, conv0_all, conv0_all, s0_all]
    y, so = pl.pallas_call(
        functools.partial(_gdn_body, L=L, HBK=HBK, NC=nc, DK=dk, DV=dv, REP=rep, HV=hv, has_s0=has_s0),
        grid=(b, hg, nc),
        in_specs=in_specs,
        out_specs=[pl.BlockSpec((1, L, wv), lambda bi, hi, ci: (bi, ci, hi)), st_spec],
        out_shape=[jax.ShapeDtypeStruct((b, t, hv * dv), BF16),
                   jax.ShapeDtypeStruct((1, b, hv, dk, dv), F32)],
        scratch_shapes=[pltpu.VMEM((HBK * rep, dk, dv), F32),
                        pltpu.VMEM((L + 8, wq), F32), pltpu.VMEM((L + 8, wq), F32),
                        pltpu.VMEM((L + 8, wv), F32)],
        compiler_params=_params(("arbitrary", "arbitrary", "arbitrary"), 48),
        name="gdn",
    )(*args)
    new_conv = proj3[:, t - (CONV_W - 1):, :cdim][None]
    return y, so, new_conv


def _rope_tables(pos, dk):
    half = dk // 2
    inv = ROPE_BASE ** (-jnp.arange(half, dtype=F32) / half)
    ang = pos.astype(F32)[:, None] * inv[None, :]
    return jnp.cos(ang), jnp.sin(ang)


def _trunk(x, p, states, pos, w, cfg):
    (norm_pre, norm_post, ple_proj, ple_gate, ret_w_in, ret_head_norm, ret_w_out,
     mlstm_w_in, mlstm_b_gate, mlstm_head_norm, mlstm_w_out,
     gdn_w_in, gdn_conv_w, gdn_a_log, gdn_dt_bias, gdn_head_norm, gdn_w_out) = w
    ret_s, ml_c, ml_n, ml_m, gdn_s, gdn_conv = states
    b, t, d = x.shape
    depth = norm_pre.shape[0]
    n_ret = ret_w_in.shape[0]
    m = b * t
    tm = cfg["tm"]
    cos, sin = _rope_tables(pos, 256)
    r = x.reshape(m, d)
    ret_out = None
    outs = {}
    for i in range(depth):
        kind, j = i % 3, i // 3
        if kind == 0:
            proj = _inproj(r, norm_pre, i, ret_w_in, j, tm, 512).reshape(b, t, -1)
            y, ret_out = _retention(proj, ret_s, j, cos, sin, ret_head_norm, ret_out, n_ret,
                                    cfg["ret_L"], cfg["ret_HB"])
            w_out = ret_w_out
        elif kind == 1:
            proj = _inproj(r, norm_pre, i, mlstm_w_in, j, tm, 512).reshape(b, t, -1)
            y, outs["c"], outs["n"], outs["m"] = _mlstm(
                proj, ml_c, ml_n, ml_m, j, mlstm_b_gate, mlstm_head_norm, cfg["ml_L"], cfg["ml_HB"])
            w_out = mlstm_w_out
        else:
            proj = _inproj(r, norm_pre, i, gdn_w_in, j, tm, 512).reshape(b, t, -1)
            y, outs["gs"], outs["gc"] = _gdn(
                proj, gdn_s, gdn_conv, j, gdn_conv_w, gdn_a_log, gdn_dt_bias, gdn_head_norm,
                cfg["gdn_L"], cfg["gdn_HBK"])
            w_out = gdn_w_out
        mix = _outproj(y.reshape(m, -1), w_out, j, tm, 512)
        r = _post_ple(mix, r, norm_post, p[i].reshape(m, -1), ple_proj, ple_gate, i, cfg["tm_ple"], 512)
    return (r.reshape(b, t, d), ret_out, outs["c"], outs["n"], outs["m"], outs["gs"], outs["gc"])


_PROMPT_CFG = dict(tm=1024, tm_ple=512, ret_L=256, ret_HB=1, ml_L=256, ml_HB=1, gdn_L=64, gdn_HBK=1)
_SAMPLE_CFG = dict(tm=512, tm_ple=512, ret_L=4, ret_HB=8, ml_L=4, ml_HB=8, gdn_L=4, gdn_HBK=4)


def kernel(x_prompt, x_sample, state_ret_S, state_mlstm_C, state_mlstm_n, state_mlstm_m, state_gdn_S, state_gdn_conv, p_prompt, p_sample, norm_pre, norm_post, ple_proj, ple_gate, ret_w_in, ret_head_norm, ret_w_out, mlstm_w_in, mlstm_b_gate, mlstm_head_norm, mlstm_w_out, gdn_w_in, gdn_conv_w, gdn_a_log, gdn_dt_bias, gdn_head_norm, gdn_w_out):
    w = (norm_pre, norm_post, ple_proj, ple_gate, ret_w_in, ret_head_norm, ret_w_out,
         mlstm_w_in, mlstm_b_gate, mlstm_head_norm, mlstm_w_out,
         gdn_w_in, gdn_conv_w, gdn_a_log, gdn_dt_bias, gdn_head_norm, gdn_w_out)
    pos_p = jnp.arange(x_prompt.shape[1])
    yp, ret_p, mc_p, mn_p, mm_p, gs_p, gc_p = _trunk(
        x_prompt, p_prompt, (None,) * 6, pos_p, w, _PROMPT_CFG)
    pos_s = PAST_LEN + jnp.arange(x_sample.shape[1])
    ys, ret_s, mc_s, mn_s, mm_s, gs_s, gc_s = _trunk(
        x_sample, p_sample,
        (state_ret_S, state_mlstm_C, state_mlstm_n, state_mlstm_m, state_gdn_S, state_gdn_conv),
        pos_s, w, _SAMPLE_CFG)
    return (yp, ys, ret_p, mc_p, mn_p, mm_p, gs_p, gc_p, ret_s, mc_s, mn_s, mm_s, gs_s, gc_s)
```

```python
import functools
import math

import jax
import jax.numpy as jnp
from jax import lax
from jax.experimental import pallas as pl
from jax.experimental.pallas import tpu as pltpu

F32 = jnp.float32
BF16 = jnp.bfloat16
EPS = 1e-6
ROPE_BASE = 10000.0
CONV_W = 4
PAST_LEN = 16384
MIB = 1024 * 1024
NEG_INF = float("-inf")


def _params(sem, vmem_mib):
    return pltpu.CompilerParams(dimension_semantics=sem, vmem_limit_bytes=vmem_mib * MIB)


def _bdot(a, b):
    return jnp.dot(a.astype(BF16), b.astype(BF16), preferred_element_type=F32)


def _bdot_nt(a, b):
    return lax.dot_general(a.astype(BF16), b.astype(BF16), (((1,), (1,)), ((), ())),
                           preferred_element_type=F32)


def _bdot_tn(a, b):
    return lax.dot_general(a.astype(BF16), b.astype(BF16), (((0,), (0,)), ((), ())),
                           preferred_element_type=F32)


def _silu(x):
    return x * jax.nn.sigmoid(x)


def _rows3(table):
    return table.reshape(table.shape[0], 1, table.shape[1])


def _inproj_body(x_ref, g_ref, w_ref, o_ref, h_ref):
    @pl.when(pl.program_id(1) == 0)
    def _():
        x = x_ref[...]
        ms = jnp.mean(x * x, axis=-1, keepdims=True)
        h_ref[...] = (x * lax.rsqrt(ms + EPS) * g_ref[...]).astype(BF16)

    o_ref[...] = jnp.dot(h_ref[...], w_ref[...].astype(BF16), preferred_element_type=F32)


def _inproj(x2d, g_all, layer, w_all, j, tm, tn):
    m, d = x2d.shape
    n = w_all.shape[-1]
    return pl.pallas_call(
        _inproj_body,
        grid=(m // tm, pl.cdiv(n, tn)),
        in_specs=[pl.BlockSpec((tm, d), lambda i, c: (i, 0)),
                  pl.BlockSpec((None, 1, d), lambda i, c: (layer, 0, 0)),
                  pl.BlockSpec((None, d, tn), lambda i, c: (j, 0, c))],
        out_specs=pl.BlockSpec((tm, tn), lambda i, c: (i, c)),
        out_shape=jax.ShapeDtypeStruct((m, n), F32),
        scratch_shapes=[pltpu.VMEM((tm, d), BF16)],
        compiler_params=_params(("arbitrary", "arbitrary"), 48),
        name="inproj",
    )(x2d, _rows3(g_all), w_all)


def _outproj_body(y_ref, w_ref, o_ref):
    o_ref[...] = jnp.dot(y_ref[...], w_ref[...].astype(BF16), preferred_element_type=F32)


def _outproj(y2d, w_all, j, tm, tn):
    m, k = y2d.shape
    n = w_all.shape[-1]
    return pl.pallas_call(
        _outproj_body,
        grid=(m // tm, n // tn),
        in_specs=[pl.BlockSpec((tm, k), lambda i, c: (i, 0)),
                  pl.BlockSpec((None, k, tn), lambda i, c: (j, 0, c))],
        out_specs=pl.BlockSpec((tm, tn), lambda i, c: (i, c)),
        out_shape=jax.ShapeDtypeStruct((m, n), F32),
        compiler_params=_params(("arbitrary", "arbitrary"), 48),
        name="outproj",
    )(y2d, w_all)


def _post_ple_body(mix_ref, r_ref, g_ref, p_ref, proj_ref, gate_ref, o_ref, r1_ref, r1b_ref, *, tn):
    c = pl.program_id(1)

    @pl.when(c == 0)
    def _():
        mix = mix_ref[...]
        ms = jnp.mean(mix * mix, axis=-1, keepdims=True)
        r1 = r_ref[...] + mix * lax.rsqrt(ms + EPS) * g_ref[...]
        r1b_ref[...] = r1.astype(BF16)
        for cc in range(r1.shape[1] // tn):
            r1_ref[cc] = r1[:, cc * tn:(cc + 1) * tn]

    gate = jnp.dot(r1b_ref[...], gate_ref[...].astype(BF16), preferred_element_type=F32)
    emb = _bdot(p_ref[...], proj_ref[...])
    o_ref[...] = r1_ref[c] + emb * jax.nn.sigmoid(gate)


def _post_ple(mix, r, g_all, p2d, proj_all, gate_all, layer, tm, tn):
    m, d = r.shape
    pd = p2d.shape[-1]
    return pl.pallas_call(
        functools.partial(_post_ple_body, tn=tn),
        grid=(m // tm, d // tn),
        in_specs=[pl.BlockSpec((tm, d), lambda i, c: (i, 0)),
                  pl.BlockSpec((tm, d), lambda i, c: (i, 0)),
                  pl.BlockSpec((None, 1, d), lambda i, c: (layer, 0, 0)),
                  pl.BlockSpec((tm, pd), lambda i, c: (i, 0)),
                  pl.BlockSpec((None, pd, tn), lambda i, c: (layer, 0, c)),
                  pl.BlockSpec((None, d, tn), lambda i, c: (layer, 0, c))],
        out_specs=pl.BlockSpec((tm, tn), lambda i, c: (i, c)),
        out_shape=jax.ShapeDtypeStruct((m, d), F32),
        scratch_shapes=[pltpu.VMEM((d // tn, tm, tn), F32), pltpu.VMEM((tm, d), BF16)],
        compiler_params=_params(("arbitrary", "arbitrary"), 48),
        name="post_ple",
    )(mix, r, _rows3(g_all), p2d, proj_all, gate_all)


def _iota2(n, m, dim):
    return lax.broadcasted_iota(jnp.int32, (n, m), dim)


def _col_to_row(col, eye):
    return jnp.sum(jnp.where(eye, col, 0.0), axis=0, keepdims=True)


def _cumsum_col_row(col, row_i, col_i, eye):
    row = _col_to_row(col, eye)
    c_col = jnp.sum(jnp.where(col_i <= row_i, row, 0.0), axis=1, keepdims=True)
    c_row = jnp.sum(jnp.where(row_i <= col_i, col, 0.0), axis=0, keepdims=True)
    return c_col, c_row


def _pick_lane(blk, lane_iota, idx):
    return jnp.sum(jnp.where(lane_iota == idx, blk, 0.0), axis=1, keepdims=True)


def _head_norm_gate(o, gain, z, center):
    if center:
        o = o - jnp.mean(o, axis=-1, keepdims=True)
    y = o * lax.rsqrt(jnp.mean(o * o, axis=-1, keepdims=True) + EPS) * gain
    return (y * _silu(z)).astype(BF16)


def _ret_body(*refs, L, HB, NC, DK, DV, has_s0, has_prev):
    lg_ref, q_ref, k_ref, v_ref, z_ref, cos_ref, sin_ref, gain_ref = refs[:8]
    s0_ref = refs[8] if has_s0 else None
    y_ref, so_ref, s_scr = refs[8 + has_s0 + has_prev:]
    hb = pl.program_id(1)
    c = pl.program_id(2)

    @pl.when(c == 0)
    def _():
        if has_s0:
            s_scr[...] = s0_ref[...]
        else:
            s_scr[...] = jnp.zeros_like(s_scr)

    cos = cos_ref[...]
    sin = sin_ref[...]
    half = DK // 2

    def rot(x):
        x1, x2 = x[:, :half], x[:, half:]
        return jnp.concatenate([x1 * cos - x2 * sin, x2 * cos + x1 * sin], axis=-1)

    row_i = _iota2(L, L, 0)
    col_i = _iota2(L, L, 1)
    rel = (row_i - col_i).astype(F32)
    idx = _iota2(L, 1, 0).astype(F32)
    for hh in range(HB):
        lg = lg_ref[hb * HB + hh]
        q = rot(q_ref[0, :, hh * DK:(hh + 1) * DK])
        k = rot(k_ref[0, :, hh * DK:(hh + 1) * DK]) * DK ** -0.5
        v = v_ref[0, :, hh * DV:(hh + 1) * DV]
        s = s_scr[hh]
        decay = jnp.exp(jnp.where(rel >= 0, lg * rel, NEG_INF))
        scores = _bdot_nt(q, k) * decay
        q_in = q * jnp.exp(lg * (idx + 1.0))
        o = _bdot(scores, v) + _bdot(q_in, s)
        k_out = k * jnp.exp(lg * (L - 1.0 - idx))
        s_new = jnp.exp(jnp.full((1, 1), L, F32) * lg) * s + _bdot_tn(k_out, v)
        s_scr[hh] = s_new
        y_ref[0, :, hh * DV:(hh + 1) * DV] = _head_norm_gate(
            o, gain_ref[:, hh * DV:(hh + 1) * DV], z_ref[0, :, hh * DV:(hh + 1) * DV], True)

    @pl.when(c == NC - 1)
    def _():
        so_ref[...] = s_scr[...]


def _retention(proj3, s0_all, j, cos, sin, gain_all, so_prev, n_layers, L, HB):
    b, t, _ = proj3.shape
    h, dv = 8, gain_all.shape[-1] // 8
    dk = 256
    nc = t // L
    hg = h // HB
    lg = jnp.log1p(-jnp.exp2(-5.0 - jnp.arange(h, dtype=F32)))
    has_s0 = s0_all is not None
    in_specs = [pl.BlockSpec(memory_space=pltpu.SMEM),
                pl.BlockSpec((1, L, HB * dk), lambda bi, hi, ci: (bi, ci, hi)),
                pl.BlockSpec((1, L, HB * dk), lambda bi, hi, ci: (bi, ci, hg + hi)),
                pl.BlockSpec((1, L, HB * dv), lambda bi, hi, ci: (bi, ci, (2 * h * dk) // (HB * dv) + hi)),
                pl.BlockSpec((1, L, HB * dv), lambda bi, hi, ci: (bi, ci, (2 * h * dk + h * dv) // (HB * dv) + hi)),
                pl.BlockSpec((L, dk // 2), lambda bi, hi, ci: (ci, 0)),
                pl.BlockSpec((L, dk // 2), lambda bi, hi, ci: (ci, 0)),
                pl.BlockSpec((None, 1, HB * dv), lambda bi, hi, ci: (j, 0, hi))]
    args = [lg, proj3, proj3, proj3, proj3, cos, sin, _rows3(gain_all)]
    st_spec = pl.BlockSpec((None, None, HB, dk, dv), lambda bi, hi, ci: (j, bi, hi, 0, 0))
    if has_s0:
        in_specs.append(st_spec)
        args.append(s0_all)
    aliases = {}
    if so_prev is not None:
        in_specs.append(pl.BlockSpec(memory_space=pl.ANY))
        args.append(so_prev)
        aliases = {len(args) - 1: 1}
    return pl.pallas_call(
        functools.partial(_ret_body, L=L, HB=HB, NC=nc, DK=dk, DV=dv, has_s0=has_s0,
                          has_prev=so_prev is not None),
        grid=(b, hg, nc),
        in_specs=in_specs,
        out_specs=[pl.BlockSpec((1, L, HB * dv), lambda bi, hi, ci: (bi, ci, hi)), st_spec],
        out_shape=[jax.ShapeDtypeStruct((b, t, h * dv), BF16),
                   jax.ShapeDtypeStruct((n_layers, b, h, dk, dv), F32)],
        scratch_shapes=[pltpu.VMEM((HB, dk, dv), F32)],
        input_output_aliases=aliases,
        compiler_params=_params(("arbitrary", "arbitrary", "arbitrary"), 48),
        name="retention",
    )(*args)


def _mlstm_body(*refs, L, HB, NC, DK, DV, H, has_s0):
    if has_s0:
        (bg_ref, q_ref, k_ref, v_ref, og_ref, z_ref, gt_ref, gain_ref, c0_ref, n0_ref, m0_ref,
         y_ref, co_ref, no_ref, mo_ref, c_scr, n_scr, m_scr) = refs
    else:
        (bg_ref, q_ref, k_ref, v_ref, og_ref, z_ref, gt_ref, gain_ref,
         y_ref, co_ref, no_ref, mo_ref, c_scr, n_scr, m_scr) = refs
    hb = pl.program_id(1)
    c = pl.program_id(2)

    @pl.when(c == 0)
    def _():
        if has_s0:
            c_scr[...] = c0_ref[...]
            n_scr[...] = n0_ref[...]
            m_scr[...] = m0_ref[...]
        else:
            c_scr[...] = jnp.zeros_like(c_scr)
            n_scr[...] = jnp.zeros_like(n_scr)
            m_scr[...] = jnp.zeros_like(m_scr)

    row_i = _iota2(L, L, 0)
    col_i = _iota2(L, L, 1)
    eye = row_i == col_i
    causal = row_i >= col_i
    lane = _iota2(L, 128, 1)
    gt = gt_ref[0]
    for hh in range(HB):
        head = hb * HB + hh
        ig = _pick_lane(gt, lane, head) + bg_ref[head]
        fg = _pick_lane(gt, lane, H + head) + bg_ref[H + head]
        lf = jax.nn.log_sigmoid(fg)
        b_col, b_row = _cumsum_col_row(lf, row_i, col_i, eye)
        i_row = _col_to_row(ig, eye)
        q = q_ref[0, :, hh * DK:(hh + 1) * DK] * DK ** -0.5
        k = k_ref[0, :, hh * DK:(hh + 1) * DK]
        v = v_ref[0, :, hh * DV:(hh + 1) * DV]
        cm = c_scr[hh]
        nv = n_scr[hh]
        m_prev = m_scr[hh]
        dlog = jnp.where(causal, b_col - b_row + i_row, NEG_INF)
        inter = b_col + m_prev
        mt = jnp.maximum(inter, jnp.max(dlog, axis=1, keepdims=True))
        s = _bdot_nt(q, k) * jnp.exp(dlog - mt)
        wi = jnp.exp(inter - mt)
        num = _bdot(s, v) + wi * _bdot(q, cm)
        den = jnp.sum(s, axis=1, keepdims=True) + wi * jnp.sum(q * nv, axis=1, keepdims=True)
        ht = num / jnp.maximum(jnp.abs(den), jnp.exp(-mt))
        m_new = mt[L - 1:L, :]
        b_last = b_col[L - 1:L, :]
        w_last = jnp.exp(b_last - b_col + ig - m_new)
        dec = jnp.exp(b_last + m_prev - m_new)
        kw = k * w_last
        c_scr[hh] = dec * cm + _bdot_tn(kw, v)
        n_scr[hh] = dec * nv + jnp.sum(kw, axis=0, keepdims=True)
        m_scr[hh] = m_new
        hcell = ht * jax.nn.sigmoid(og_ref[0, :, hh * DV:(hh + 1) * DV])
        y_ref[0, :, hh * DV:(hh + 1) * DV] = _head_norm_gate(
            hcell, gain_ref[:, hh * DV:(hh + 1) * DV], z_ref[0, :, hh * DV:(hh + 1) * DV], True)

    @pl.when(c == NC - 1)
    def _():
        co_ref[...] = c_scr[...]
        no_ref[...] = n_scr[...]
        mo_ref[...] = m_scr[...]


def _mlstm(proj3, c0_all, n0_all, m0_all, j, bgate_all, gain_all, L, HB):
    b, t, n_in = proj3.shape
    h, dk = 8, 256
    dv = gain_all.shape[-1] // h
    nc = t // L
    hg = h // HB
    has_s0 = c0_all is not None
    v_off = 2 * h * dk
    gt_blk = (v_off + 3 * h * dv) // 128
    in_specs = [pl.BlockSpec(memory_space=pltpu.SMEM),
                pl.BlockSpec((1, L, HB * dk), lambda bi, hi, ci: (bi, ci, hi)),
                pl.BlockSpec((1, L, HB * dk), lambda bi, hi, ci: (bi, ci, hg + hi)),
                pl.BlockSpec((1, L, HB * dv), lambda bi, hi, ci: (bi, ci, v_off // (HB * dv) + hi)),
                pl.BlockSpec((1, L, HB * dv), lambda bi, hi, ci: (bi, ci, (v_off + h * dv) // (HB * dv) + hi)),
                pl.BlockSpec((1, L, HB * dv), lambda bi, hi, ci: (bi, ci, (v_off + 2 * h * dv) // (HB * dv) + hi)),
                pl.BlockSpec((1, L, 128), lambda bi, hi, ci: (bi, ci, gt_blk)),
                pl.BlockSpec((None, 1, HB * dv), lambda bi, hi, ci: (j, 0, hi))]
    args = [bgate_all[j], proj3, proj3, proj3, proj3, proj3, proj3, _rows3(gain_all)]
    c_spec = pl.BlockSpec((None, None, HB, dk, dv), lambda bi, hi, ci: (j, bi, hi, 0, 0))
    n_spec = pl.BlockSpec((None, None, HB, 1, dk), lambda bi, hi, ci: (j, bi, hi, 0, 0))
    m_spec = pl.BlockSpec((None, None, HB, 1, 1), lambda bi, hi, ci: (j, bi, hi, 0, 0))
    nl = 1
    if has_s0:
        assert c0_all.shape[0] == nl
        in_specs += [c_spec, n_spec, m_spec]
        args += [c0_all, n0_all.reshape(nl, b, h, 1, dk), m0_all.reshape(nl, b, h, 1, 1)]
    y, co, no, mo = pl.pallas_call(
        functools.partial(_mlstm_body, L=L, HB=HB, NC=nc, DK=dk, DV=dv, H=h, has_s0=has_s0),
        grid=(b, hg, nc),
        in_specs=in_specs,
        out_specs=[pl.BlockSpec((1, L, HB * dv), lambda bi, hi, ci: (bi, ci, hi)), c_spec, n_spec, m_spec],
        out_shape=[jax.ShapeDtypeStruct((b, t, h * dv), BF16),
                   jax.ShapeDtypeStruct((nl, b, h, dk, dv), F32),
                   jax.ShapeDtypeStruct((nl, b, h, 1, dk), F32),
                   jax.ShapeDtypeStruct((nl, b, h, 1, 1), F32)],
        scratch_shapes=[pltpu.VMEM((HB, dk, dv), F32), pltpu.VMEM((HB, 1, dk), F32),
                        pltpu.VMEM((HB, 1, 1), F32)],
        compiler_params=_params(("arbitrary", "arbitrary", "arbitrary"), 48),
        name="mlstm",
    )(*args)
    return y, co, no.reshape(nl, b, h, dk), mo.reshape(nl, b, h)


def _merge_masks(row_i, col_i, L):
    masks = []
    s = 1
    while s < L:
        masks.append(((row_i // (2 * s)) == (col_i // (2 * s))) & ((row_i // s) > (col_i // s)))
        s *= 2
    return masks


def _unit_lower_inverse(a_strict, eye, masks, mm):
    x = jnp.where(eye, 1.0, 0.0) - jnp.where(masks[0], a_strict, 0.0)
    for mask in masks[1:]:
        e = jnp.where(mask, a_strict, 0.0)
        x = x - mm(x, mm(e, x))
    return x


def _vpu_mm(a, b):
    out = a[:, 0:1] * b[0:1, :]
    for kk in range(1, a.shape[1]):
        out = out + a[:, kk:kk + 1] * b[kk:kk + 1, :]
    return out


def _bmm(a, b):
    return jnp.einsum("gik,gkj->gij", a.astype(BF16), b.astype(BF16), preferred_element_type=F32)


def _bmm_nt(a, b):
    return jnp.einsum("gik,gjk->gij", a.astype(BF16), b.astype(BF16), preferred_element_type=F32)


def _bmm_tn(a, b):
    return jnp.einsum("gki,gkj->gij", a.astype(BF16), b.astype(BF16), preferred_element_type=F32)


def _gdn_body(*refs, L, HBK, NC, DK, DV, REP, HV, has_s0):
    if has_s0:
        (al_ref, dt_ref, qx_ref, kx_ref, vx_ref, z_ref, gt_ref, cwq_ref, cwk_ref, cwv_ref, gain_ref,
         cq0_ref, ck0_ref, cv0_ref, s0_ref, y_ref, so_ref, s_scr, eq_scr, ek_scr, ev_scr) = refs
    else:
        (al_ref, dt_ref, qx_ref, kx_ref, vx_ref, z_ref, gt_ref, cwq_ref, cwk_ref, cwv_ref, gain_ref,
         y_ref, so_ref, s_scr, eq_scr, ek_scr, ev_scr) = refs
    hb = pl.program_id(1)
    c = pl.program_id(2)
    pad = 8
    taps = CONV_W - 1

    @pl.when(c == 0)
    def _():
        for scr, c0 in ((eq_scr, cq0_ref if has_s0 else None), (ek_scr, ck0_ref if has_s0 else None),
                        (ev_scr, cv0_ref if has_s0 else None)):
            scr[0:pad, :] = jnp.zeros((pad, scr.shape[1]), F32)
            if has_s0:
                scr[pad - taps:pad, :] = c0[...]
        if has_s0:
            s_scr[...] = s0_ref[...]
        else:
            s_scr[...] = jnp.zeros_like(s_scr)

    def conv(x_ref, scr, cw_ref):
        scr[pad:pad + L, :] = x_ref[0]
        acc = scr[pl.ds(pad - taps, L), :] * cw_ref[0:1, :]
        for w in range(1, CONV_W):
            acc = acc + scr[pl.ds(pad - taps + w, L), :] * cw_ref[w:w + 1, :]
        if NC > 1:
            scr[0:pad, :] = scr[L:L + pad, :]
        return _silu(acc)

    cq = conv(qx_ref, eq_scr, cwq_ref)
    ck = conv(kx_ref, ek_scr, cwk_ref)
    cv = conv(vx_ref, ev_scr, cwv_ref)

    row_i = _iota2(L, L, 0)
    col_i = _iota2(L, L, 1)
    eye = row_i == col_i
    incl = row_i >= col_i
    strict = row_i > col_i
    lane = _iota2(L, 128, 1)
    masks = _merge_masks(row_i, col_i, L)
    small_mm = _vpu_mm
    gt = gt_ref[0]
    for kh in range(HBK):
        q = cq[:, kh * DK:(kh + 1) * DK]
        k = ck[:, kh * DK:(kh + 1) * DK]
        q = q * lax.rsqrt(jnp.sum(q * q, axis=-1, keepdims=True) + EPS) * DK ** -0.5
        k = k * lax.rsqrt(jnp.sum(k * k, axis=-1, keepdims=True) + EPS)
        kk = _bdot_nt(k, k)
        qk = _bdot_nt(q, k)
        for r in range(REP):
            vh_local = kh * REP + r
            head = (hb * HBK + kh) * REP + r
            v = cv[:, vh_local * DV:(vh_local + 1) * DV]
            beta = jax.nn.sigmoid(_pick_lane(gt, lane, head))
            a_neg = -jnp.exp(jnp.full((1, 1), al_ref[head], F32))
            g = a_neg * jax.nn.softplus(_pick_lane(gt, lane, HV + head) + dt_ref[head])
            g_col, g_row = _cumsum_col_row(g, row_i, col_i, eye)
            decay = jnp.exp(jnp.where(incl, g_col - g_row, NEG_INF))
            a = jnp.where(strict, beta * kk * decay, 0.0)
            x = _unit_lower_inverse(a, eye, masks, small_mm)
            s = s_scr[vh_local]
            eg = jnp.exp(g_col)
            rhs = beta * v - (beta * eg) * _bdot(k, s)
            u = small_mm(x, rhs)
            o = eg * _bdot(q, s) + small_mm(qk * decay, u)
            g_last = g_col[L - 1:L, :]
            kw = k * jnp.exp(g_last - g_col)
            s_scr[vh_local] = jnp.exp(g_last) * s + _bdot_tn(kw, u)
            y_ref[0, :, vh_local * DV:(vh_local + 1) * DV] = _head_norm_gate(
                o, gain_ref[...], z_ref[0, :, vh_local * DV:(vh_local + 1) * DV], False)

    @pl.when(c == NC - 1)
    def _():
        so_ref[...] = s_scr[...]


def _gdn(proj3, s0_all, conv0_all, j, conv_w_all, a_log_all, dt_bias_all, gain_all, L, HBK):
    b, t, n_in = proj3.shape
    dk = dv = gain_all.shape[-1]
    hv = a_log_all.shape[-1]
    cdim = conv_w_all.shape[-1]
    hk = (cdim - hv * dv) // (2 * dk)
    rep = hv // hk
    assert 2 * hv <= 128 and rep * hk == hv
    assert L <= 8, "this form keeps the chunk's triangular solve on the vector unit"
    nc = t // L
    hg = hk // HBK
    has_s0 = s0_all is not None
    wq, wv = HBK * dk, HBK * rep * dv
    k_blk = (hk * dk) // wq
    v_blk = (2 * hk * dk) // wv
    z_blk = cdim // wv
    gt_blk = (cdim + hv * dv) // 128

    def cspec(width, off):
        return pl.BlockSpec((None, CONV_W, width), lambda bi, hi, ci: (j, 0, off + hi))

    def c0spec(width, off):
        return pl.BlockSpec((None, None, CONV_W - 1, width), lambda bi, hi, ci: (j, bi, 0, off + hi))

    in_specs = [pl.BlockSpec(memory_space=pltpu.SMEM),
                pl.BlockSpec(memory_space=pltpu.SMEM),
                pl.BlockSpec((1, L, wq), lambda bi, hi, ci: (bi, ci, hi)),
                pl.BlockSpec((1, L, wq), lambda bi, hi, ci: (bi, ci, k_blk + hi)),
                pl.BlockSpec((1, L, wv), lambda bi, hi, ci: (bi, ci, v_blk + hi)),
                pl.BlockSpec((1, L, wv), lambda bi, hi, ci: (bi, ci, z_blk + hi)),
                pl.BlockSpec((1, L, 128), lambda bi, hi, ci: (bi, ci, gt_blk)),
                cspec(wq, 0), cspec(wq, k_blk), cspec(wv, v_blk),
                pl.BlockSpec((None, 1, dv), lambda bi, hi, ci: (j, 0, 0))]
    args = [a_log_all[j], dt_bias_all[j], proj3, proj3, proj3, proj3, proj3,
            conv_w_all, conv_w_all, conv_w_all, _rows3(gain_all)]
    st_spec = pl.BlockSpec((None, None, HBK * rep, dk, dv), lambda bi, hi, ci: (j, bi, hi, 0, 0))
    if has_s0:
        in_specs += [c0spec(wq, 0), c0spec(wq, k_blk), c0spec(wv, v_blk), st_spec]
        args += [conv0_all, conv0_all, conv0_all, s0_all]
    y, so = pl.pallas_call(
        functools.partial(_gdn_body, L=L, HBK=HBK, NC=nc, DK=dk, DV=dv, REP=rep, HV=hv, has_s0=has_s0),
        grid=(b, hg, nc),
        in_specs=in_specs,
        out_specs=[pl.BlockSpec((1, L, wv), lambda bi, hi, ci: (bi, ci, hi)), st_spec],
        out_shape=[jax.ShapeDtypeStruct((b, t, hv * dv), BF16),
                   jax.ShapeDtypeStruct((1, b, hv, dk, dv), F32)],
        scratch_shapes=[pltpu.VMEM((HBK * rep, dk, dv), F32),
                        pltpu.VMEM((L + 8, wq), F32), pltpu.VMEM((L + 8, wq), F32),
                        pltpu.VMEM((L + 8, wv), F32)],
        compiler_params=_params(("arbitrary", "arbitrary", "arbitrary"), 48),
        name="gdn",
    )(*args)
    new_conv = proj3[:, t - (CONV_W - 1):, :cdim][None]
    return y, so, new_conv


def _gdn_seq_body(al_ref, dt_ref, qx_ref, kx_ref, vx_ref, z_ref, gt_ref, cwq_ref, cwk_ref, cwv_ref,
                  gain_ref, y_ref, so_ref, s_scr, eq_scr, ek_scr, ev_scr,
                  *, TB, C, HBK, NTB, DK, DV, REP, HV):
    hb = pl.program_id(1)
    tb = pl.program_id(2)
    G = TB // C
    pad = 8
    taps = CONV_W - 1

    @pl.when(tb == 0)
    def _():
        for scr in (eq_scr, ek_scr, ev_scr):
            scr[0:pad, :] = jnp.zeros((pad, scr.shape[1]), F32)
        s_scr[...] = jnp.zeros_like(s_scr)

    def conv(x_ref, scr, cw_ref):
        scr[pad:pad + TB, :] = x_ref[0]
        acc = scr[pl.ds(pad - taps, TB), :] * cw_ref[0:1, :]
        for w in range(1, CONV_W):
            acc = acc + scr[pl.ds(pad - taps + w, TB), :] * cw_ref[w:w + 1, :]
        if NTB > 1:
            scr[0:pad, :] = scr[TB:TB + pad, :]
        return _silu(acc)

    cq = conv(qx_ref, eq_scr, cwq_ref)
    ck = conv(kx_ref, ek_scr, cwk_ref)
    cv = conv(vx_ref, ev_scr, cwv_ref)

    row_i = _iota2(C, C, 0)
    col_i = _iota2(C, C, 1)
    eye = row_i == col_i
    incl = row_i >= col_i
    strict = row_i > col_i
    masks = _merge_masks(row_i, col_i, C)
    lane = _iota2(TB, 128, 1)
    gt = gt_ref[0]
    for kh in range(HBK):
        q = cq[:, kh * DK:(kh + 1) * DK]
        k = ck[:, kh * DK:(kh + 1) * DK]
        q = q * lax.rsqrt(jnp.sum(q * q, axis=-1, keepdims=True) + EPS) * DK ** -0.5
        k = k * lax.rsqrt(jnp.sum(k * k, axis=-1, keepdims=True) + EPS)
        q3 = q.reshape(G, C, DK)
        k3 = k.reshape(G, C, DK)
        kk = _bmm_nt(k3, k3)
        qk = _bmm_nt(q3, k3)
        for r in range(REP):
            vh = kh * REP + r
            head = (hb * HBK + kh) * REP + r
            v3 = cv[:, vh * DV:(vh + 1) * DV].reshape(G, C, DV)
            beta = jax.nn.sigmoid(_pick_lane(gt, lane, head)).reshape(G, C, 1)
            a_neg = -jnp.exp(jnp.full((1, 1), al_ref[head], F32))
            g = (a_neg * jax.nn.softplus(_pick_lane(gt, lane, HV + head) + dt_ref[head])).reshape(G, C, 1)
            g_lanes = jnp.sum(jnp.where(eye, g, 0.0), axis=1, keepdims=True)
            g_col = jnp.sum(jnp.where(incl, g_lanes, 0.0), axis=2, keepdims=True)
            g_row = jnp.sum(jnp.where(row_i <= col_i, g, 0.0), axis=1, keepdims=True)
            decay = jnp.exp(jnp.where(incl, g_col - g_row, NEG_INF))
            a = jnp.where(strict, beta * kk * decay, 0.0)
            x = _unit_lower_inverse(a, eye, masks, _bmm)
            eg = jnp.exp(g_col)
            wu = _bmm(x, jnp.concatenate([(beta * eg) * k3, beta * v3], axis=-1))
            qo = _bmm(qk * decay, wu)
            q_eff = (eg * q3 - qo[:, :, :DK]).astype(BF16)
            o0 = qo[:, :, DK:]
            g_last = g_col[:, C - 1:C, :]
            mb = _bmm_tn(k3 * jnp.exp(g_last - g_col), wu)
            m_eff = mb[:, :, :DK].astype(BF16)
            b_eff = mb[:, :, DK:]
            eg_last = jnp.exp(g_last)
            s = s_scr[vh]
            outs = []
            for c in range(G):
                sb = s.astype(BF16)
                outs.append(jnp.dot(q_eff[c], sb, preferred_element_type=F32) + o0[c])
                s = eg_last[c] * s - jnp.dot(m_eff[c], sb, preferred_element_type=F32) + b_eff[c]
            s_scr[vh] = s
            o = jnp.concatenate(outs, axis=0) if G > 1 else outs[0]
            y_ref[0, :, vh * DV:(vh + 1) * DV] = _head_norm_gate(
                o, gain_ref[...], z_ref[0, :, vh * DV:(vh + 1) * DV], False)

    @pl.when(tb == NTB - 1)
    def _():
        so_ref[...] = s_scr[...]


def _gdn_seq(proj3, j, conv_w_all, a_log_all, dt_bias_all, gain_all, TB, C, HBK):
    b, t, n_in = proj3.shape
    dk = dv = gain_all.shape[-1]
    hv = a_log_all.shape[-1]
    cdim = conv_w_all.shape[-1]
    hk = (cdim - hv * dv) // (2 * dk)
    rep = hv // hk
    assert 2 * hv <= 128 and rep * hk == hv and dk == dv
    ntb = t // TB
    hg = hk // HBK
    wq, wv = HBK * dk, HBK * rep * dv
    k_blk = (hk * dk) // wq
    v_blk = (2 * hk * dk) // wv
    z_blk = cdim // wv
    gt_blk = (cdim + hv * dv) // 128

    def cspec(width, off):
        return pl.BlockSpec((None, CONV_W, width), lambda bi, hi, ti: (j, 0, off + hi))

    in_specs = [pl.BlockSpec(memory_space=pltpu.SMEM),
                pl.BlockSpec(memory_space=pltpu.SMEM),
                pl.BlockSpec((1, TB, wq), lambda bi, hi, ti: (bi, ti, hi)),
                pl.BlockSpec((1, TB, wq), lambda bi, hi, ti: (bi, ti, k_blk + hi)),
                pl.BlockSpec((1, TB, wv), lambda bi, hi, ti: (bi, ti, v_blk + hi)),
                pl.BlockSpec((1, TB, wv), lambda bi, hi, ti: (bi, ti, z_blk + hi)),
                pl.BlockSpec((1, TB, 128), lambda bi, hi, ti: (bi, ti, gt_blk)),
                cspec(wq, 0), cspec(wq, k_blk), cspec(wv, v_blk),
                pl.BlockSpec((None, 1, dv), lambda bi, hi, ti: (j, 0, 0))]
    args = [a_log_all[j], dt_bias_all[j], proj3, proj3, proj3, proj3, proj3,
            conv_w_all, conv_w_all, conv_w_all, _rows3(gain_all)]
    st_spec = pl.BlockSpec((None, None, HBK * rep, dk, dv), lambda bi, hi, ti: (0, bi, hi, 0, 0))
    y, so = pl.pallas_call(
        functools.partial(_gdn_seq_body, TB=TB, C=C, HBK=HBK, NTB=ntb, DK=dk, DV=dv, REP=rep, HV=hv),
        grid=(b, hg, ntb),
        in_specs=in_specs,
        out_specs=[pl.BlockSpec((1, TB, wv), lambda bi, hi, ti: (bi, ti, hi)), st_spec],
        out_shape=[jax.ShapeDtypeStruct((b, t, hv * dv), BF16),
                   jax.ShapeDtypeStruct((1, b, hv, dk, dv), F32)],
        scratch_shapes=[pltpu.VMEM((HBK * rep, dk, dv), F32),
                        pltpu.VMEM((TB + 8, wq), F32), pltpu.VMEM((TB + 8, wq), F32),
                        pltpu.VMEM((TB + 8, wv), F32)],
        compiler_params=_params(("arbitrary", "arbitrary", "arbitrary"), 48),
        name="gdn_seq",
    )(*args)
    new_conv = proj3[:, t - (CONV_W - 1):, :cdim][None]
    return y, so, new_conv


def _rope_tables(pos, dk):
    half = dk // 2
    inv = ROPE_BASE ** (-jnp.arange(half, dtype=F32) / half)
    ang = pos.astype(F32)[:, None] * inv[None, :]
    return jnp.cos(ang), jnp.sin(ang)


def _trunk(x, p, states, pos, w, cfg):
    (norm_pre, norm_post, ple_proj, ple_gate, ret_w_in, ret_head_norm, ret_w_out,
     mlstm_w_in, mlstm_b_gate, mlstm_head_norm, mlstm_w_out,
     gdn_w_in, gdn_conv_w, gdn_a_log, gdn_dt_bias, gdn_head_norm, gdn_w_out) = w
    ret_s, ml_c, ml_n, ml_m, gdn_s, gdn_conv = states
    b, t, d = x.shape
    depth = norm_pre.shape[0]
    n_ret = ret_w_in.shape[0]
    m = b * t
    tm = cfg["tm"]
    cos, sin = _rope_tables(pos, 256)
    r = x.reshape(m, d)
    ret_out = None
    outs = {}
    for i in range(depth):
        kind, j = i % 3, i // 3
        if kind == 0:
            proj = _inproj(r, norm_pre, i, ret_w_in, j, tm, 512).reshape(b, t, -1)
            y, ret_out = _retention(proj, ret_s, j, cos, sin, ret_head_norm, ret_out, n_ret,
                                    cfg["ret_L"], cfg["ret_HB"])
            w_out = ret_w_out
        elif kind == 1:
            proj = _inproj(r, norm_pre, i, mlstm_w_in, j, tm, 512).reshape(b, t, -1)
            y, outs["c"], outs["n"], outs["m"] = _mlstm(
                proj, ml_c, ml_n, ml_m, j, mlstm_b_gate, mlstm_head_norm, cfg["ml_L"], cfg["ml_HB"])
            w_out = mlstm_w_out
        else:
            proj = _inproj(r, norm_pre, i, gdn_w_in, j, tm, 512).reshape(b, t, -1)
            if gdn_s is None:
                y, outs["gs"], outs["gc"] = _gdn_seq(
                    proj, j, gdn_conv_w, gdn_a_log, gdn_dt_bias, gdn_head_norm,
                    cfg["gdn_TB"], cfg["gdn_L"], cfg["gdn_HBK"])
            else:
                y, outs["gs"], outs["gc"] = _gdn(
                    proj, gdn_s, gdn_conv, j, gdn_conv_w, gdn_a_log, gdn_dt_bias, gdn_head_norm,
                    cfg["gdn_L"], cfg["gdn_HBK"])
            w_out = gdn_w_out
        mix = _outproj(y.reshape(m, -1), w_out, j, tm, 512)
        r = _post_ple(mix, r, norm_post, p[i].reshape(m, -1), ple_proj, ple_gate, i, cfg["tm_ple"], 512)
    return (r.reshape(b, t, d), ret_out, outs["c"], outs["n"], outs["m"], outs["gs"], outs["gc"])


_PROMPT_CFG = dict(tm=1024, tm_ple=512, ret_L=256, ret_HB=1, ml_L=256, ml_HB=1,
                   gdn_TB=512, gdn_L=64, gdn_HBK=1)
_SAMPLE_CFG = dict(tm=512, tm_ple=512, ret_L=4, ret_HB=8, ml_L=4, ml_HB=8, gdn_L=4, gdn_HBK=4)


def kernel(x_prompt, x_sample, state_ret_S, state_mlstm_C, state_mlstm_n, state_mlstm_m, state_gdn_S, state_gdn_conv, p_prompt, p_sample, norm_pre, norm_post, ple_proj, ple_gate, ret_w_in, ret_head_norm, ret_w_out, mlstm_w_in, mlstm_b_gate, mlstm_head_norm, mlstm_w_out, gdn_w_in, gdn_conv_w, gdn_a_log, gdn_dt_bias, gdn_head_norm, gdn_w_out):
    w = (norm_pre, norm_post, ple_proj, ple_gate, ret_w_in, ret_head_norm, ret_w_out,
         mlstm_w_in, mlstm_b_gate, mlstm_head_norm, mlstm_w_out,
         gdn_w_in, gdn_conv_w, gdn_a_log, gdn_dt_bias, gdn_head_norm, gdn_w_out)
    pos_p = jnp.arange(x_prompt.shape[1])
    yp, ret_p, mc_p, mn_p, mm_p, gs_p, gc_p = _trunk(
        x_prompt, p_prompt, (None,) * 6, pos_p, w, _PROMPT_CFG)
    pos_s = PAST_LEN + jnp.arange(x_sample.shape[1])
    ys, ret_s, mc_s, mn_s, mm_s, gs_s, gc_s = _trunk(
        x_sample, p_sample,
        (state_ret_S, state_mlstm_C, state_mlstm_n, state_mlstm_m, state_gdn_S, state_gdn_conv),
        pos_s, w, _SAMPLE_CFG)
    return (yp, ys, ret_p, mc_p, mn_p, mm_p, gs_p, gc_p, ret_s, mc_s, mn_s, mm_s, gs_s, gc_s)
```

```python
import functools

import jax
import jax.numpy as jnp
from jax import lax
from jax.experimental import pallas as pl
from jax.experimental.pallas import tpu as pltpu

F32 = jnp.float32
BF16 = jnp.bfloat16
EPS = 1e-6
ROPE_BASE = 10000.0
CONV_W = 4
PAST_LEN = 16384
MIB = 1024 * 1024
NEG_INF = float("-inf")
CONV_PAD = 8


def _params(sem, vmem_mib):
    return pltpu.CompilerParams(dimension_semantics=sem, vmem_limit_bytes=vmem_mib * MIB)


def _bdot(a, b):
    return jnp.dot(a.astype(BF16), b.astype(BF16), preferred_element_type=F32)


def _bdot_nt(a, b):
    return lax.dot_general(a.astype(BF16), b.astype(BF16), (((1,), (1,)), ((), ())),
                           preferred_element_type=F32)


def _bdot_tn(a, b):
    return lax.dot_general(a.astype(BF16), b.astype(BF16), (((0,), (0,)), ((), ())),
                           preferred_element_type=F32)


def _silu(x):
    return x * jax.nn.sigmoid(x)


def _rows3(table):
    return table.reshape(table.shape[0], 1, table.shape[1])


SLAB = 256


def _row_slabs(tm):
    step = min(SLAB, tm)
    return [slice(s, s + step) for s in range(0, tm, step)]


def _once_per_row_tile(block_shape, index_map):
    return pl.BlockSpec(block_shape, index_map, pipeline_mode=pl.Buffered(1))


def _inproj_body(x_ref, g_ref, w_ref, o_ref, h_ref):
    @pl.when(pl.program_id(1) == 0)
    def _():
        for rows in _row_slabs(x_ref.shape[0]):
            x = x_ref[rows, :]
            ms = jnp.mean(x * x, axis=-1, keepdims=True)
            h_ref[rows, :] = (x * lax.rsqrt(ms + EPS) * g_ref[...]).astype(BF16)

    o_ref[...] = jnp.dot(h_ref[...], w_ref[...].astype(BF16), preferred_element_type=F32)


def _inproj(x2d, g_all, layer, w_all, j, tm, tn):
    m, d = x2d.shape
    n = w_all.shape[-1]
    return pl.pallas_call(
        _inproj_body,
        grid=(m // tm, pl.cdiv(n, tn)),
        in_specs=[_once_per_row_tile((tm, d), lambda i, c: (i, 0)),
                  pl.BlockSpec((None, 1, d), lambda i, c: (layer, 0, 0)),
                  pl.BlockSpec((None, d, tn), lambda i, c: (j, 0, c))],
        out_specs=pl.BlockSpec((tm, tn), lambda i, c: (i, c)),
        out_shape=jax.ShapeDtypeStruct((m, n), F32),
        scratch_shapes=[pltpu.VMEM((tm, d), BF16)],
        compiler_params=_params(("arbitrary", "arbitrary"), 56),
        name="inproj",
    )(x2d, _rows3(g_all), w_all)


def _outproj_body(y_ref, w_ref, o_ref):
    o_ref[...] = jnp.dot(y_ref[...].astype(BF16), w_ref[...].astype(BF16), preferred_element_type=F32)


def _outproj(y2d, w_all, j, tm, tn):
    m, k = y2d.shape
    n = w_all.shape[-1]
    return pl.pallas_call(
        _outproj_body,
        grid=(m // tm, n // tn),
        in_specs=[_once_per_row_tile((tm, k), lambda i, c: (i, 0)),
                  pl.BlockSpec((None, k, tn), lambda i, c: (j, 0, c))],
        out_specs=pl.BlockSpec((tm, tn), lambda i, c: (i, c)),
        out_shape=jax.ShapeDtypeStruct((m, n), F32),
        compiler_params=_params(("arbitrary", "arbitrary"), 56),
        name="outproj",
    )(y2d, w_all)


def _post_ple_body(mix_ref, r_ref, g_ref, p_ref, proj_ref, gate_ref, o_ref, r1_ref, r1b_ref, *, tn):
    c = pl.program_id(1)

    @pl.when(c == 0)
    def _():
        for rows in _row_slabs(mix_ref.shape[0]):
            mix = mix_ref[rows, :]
            ms = jnp.mean(mix * mix, axis=-1, keepdims=True)
            r1 = r_ref[rows, :] + mix * lax.rsqrt(ms + EPS) * g_ref[...]
            r1b_ref[rows, :] = r1.astype(BF16)
            for cc in range(r1.shape[1] // tn):
                r1_ref[cc, rows, :] = r1[:, cc * tn:(cc + 1) * tn]

    gate = jnp.dot(r1b_ref[...], gate_ref[...].astype(BF16), preferred_element_type=F32)
    emb = _bdot(p_ref[...], proj_ref[...])
    o_ref[...] = r1_ref[c] + emb * jax.nn.sigmoid(gate)


def _post_ple(mix, r, g_all, p_all, proj_all, gate_all, layer, tm, tn):
    m, d = r.shape
    pd = p_all.shape[-1]
    return pl.pallas_call(
        functools.partial(_post_ple_body, tn=tn),
        grid=(m // tm, d // tn),
        in_specs=[_once_per_row_tile((tm, d), lambda i, c: (i, 0)),
                  _once_per_row_tile((tm, d), lambda i, c: (i, 0)),
                  pl.BlockSpec((None, 1, d), lambda i, c: (layer, 0, 0)),
                  pl.BlockSpec((None, tm, pd), lambda i, c: (layer, i, 0)),
                  pl.BlockSpec((None, pd, tn), lambda i, c: (layer, 0, c)),
                  pl.BlockSpec((None, d, tn), lambda i, c: (layer, 0, c))],
        out_specs=pl.BlockSpec((tm, tn), lambda i, c: (i, c)),
        out_shape=jax.ShapeDtypeStruct((m, d), F32),
        scratch_shapes=[pltpu.VMEM((d // tn, tm, tn), F32), pltpu.VMEM((tm, d), BF16)],
        compiler_params=_params(("arbitrary", "arbitrary"), 56),
        name="post_ple",
    )(mix, r, _rows3(g_all), p_all, proj_all, gate_all)


def _iota2(n, m, dim):
    return lax.broadcasted_iota(jnp.int32, (n, m), dim)


def _col_to_row(col, eye):
    return jnp.sum(jnp.where(eye, col, 0.0), axis=0, keepdims=True)


def _cumsum_col_row(col, row_i, col_i, eye):
    row = _col_to_row(col, eye)
    c_col = jnp.sum(jnp.where(col_i <= row_i, row, 0.0), axis=1, keepdims=True)
    c_row = jnp.sum(jnp.where(row_i <= col_i, col, 0.0), axis=0, keepdims=True)
    return c_col, c_row


def _pick_lane(blk, lane_iota, idx):
    return jnp.sum(jnp.where(lane_iota == idx, blk, 0.0), axis=1, keepdims=True)


def _head_norm_gate(o, gain, z, center):
    if center:
        o = o - jnp.mean(o, axis=-1, keepdims=True)
    y = o * lax.rsqrt(jnp.mean(o * o, axis=-1, keepdims=True) + EPS) * gain
    return y * _silu(z)


def _row_block(nc):
    return lambda col: (lambda bi, hi, ci: (bi * nc + ci, col(hi)))


class _State:
    def __init__(self, in_ref, out_ref, scr, nc):
        self.in_ref, self.out_ref, self.scr, self.nc = in_ref, out_ref, scr, nc

    def start(self, chunk):
        if self.nc > 1:
            @pl.when(chunk == 0)
            def _():
                if self.in_ref is None:
                    self.scr[...] = jnp.zeros_like(self.scr)
                else:
                    self.scr[...] = self.in_ref[...]

    def get(self, bb, hh):
        if self.nc > 1:
            return self.scr[bb, hh]
        if self.in_ref is None:
            return jnp.zeros(self.out_ref.shape[2:], F32)
        return self.in_ref[bb, hh]

    def put(self, bb, hh, val):
        if self.nc > 1:
            self.scr[bb, hh] = val
        else:
            self.out_ref[bb, hh] = val

    def finish(self, chunk):
        if self.nc > 1:
            @pl.when(chunk == self.nc - 1)
            def _():
                self.out_ref[...] = self.scr[...]


def _ret_body(*refs, L, BB, HB, NC, DK, DV, has_s0, has_prev):
    lg_ref, q_ref, k_ref, v_ref, z_ref, cos_ref, sin_ref, gain_ref = refs[:8]
    s0_ref = refs[8] if has_s0 else None
    rest = refs[8 + has_s0 + has_prev:]
    y_ref, so_ref = rest[:2]
    st = _State(s0_ref, so_ref, rest[2] if NC > 1 else None, NC)
    hb = pl.program_id(1)
    c = pl.program_id(2)
    st.start(c)

    cos = cos_ref[...]
    sin = sin_ref[...]
    half = DK // 2

    def rot(x):
        x1, x2 = x[:, :half], x[:, half:]
        return jnp.concatenate([x1 * cos - x2 * sin, x2 * cos + x1 * sin], axis=-1)

    row_i = _iota2(L, L, 0)
    col_i = _iota2(L, L, 1)
    rel = (row_i - col_i).astype(F32)
    idx = _iota2(L, 1, 0).astype(F32)
    for hh in range(HB):
        lg = lg_ref[hb * HB + hh]
        decay = jnp.exp(jnp.where(rel >= 0, lg * rel, NEG_INF))
        w_in = jnp.exp(lg * (idx + 1.0))
        w_out = jnp.exp(lg * (L - 1.0 - idx))
        w_all = jnp.exp(jnp.full((1, 1), L, F32) * lg)
        hq = slice(hh * DK, (hh + 1) * DK)
        hv = slice(hh * DV, (hh + 1) * DV)
        for bb in range(BB):
            rows = slice(bb * L, (bb + 1) * L)
            q = rot(q_ref[rows, hq])
            k = rot(k_ref[rows, hq]) * DK ** -0.5
            v = v_ref[rows, hv]
            s = st.get(bb, hh)
            scores = _bdot_nt(q, k) * decay
            o = _bdot(scores, v) + _bdot(q * w_in, s)
            st.put(bb, hh, w_all * s + _bdot_tn(k * w_out, v))
            y_ref[rows, hv] = _head_norm_gate(o, gain_ref[:, hv], z_ref[rows, hv], True).astype(y_ref.dtype)
    st.finish(c)


def _retention(proj, b, t, s0_all, j, cos, sin, gain_all, so_prev, n_layers, L, BB, HB, y_dtype):
    h, dk = 8, 256
    dv = gain_all.shape[-1] // h
    nc = t // L
    assert BB == 1 or nc == 1
    hg = h // HB
    rb = _row_block(nc)
    lg = jnp.log1p(-jnp.exp2(-5.0 - jnp.arange(h, dtype=F32)))
    has_s0 = s0_all is not None
    v_blk = (2 * h * dk) // (HB * dv)
    z_blk = (2 * h * dk + h * dv) // (HB * dv)
    in_specs = [pl.BlockSpec(memory_space=pltpu.SMEM),
                pl.BlockSpec((BB * L, HB * dk), rb(lambda hi: hi)),
                pl.BlockSpec((BB * L, HB * dk), rb(lambda hi: hg + hi)),
                pl.BlockSpec((BB * L, HB * dv), rb(lambda hi: v_blk + hi)),
                pl.BlockSpec((BB * L, HB * dv), rb(lambda hi: z_blk + hi)),
                pl.BlockSpec((L, dk // 2), lambda bi, hi, ci: (ci, 0)),
                pl.BlockSpec((L, dk // 2), lambda bi, hi, ci: (ci, 0)),
                pl.BlockSpec((None, 1, HB * dv), lambda bi, hi, ci: (j, 0, hi))]
    args = [lg, proj, proj, proj, proj, cos, sin, _rows3(gain_all)]
    st_spec = pl.BlockSpec((None, BB, HB, dk, dv), lambda bi, hi, ci: (j, bi, hi, 0, 0))
    if has_s0:
        in_specs.append(st_spec)
        args.append(s0_all)
    aliases = {}
    if so_prev is not None:
        in_specs.append(pl.BlockSpec(memory_space=pl.ANY))
        args.append(so_prev)
        aliases = {len(args) - 1: 1}
    return pl.pallas_call(
        functools.partial(_ret_body, L=L, BB=BB, HB=HB, NC=nc, DK=dk, DV=dv, has_s0=has_s0,
                          has_prev=so_prev is not None),
        grid=(b // BB, hg, nc),
        in_specs=in_specs,
        out_specs=[pl.BlockSpec((BB * L, HB * dv), rb(lambda hi: hi)), st_spec],
        out_shape=[jax.ShapeDtypeStruct((b * t, h * dv), y_dtype),
                   jax.ShapeDtypeStruct((n_layers, b, h, dk, dv), F32)],
        scratch_shapes=[pltpu.VMEM((BB, HB, dk, dv), F32)] if nc > 1 else [],
        input_output_aliases=aliases,
        compiler_params=_params(("arbitrary", "arbitrary", "arbitrary"), 48),
        name="retention",
    )(*args)


def _mlstm_body(*refs, L, BB, HB, NC, DK, DV, H, has_s0):
    bg_ref, q_ref, k_ref, v_ref, og_ref, z_ref, gt_ref, gain_ref = refs[:8]
    c0_ref, n0_ref, m0_ref = refs[8:11] if has_s0 else (None, None, None)
    rest = refs[8 + 3 * has_s0:]
    y_ref, co_ref, no_ref, mo_ref = rest[:4]
    scr = rest[4:] if NC > 1 else (None, None, None)
    st_c = _State(c0_ref, co_ref, scr[0], NC)
    st_n = _State(n0_ref, no_ref, scr[1], NC)
    st_m = _State(m0_ref, mo_ref, scr[2], NC)
    hb = pl.program_id(1)
    c = pl.program_id(2)
    for st in (st_c, st_n, st_m):
        st.start(c)

    row_i = _iota2(L, L, 0)
    col_i = _iota2(L, L, 1)
    eye = row_i == col_i
    causal = row_i >= col_i
    lane = _iota2(L, 128, 1)
    for bb in range(BB):
        rows = slice(bb * L, (bb + 1) * L)
        gt = gt_ref[rows, :]
        for hh in range(HB):
            head = hb * HB + hh
            hq = slice(hh * DK, (hh + 1) * DK)
            hv = slice(hh * DV, (hh + 1) * DV)
            ig = _pick_lane(gt, lane, head) + bg_ref[head]
            fg = _pick_lane(gt, lane, H + head) + bg_ref[H + head]
            lf = jax.nn.log_sigmoid(fg)
            b_col, b_row = _cumsum_col_row(lf, row_i, col_i, eye)
            i_row = _col_to_row(ig, eye)
            q = q_ref[rows, hq] * DK ** -0.5
            k = k_ref[rows, hq]
            v = v_ref[rows, hv]
            cm = st_c.get(bb, hh)
            nv = st_n.get(bb, hh)
            m_prev = st_m.get(bb, hh)
            dlog = jnp.where(causal, b_col - b_row + i_row, NEG_INF)
            inter = b_col + m_prev
            mt = jnp.maximum(inter, jnp.max(dlog, axis=1, keepdims=True))
            s = _bdot_nt(q, k) * jnp.exp(dlog - mt)
            wi = jnp.exp(inter - mt)
            num = _bdot(s, v) + wi * _bdot(q, cm)
            den = jnp.sum(s, axis=1, keepdims=True) + wi * jnp.sum(q * nv, axis=1, keepdims=True)
            ht = num / jnp.maximum(jnp.abs(den), jnp.exp(-mt))
            m_new = mt[L - 1:L, :]
            b_last = b_col[L - 1:L, :]
            w_last = jnp.exp(b_last - b_col + ig - m_new)
            dec = jnp.exp(b_last + m_prev - m_new)
            kw = k * w_last
            st_c.put(bb, hh, dec * cm + _bdot_tn(kw, v))
            st_n.put(bb, hh, dec * nv + jnp.sum(kw, axis=0, keepdims=True))
            st_m.put(bb, hh, m_new)
            hcell = ht * jax.nn.sigmoid(og_ref[rows, hv])
            y_ref[rows, hv] = _head_norm_gate(hcell, gain_ref[:, hv], z_ref[rows, hv], True).astype(y_ref.dtype)
    for st in (st_c, st_n, st_m):
        st.finish(c)


def _mlstm(proj, b, t, c0_all, n0_all, m0_all, j, bgate_all, gain_all, L, BB, HB, y_dtype):
    h, dk = 8, 256
    dv = gain_all.shape[-1] // h
    nc = t // L
    assert BB == 1 or nc == 1
    hg = h // HB
    rb = _row_block(nc)
    has_s0 = c0_all is not None
    v_off = 2 * h * dk
    wv = HB * dv
    gt_blk = (v_off + 3 * h * dv) // 128
    in_specs = [pl.BlockSpec(memory_space=pltpu.SMEM),
                pl.BlockSpec((BB * L, HB * dk), rb(lambda hi: hi)),
                pl.BlockSpec((BB * L, HB * dk), rb(lambda hi: hg + hi)),
                pl.BlockSpec((BB * L, wv), rb(lambda hi: v_off // wv + hi)),
                pl.BlockSpec((BB * L, wv), rb(lambda hi: (v_off + h * dv) // wv + hi)),
                pl.BlockSpec((BB * L, wv), rb(lambda hi: (v_off + 2 * h * dv) // wv + hi)),
                pl.BlockSpec((BB * L, 128), rb(lambda hi: gt_blk)),
                pl.BlockSpec((None, 1, wv), lambda bi, hi, ci: (j, 0, hi))]
    args = [bgate_all[j], proj, proj, proj, proj, proj, proj, _rows3(gain_all)]
    c_spec = pl.BlockSpec((None, BB, HB, dk, dv), lambda bi, hi, ci: (j, bi, hi, 0, 0))
    n_spec = pl.BlockSpec((None, BB, HB, 1, dk), lambda bi, hi, ci: (j, bi, hi, 0, 0))
    m_spec = pl.BlockSpec((None, BB, HB, 1, 1), lambda bi, hi, ci: (j, bi, hi, 0, 0))
    nl = 1
    if has_s0:
        assert c0_all.shape[0] == nl
        in_specs += [c_spec, n_spec, m_spec]
        args += [c0_all, n0_all.reshape(nl, b, h, 1, dk), m0_all.reshape(nl, b, h, 1, 1)]
    scratch = [pltpu.VMEM((BB, HB, dk, dv), F32), pltpu.VMEM((BB, HB, 1, dk), F32),
               pltpu.VMEM((BB, HB, 1, 1), F32)] if nc > 1 else []
    y, co, no, mo = pl.pallas_call(
        functools.partial(_mlstm_body, L=L, BB=BB, HB=HB, NC=nc, DK=dk, DV=dv, H=h, has_s0=has_s0),
        grid=(b // BB, hg, nc),
        in_specs=in_specs,
        out_specs=[pl.BlockSpec((BB * L, wv), rb(lambda hi: hi)), c_spec, n_spec, m_spec],
        out_shape=[jax.ShapeDtypeStruct((b * t, h * dv), y_dtype),
                   jax.ShapeDtypeStruct((nl, b, h, dk, dv), F32),
                   jax.ShapeDtypeStruct((nl, b, h, 1, dk), F32),
                   jax.ShapeDtypeStruct((nl, b, h, 1, 1), F32)],
        scratch_shapes=scratch,
        compiler_params=_params(("arbitrary", "arbitrary", "arbitrary"), 48),
        name="mlstm",
    )(*args)
    return y, co, no.reshape(nl, b, h, dk), mo.reshape(nl, b, h)


def _merge_masks(row_i, col_i, L):
    masks = []
    s = 1
    while s < L:
        masks.append(((row_i // (2 * s)) == (col_i // (2 * s))) & ((row_i // s) > (col_i // s)))
        s *= 2
    return masks


def _unit_lower_inverse(a_strict, eye, masks, mm):
    x = jnp.where(eye, 1.0, 0.0) - jnp.where(masks[0], a_strict, 0.0)
    for mask in masks[1:]:
        e = jnp.where(mask, a_strict, 0.0)
        x = x - mm(x, mm(e, x))
    return x


def _vpu_mm(a, b):
    out = a[:, 0:1] * b[0:1, :]
    for kk in range(1, a.shape[1]):
        out = out + a[:, kk:kk + 1] * b[kk:kk + 1, :]
    return out


def _bmm(a, b):
    return jnp.einsum("gik,gkj->gij", a.astype(BF16), b.astype(BF16), preferred_element_type=F32)


def _bmm_nt(a, b):
    return jnp.einsum("gik,gjk->gij", a.astype(BF16), b.astype(BF16), preferred_element_type=F32)


def _bmm_tn(a, b):
    return jnp.einsum("gki,gkj->gij", a.astype(BF16), b.astype(BF16), preferred_element_type=F32)


def _gdn_dims(conv_w_all, a_log_all, gain_all):
    dk = dv = gain_all.shape[-1]
    hv = a_log_all.shape[-1]
    cdim = conv_w_all.shape[-1]
    hk = (cdim - hv * dv) // (2 * dk)
    rep = hv // hk
    assert 2 * hv <= 128 and rep * hk == hv
    return dk, dv, hv, hk, rep, cdim


def _gdn_step_body(al_ref, dt_ref, qx_ref, kx_ref, vx_ref, z_ref, gt_ref, cwq_ref, cwk_ref, cwv_ref,
                   gain_ref, cq0_ref, ck0_ref, cv0_ref, s0_ref, y_ref, so_ref, eq_scr, ek_scr, ev_scr,
                   *, L, BB, HBK, DK, DV, REP, HV):
    hb = pl.program_id(1)
    taps = CONV_W - 1

    def conv(x_ref, c0_ref, scr, cw_ref, bb):
        scr[CONV_PAD - taps:CONV_PAD, :] = c0_ref[bb]
        scr[CONV_PAD:CONV_PAD + L, :] = x_ref[bb * L:(bb + 1) * L, :]
        acc = scr[pl.ds(CONV_PAD - taps, L), :] * cw_ref[0:1, :]
        for w in range(1, CONV_W):
            acc = acc + scr[pl.ds(CONV_PAD - taps + w, L), :] * cw_ref[w:w + 1, :]
        return _silu(acc)

    row_i = _iota2(L, L, 0)
    col_i = _iota2(L, L, 1)
    eye = row_i == col_i
    incl = row_i >= col_i
    strict = row_i > col_i
    lane = _iota2(L, 128, 1)
    masks = _merge_masks(row_i, col_i, L)
    for bb in range(BB):
        rows = slice(bb * L, (bb + 1) * L)
        cq = conv(qx_ref, cq0_ref, eq_scr, cwq_ref, bb)
        ck = conv(kx_ref, ck0_ref, ek_scr, cwk_ref, bb)
        cv = conv(vx_ref, cv0_ref, ev_scr, cwv_ref, bb)
        gt = gt_ref[rows, :]
        for kh in range(HBK):
            q = cq[:, kh * DK:(kh + 1) * DK]
            k = ck[:, kh * DK:(kh + 1) * DK]
            q = q * lax.rsqrt(jnp.sum(q * q, axis=-1, keepdims=True) + EPS) * DK ** -0.5
            k = k * lax.rsqrt(jnp.sum(k * k, axis=-1, keepdims=True) + EPS)
            kk = _bdot_nt(k, k)
            qk = _bdot_nt(q, k)
            for r in range(REP):
                vh = kh * REP + r
                head = (hb * HBK + kh) * REP + r
                hv = slice(vh * DV, (vh + 1) * DV)
                v = cv[:, hv]
                beta = jax.nn.sigmoid(_pick_lane(gt, lane, head))
                a_neg = -jnp.exp(jnp.full((1, 1), al_ref[head], F32))
                g = a_neg * jax.nn.softplus(_pick_lane(gt, lane, HV + head) + dt_ref[head])
                g_col, g_row = _cumsum_col_row(g, row_i, col_i, eye)
                decay = jnp.exp(jnp.where(incl, g_col - g_row, NEG_INF))
                a = jnp.where(strict, beta * kk * decay, 0.0)
                x = _unit_lower_inverse(a, eye, masks, _vpu_mm)
                s = s0_ref[bb, vh]
                eg = jnp.exp(g_col)
                rhs = beta * v - (beta * eg) * _bdot(k, s)
                u = _vpu_mm(x, rhs)
                o = eg * _bdot(q, s) + _vpu_mm(qk * decay, u)
                g_last = g_col[L - 1:L, :]
                kw = k * jnp.exp(g_last - g_col)
                so_ref[bb, vh] = jnp.exp(g_last) * s + _bdot_tn(kw, u)
                y_ref[rows, hv] = _head_norm_gate(o, gain_ref[...], z_ref[rows, hv], False).astype(y_ref.dtype)


def _gdn_step(proj, b, t, s0_all, conv0_all, j, conv_w_all, a_log_all, dt_bias_all, gain_all, BB, HBK, y_dtype):
    dk, dv, hv, hk, rep, cdim = _gdn_dims(conv_w_all, a_log_all, gain_all)
    L = t
    assert L <= 8, "this form keeps the chunk's triangular solve on the vector unit"
    hg = hk // HBK
    wq, wv = HBK * dk, HBK * rep * dv
    k_blk = (hk * dk) // wq
    v_blk = (2 * hk * dk) // wv
    z_blk = cdim // wv
    gt_blk = (cdim + hv * dv) // 128
    rb = _row_block(1)

    def cspec(width, off):
        return pl.BlockSpec((None, CONV_W, width), lambda bi, hi, ci: (j, 0, off + hi))

    def c0spec(width, off):
        return pl.BlockSpec((None, BB, CONV_W - 1, width), lambda bi, hi, ci: (j, bi, 0, off + hi))

    st_spec = pl.BlockSpec((None, BB, HBK * rep, dk, dv), lambda bi, hi, ci: (j, bi, hi, 0, 0))
    in_specs = [pl.BlockSpec(memory_space=pltpu.SMEM),
                pl.BlockSpec(memory_space=pltpu.SMEM),
                pl.BlockSpec((BB * L, wq), rb(lambda hi: hi)),
                pl.BlockSpec((BB * L, wq), rb(lambda hi: k_blk + hi)),
                pl.BlockSpec((BB * L, wv), rb(lambda hi: v_blk + hi)),
                pl.BlockSpec((BB * L, wv), rb(lambda hi: z_blk + hi)),
                pl.BlockSpec((BB * L, 128), rb(lambda hi: gt_blk)),
                cspec(wq, 0), cspec(wq, k_blk), cspec(wv, v_blk),
                pl.BlockSpec((None, 1, dv), lambda bi, hi, ci: (j, 0, 0)),
                c0spec(wq, 0), c0spec(wq, k_blk), c0spec(wv, v_blk), st_spec]
    args = [a_log_all[j], dt_bias_all[j], proj, proj, proj, proj, proj,
            conv_w_all, conv_w_all, conv_w_all, _rows3(gain_all),
            conv0_all, conv0_all, conv0_all, s0_all]
    y, so = pl.pallas_call(
        functools.partial(_gdn_step_body, L=L, BB=BB, HBK=HBK, DK=dk, DV=dv, REP=rep, HV=hv),
        grid=(b // BB, hg, 1),
        in_specs=in_specs,
        out_specs=[pl.BlockSpec((BB * L, wv), rb(lambda hi: hi)), st_spec],
        out_shape=[jax.ShapeDtypeStruct((b * t, hv * dv), y_dtype),
                   jax.ShapeDtypeStruct((1, b, hv, dk, dv), F32)],
        scratch_shapes=[pltpu.VMEM((L + CONV_PAD, wq), F32), pltpu.VMEM((L + CONV_PAD, wq), F32),
                        pltpu.VMEM((L + CONV_PAD, wv), F32)],
        compiler_params=_params(("arbitrary", "arbitrary", "arbitrary"), 48),
        name="gdn_step",
    )(*args)
    return y, so


def _gdn_seq_body(al_ref, dt_ref, qx_ref, kx_ref, vx_ref, z_ref, gt_ref, cwq_ref, cwk_ref, cwv_ref,
                  gain_ref, y_ref, so_ref, s_scr, eq_scr, ek_scr, ev_scr,
                  *, TB, C, HBK, NTB, DK, DV, REP, HV):
    hb = pl.program_id(1)
    tb = pl.program_id(2)
    G = TB // C
    taps = CONV_W - 1

    @pl.when(tb == 0)
    def _():
        for scr in (eq_scr, ek_scr, ev_scr):
            scr[0:CONV_PAD, :] = jnp.zeros((CONV_PAD, scr.shape[1]), F32)
        s_scr[...] = jnp.zeros_like(s_scr)

    def conv(x_ref, scr, cw_ref):
        scr[CONV_PAD:CONV_PAD + TB, :] = x_ref[...]
        acc = scr[pl.ds(CONV_PAD - taps, TB), :] * cw_ref[0:1, :]
        for w in range(1, CONV_W):
            acc = acc + scr[pl.ds(CONV_PAD - taps + w, TB), :] * cw_ref[w:w + 1, :]
        if NTB > 1:
            scr[0:CONV_PAD, :] = scr[TB:TB + CONV_PAD, :]
        return _silu(acc)

    cq = conv(qx_ref, eq_scr, cwq_ref)
    ck = conv(kx_ref, ek_scr, cwk_ref)
    cv = conv(vx_ref, ev_scr, cwv_ref)

    row_i = _iota2(C, C, 0)
    col_i = _iota2(C, C, 1)
    eye = row_i == col_i
    incl = row_i >= col_i
    strict = row_i > col_i
    masks = _merge_masks(row_i, col_i, C)
    lane = _iota2(TB, 128, 1)
    gt = gt_ref[...]
    for kh in range(HBK):
        q = cq[:, kh * DK:(kh + 1) * DK]
        k = ck[:, kh * DK:(kh + 1) * DK]
        q = q * lax.rsqrt(jnp.sum(q * q, axis=-1, keepdims=True) + EPS) * DK ** -0.5
        k = k * lax.rsqrt(jnp.sum(k * k, axis=-1, keepdims=True) + EPS)
        q3 = q.reshape(G, C, DK)
        k3 = k.reshape(G, C, DK)
        kk = _bmm_nt(k3, k3)
        qk = _bmm_nt(q3, k3)
        for r in range(REP):
            vh = kh * REP + r
            head = (hb * HBK + kh) * REP + r
            hv = slice(vh * DV, (vh + 1) * DV)
            v3 = cv[:, hv].reshape(G, C, DV)
            beta = jax.nn.sigmoid(_pick_lane(gt, lane, head)).reshape(G, C, 1)
            a_neg = -jnp.exp(jnp.full((1, 1), al_ref[head], F32))
            g = (a_neg * jax.nn.softplus(_pick_lane(gt, lane, HV + head) + dt_ref[head])).reshape(G, C, 1)
            g_lanes = jnp.sum(jnp.where(eye, g, 0.0), axis=1, keepdims=True)
            g_col = jnp.sum(jnp.where(incl, g_lanes, 0.0), axis=2, keepdims=True)
            g_row = jnp.sum(jnp.where(row_i <= col_i, g, 0.0), axis=1, keepdims=True)
            decay = jnp.exp(jnp.where(incl, g_col - g_row, NEG_INF))
            a = jnp.where(strict, beta * kk * decay, 0.0)
            x = _unit_lower_inverse(a, eye, masks, _bmm)
            eg = jnp.exp(g_col)
            wu = _bmm(x, jnp.concatenate([(beta * eg) * k3, beta * v3], axis=-1))
            qo = _bmm(qk * decay, wu)
            q_eff = (eg * q3 - qo[:, :, :DK]).astype(BF16)
            o0 = qo[:, :, DK:]
            g_last = g_col[:, C - 1:C, :]
            mb = _bmm_tn(k3 * jnp.exp(g_last - g_col), wu)
            m_eff = mb[:, :, :DK].astype(BF16)
            b_eff = mb[:, :, DK:]
            eg_last = jnp.exp(g_last)
            s = s_scr[vh]
            outs = []
            for c in range(G):
                sb = s.astype(BF16)
                outs.append(jnp.dot(q_eff[c], sb, preferred_element_type=F32) + o0[c])
                s = eg_last[c] * s - jnp.dot(m_eff[c], sb, preferred_element_type=F32) + b_eff[c]
            s_scr[vh] = s
            o = jnp.concatenate(outs, axis=0) if G > 1 else outs[0]
            y_ref[:, hv] = _head_norm_gate(o, gain_ref[...], z_ref[:, hv], False).astype(y_ref.dtype)

    @pl.when(tb == NTB - 1)
    def _():
        so_ref[...] = s_scr[...]


def _gdn_seq(proj, b, t, j, conv_w_all, a_log_all, dt_bias_all, gain_all, TB, C, HBK, y_dtype):
    dk, dv, hv, hk, rep, cdim = _gdn_dims(conv_w_all, a_log_all, gain_all)
    assert dk == dv
    ntb = t // TB
    hg = hk // HBK
    wq, wv = HBK * dk, HBK * rep * dv
    k_blk = (hk * dk) // wq
    v_blk = (2 * hk * dk) // wv
    z_blk = cdim // wv
    gt_blk = (cdim + hv * dv) // 128
    rb = _row_block(ntb)

    def cspec(width, off):
        return pl.BlockSpec((None, CONV_W, width), lambda bi, hi, ti: (j, 0, off + hi))

    in_specs = [pl.BlockSpec(memory_space=pltpu.SMEM),
                pl.BlockSpec(memory_space=pltpu.SMEM),
                pl.BlockSpec((TB, wq), rb(lambda hi: hi)),
                pl.BlockSpec((TB, wq), rb(lambda hi: k_blk + hi)),
                pl.BlockSpec((TB, wv), rb(lambda hi: v_blk + hi)),
                pl.BlockSpec((TB, wv), rb(lambda hi: z_blk + hi)),
                pl.BlockSpec((TB, 128), rb(lambda hi: gt_blk)),
                cspec(wq, 0), cspec(wq, k_blk), cspec(wv, v_blk),
                pl.BlockSpec((None, 1, dv), lambda bi, hi, ti: (j, 0, 0))]
    args = [a_log_all[j], dt_bias_all[j], proj, proj, proj, proj, proj,
            conv_w_all, conv_w_all, conv_w_all, _rows3(gain_all)]
    st_spec = pl.BlockSpec((None, None, HBK * rep, dk, dv), lambda bi, hi, ti: (0, bi, hi, 0, 0))
    return pl.pallas_call(
        functools.partial(_gdn_seq_body, TB=TB, C=C, HBK=HBK, NTB=ntb, DK=dk, DV=dv, REP=rep, HV=hv),
        grid=(b, hg, ntb),
        in_specs=in_specs,
        out_specs=[pl.BlockSpec((TB, wv), rb(lambda hi: hi)), st_spec],
        out_shape=[jax.ShapeDtypeStruct((b * t, hv * dv), y_dtype),
                   jax.ShapeDtypeStruct((1, b, hv, dk, dv), F32)],
        scratch_shapes=[pltpu.VMEM((HBK * rep, dk, dv), F32),
                        pltpu.VMEM((TB + CONV_PAD, wq), F32), pltpu.VMEM((TB + CONV_PAD, wq), F32),
                        pltpu.VMEM((TB + CONV_PAD, wv), F32)],
        compiler_params=_params(("arbitrary", "arbitrary", "arbitrary"), 48),
        name="gdn_seq",
    )(*args)


def _rope_tables(pos, dk):
    half = dk // 2
    inv = ROPE_BASE ** (-jnp.arange(half, dtype=F32) / half)
    ang = pos.astype(F32)[:, None] * inv[None, :]
    return jnp.cos(ang), jnp.sin(ang)


def _trunk(x, p, states, pos, w, cfg):
    (norm_pre, norm_post, ple_proj, ple_gate, ret_w_in, ret_head_norm, ret_w_out,
     mlstm_w_in, mlstm_b_gate, mlstm_head_norm, mlstm_w_out,
     gdn_w_in, gdn_conv_w, gdn_a_log, gdn_dt_bias, gdn_head_norm, gdn_w_out) = w
    ret_s, ml_c, ml_n, ml_m, gdn_s, gdn_conv = states
    b, t, d = x.shape
    depth = norm_pre.shape[0]
    n_ret = ret_w_in.shape[0]
    m = b * t
    tm, ydt = cfg["tm"], cfg["y_dtype"]
    cos, sin = _rope_tables(pos, 256)
    r = x.reshape(m, d)
    p2 = p.reshape(depth, m, p.shape[-1])
    ret_out = None
    outs = {}
    for i in range(depth):
        kind, j = i % 3, i // 3
        if kind == 0:
            proj = _inproj(r, norm_pre, i, ret_w_in, j, tm, 512)
            y, ret_out = _retention(proj, b, t, ret_s, j, cos, sin, ret_head_norm, ret_out, n_ret,
                                    cfg["ret_L"], cfg["ret_BB"], cfg["ret_HB"], ydt)
            w_out = ret_w_out
        elif kind == 1:
            proj = _inproj(r, norm_pre, i, mlstm_w_in, j, tm, 512)
            y, outs["c"], outs["n"], outs["m"] = _mlstm(
                proj, b, t, ml_c, ml_n, ml_m, j, mlstm_b_gate, mlstm_head_norm,
                cfg["ml_L"], cfg["ml_BB"], cfg["ml_HB"], ydt)
            w_out = mlstm_w_out
        else:
            proj = _inproj(r, norm_pre, i, gdn_w_in, j, tm, 512)
            if gdn_s is None:
                y, outs["gs"] = _gdn_seq(proj, b, t, j, gdn_conv_w, gdn_a_log, gdn_dt_bias, gdn_head_norm,
                                         cfg["gdn_TB"], cfg["gdn_L"], cfg["gdn_HBK"], ydt)
            else:
                y, outs["gs"] = _gdn_step(proj, b, t, gdn_s, gdn_conv, j, gdn_conv_w, gdn_a_log, gdn_dt_bias,
                                          gdn_head_norm, cfg["gdn_BB"], cfg["gdn_HBK"], ydt)
            cdim = gdn_conv_w.shape[-1]
            outs["gc"] = proj[:, :cdim].reshape(b, t, cdim)[:, t - (CONV_W - 1):][None]
            w_out = gdn_w_out
        mix = _outproj(y, w_out, j, tm, cfg["tn_out"])
        r = _post_ple(mix, r, norm_post, p2, ple_proj, ple_gate, i, cfg["tm_ple"], 512)
    return (r.reshape(b, t, d), ret_out, outs["c"], outs["n"], outs["m"], outs["gs"], outs["gc"])


_PROMPT_CFG = dict(tm=2048, tn_out=256, tm_ple=1024, y_dtype=BF16, ret_L=256, ret_BB=1, ret_HB=1,
                   ml_L=256, ml_BB=1, ml_HB=1, gdn_TB=512, gdn_L=64, gdn_HBK=1)
_SAMPLE_CFG = dict(tm=512, tn_out=512, tm_ple=512, y_dtype=F32, ret_L=4, ret_BB=2, ret_HB=4,
                   ml_L=4, ml_BB=2, ml_HB=4, gdn_BB=2, gdn_HBK=4)


def kernel(x_prompt, x_sample, state_ret_S, state_mlstm_C, state_mlstm_n, state_mlstm_m, state_gdn_S, state_gdn_conv, p_prompt, p_sample, norm_pre, norm_post, ple_proj, ple_gate, ret_w_in, ret_head_norm, ret_w_out, mlstm_w_in, mlstm_b_gate, mlstm_head_norm, mlstm_w_out, gdn_w_in, gdn_conv_w, gdn_a_log, gdn_dt_bias, gdn_head_norm, gdn_w_out):
    w = (norm_pre, norm_post, ple_proj, ple_gate, ret_w_in, ret_head_norm, ret_w_out,
         mlstm_w_in, mlstm_b_gate, mlstm_head_norm, mlstm_w_out,
         gdn_w_in, gdn_conv_w, gdn_a_log, gdn_dt_bias, gdn_head_norm, gdn_w_out)
    pos_p = jnp.arange(x_prompt.shape[1])
    yp, ret_p, mc_p, mn_p, mm_p, gs_p, gc_p = _trunk(
        x_prompt, p_prompt, (None,) * 6, pos_p, w, _PROMPT_CFG)
    pos_s = PAST_LEN + jnp.arange(x_sample.shape[1])
    ys, ret_s, mc_s, mn_s, mm_s, gs_s, gc_s = _trunk(
        x_sample, p_sample,
        (state_ret_S, state_mlstm_C, state_mlstm_n, state_mlstm_m, state_gdn_S, state_gdn_conv),
        pos_s, w, _SAMPLE_CFG)
    return (yp, ys, ret_p, mc_p, mn_p, mm_p, gs_p, gc_p, ret_s, mc_s, mn_s, mm_s, gs_s, gc_s)
```

```python
import functools

import jax
import jax.numpy as jnp
from jax import lax
from jax.experimental import pallas as pl
from jax.experimental.pallas import tpu as pltpu

F32 = jnp.float32
BF16 = jnp.bfloat16
EPS = 1e-6
ROPE_BASE = 10000.0
CONV_W = 4
PAST_LEN = 16384
MIB = 1024 * 1024
NEG_INF = float("-inf")
CONV_PAD = 8
LANES = 128
TAIL_ROWS = 8


def _params(sem, vmem_mib):
    return pltpu.CompilerParams(dimension_semantics=sem, vmem_limit_bytes=vmem_mib * MIB)


def _bdot(a, b):
    return jnp.dot(a.astype(BF16), b.astype(BF16), preferred_element_type=F32)


def _bdot_nt(a, b):
    return lax.dot_general(a.astype(BF16), b.astype(BF16), (((1,), (1,)), ((), ())),
                           preferred_element_type=F32)


def _bdot_tn(a, b):
    return lax.dot_general(a.astype(BF16), b.astype(BF16), (((0,), (0,)), ((), ())),
                           preferred_element_type=F32)


def _silu(x):
    return x * jax.nn.sigmoid(x)


def _rows3(table):
    return table.reshape(table.shape[0], 1, table.shape[1])


SLAB = 256


def _row_slabs(tm):
    step = min(SLAB, tm)
    return [slice(s, s + step) for s in range(0, tm, step)]


def _once_per_row_tile(block_shape, index_map):
    return pl.BlockSpec(block_shape, index_map, pipeline_mode=pl.Buffered(1))


def _inproj_body(x_ref, g_ref, w_ref, *rest, w_is_nk, n_main, has_gates, has_tail):
    outs = list(rest[:1 + has_gates + has_tail])
    h_ref = rest[-1]
    o_ref = outs.pop(0)
    gates_ref = outs.pop(0) if has_gates else None
    tail_ref = outs.pop(0) if has_tail else None
    c = pl.program_id(1)

    @pl.when(c == 0)
    def _():
        for rows in _row_slabs(x_ref.shape[0]):
            x = x_ref[rows, :]
            ms = jnp.mean(x * x, axis=-1, keepdims=True)
            h_ref[rows, :] = (x * lax.rsqrt(ms + EPS) * g_ref[...]).astype(BF16)

    w = w_ref[...].astype(BF16)
    contract = (((1,), (1,)), ((), ())) if w_is_nk else (((1,), (0,)), ((), ()))
    acc = lax.dot_general(h_ref[...], w, contract, preferred_element_type=F32)

    def write_main():
        o_ref[...] = acc.astype(o_ref.dtype)
        if has_tail:
            tail_ref[...] = acc[acc.shape[0] - TAIL_ROWS:, :]

    if has_gates:
        pl.when(c < n_main)(write_main)

        @pl.when(c == n_main)
        def _():
            gates_ref[...] = acc[:, :LANES]
    else:
        write_main()


def _inproj(x2d, g_all, layer, w_all, j, tm, tn, out_dtype, want_tail=False):
    m, d = x2d.shape
    n = w_all.shape[-1]
    n_main = n // tn
    has_gates = n % tn != 0
    assert n - n_main * tn <= LANES
    w_is_nk = n % LANES != 0
    if w_is_nk:
        w_all = jnp.swapaxes(w_all, 1, 2)
        w_spec = pl.BlockSpec((None, tn, d), lambda i, c: (j, c, 0))
    else:
        w_spec = pl.BlockSpec((None, d, tn), lambda i, c: (j, 0, c))

    def main_col(c):
        return jnp.minimum(c, n_main - 1)

    out_specs = [pl.BlockSpec((tm, tn), lambda i, c: (i, main_col(c)))]
    out_shape = [jax.ShapeDtypeStruct((m, n_main * tn), out_dtype)]
    if has_gates:
        out_specs.append(pl.BlockSpec((tm, LANES), lambda i, c: (i, 0)))
        out_shape.append(jax.ShapeDtypeStruct((m, LANES), F32))
    if want_tail:
        out_specs.append(pl.BlockSpec((None, TAIL_ROWS, tn), lambda i, c: (i, 0, main_col(c))))
        out_shape.append(jax.ShapeDtypeStruct((m // tm, TAIL_ROWS, n_main * tn), F32))
    res = pl.pallas_call(
        functools.partial(_inproj_body, w_is_nk=w_is_nk, n_main=n_main, has_gates=has_gates,
                          has_tail=want_tail),
        grid=(m // tm, n_main + has_gates),
        in_specs=[_once_per_row_tile((tm, d), lambda i, c: (i, 0)),
                  pl.BlockSpec((None, 1, d), lambda i, c: (layer, 0, 0)),
                  w_spec],
        out_specs=out_specs,
        out_shape=out_shape,
        scratch_shapes=[pltpu.VMEM((tm, d), BF16)],
        compiler_params=_params(("arbitrary", "arbitrary"), 56),
        name="inproj",
    )(x2d, _rows3(g_all), w_all)
    res = list(res)
    proj = res.pop(0)
    gates = res.pop(0) if has_gates else None
    tail = res.pop(0) if want_tail else None
    return proj, gates, tail


def _outproj_body(y_ref, w_ref, o_ref):
    o_ref[...] = jnp.dot(y_ref[...].astype(BF16), w_ref[...].astype(BF16), preferred_element_type=F32)


def _outproj(y2d, w_all, j, tm, tn):
    m, k = y2d.shape
    n = w_all.shape[-1]
    return pl.pallas_call(
        _outproj_body,
        grid=(m // tm, n // tn),
        in_specs=[_once_per_row_tile((tm, k), lambda i, c: (i, 0)),
                  pl.BlockSpec((None, k, tn), lambda i, c: (j, 0, c))],
        out_specs=pl.BlockSpec((tm, tn), lambda i, c: (i, c)),
        out_shape=jax.ShapeDtypeStruct((m, n), F32),
        compiler_params=_params(("arbitrary", "arbitrary"), 56),
        name="outproj",
    )(y2d, w_all)


def _post_ple_body(mix_ref, r_ref, g_ref, p_ref, proj_ref, gate_ref, o_ref, r1_ref, r1b_ref, *, tn):
    c = pl.program_id(1)

    @pl.when(c == 0)
    def _():
        for rows in _row_slabs(mix_ref.shape[0]):
            mix = mix_ref[rows, :]
            ms = jnp.mean(mix * mix, axis=-1, keepdims=True)
            r1 = r_ref[rows, :] + mix * lax.rsqrt(ms + EPS) * g_ref[...]
            r1b_ref[rows, :] = r1.astype(BF16)
            for cc in range(r1.shape[1] // tn):
                r1_ref[cc, rows, :] = r1[:, cc * tn:(cc + 1) * tn]

    gate = jnp.dot(r1b_ref[...], gate_ref[...].astype(BF16), preferred_element_type=F32)
    emb = _bdot(p_ref[...], proj_ref[...])
    o_ref[...] = r1_ref[c] + emb * jax.nn.sigmoid(gate)


def _post_ple(mix, r, g_all, p_all, proj_all, gate_all, layer, tm, tn):
    m, d = r.shape
    pd = p_all.shape[-1]
    return pl.pallas_call(
        functools.partial(_post_ple_body, tn=tn),
        grid=(m // tm, d // tn),
        in_specs=[_once_per_row_tile((tm, d), lambda i, c: (i, 0)),
                  _once_per_row_tile((tm, d), lambda i, c: (i, 0)),
                  pl.BlockSpec((None, 1, d), lambda i, c: (layer, 0, 0)),
                  pl.BlockSpec((None, tm, pd), lambda i, c: (layer, i, 0)),
                  pl.BlockSpec((None, pd, tn), lambda i, c: (layer, 0, c)),
                  pl.BlockSpec((None, d, tn), lambda i, c: (layer, 0, c))],
        out_specs=pl.BlockSpec((tm, tn), lambda i, c: (i, c)),
        out_shape=jax.ShapeDtypeStruct((m, d), F32),
        scratch_shapes=[pltpu.VMEM((d // tn, tm, tn), F32), pltpu.VMEM((tm, d), BF16)],
        compiler_params=_params(("arbitrary", "arbitrary"), 56),
        name="post_ple",
    )(mix, r, _rows3(g_all), p_all, proj_all, gate_all)


def _iota2(n, m, dim):
    return lax.broadcasted_iota(jnp.int32, (n, m), dim)


def _col_to_row(col, eye):
    return jnp.sum(jnp.where(eye, col, 0.0), axis=0, keepdims=True)


def _cumsum_col_row(col, row_i, col_i, eye):
    row = _col_to_row(col, eye)
    c_col = jnp.sum(jnp.where(col_i <= row_i, row, 0.0), axis=1, keepdims=True)
    c_row = jnp.sum(jnp.where(row_i <= col_i, col, 0.0), axis=0, keepdims=True)
    return c_col, c_row


def _pick_lane(blk, lane_iota, idx):
    return jnp.sum(jnp.where(lane_iota == idx, blk, 0.0), axis=1, keepdims=True)


def _head_norm_gate(o, gain, z, center):
    if center:
        o = o - jnp.mean(o, axis=-1, keepdims=True)
    y = o * lax.rsqrt(jnp.mean(o * o, axis=-1, keepdims=True) + EPS) * gain
    return y * _silu(z.astype(F32))


def _row_block(nc):
    return lambda col: (lambda bi, hi, ci: (bi * nc + ci, col(hi)))


class _State:
    def __init__(self, in_ref, out_ref, scr, nc):
        self.in_ref, self.out_ref, self.scr, self.nc = in_ref, out_ref, scr, nc

    def start(self, chunk):
        if self.nc > 1:
            @pl.when(chunk == 0)
            def _():
                if self.in_ref is None:
                    self.scr[...] = jnp.zeros_like(self.scr)
                else:
                    self.scr[...] = self.in_ref[...]

    def get(self, bb, hh):
        if self.nc > 1:
            return self.scr[bb, hh]
        if self.in_ref is None:
            return jnp.zeros(self.out_ref.shape[2:], F32)
        return self.in_ref[bb, hh]

    def put(self, bb, hh, val):
        if self.nc > 1:
            self.scr[bb, hh] = val
        else:
            self.out_ref[bb, hh] = val

    def finish(self, chunk):
        if self.nc > 1:
            @pl.when(chunk == self.nc - 1)
            def _():
                self.out_ref[...] = self.scr[...]


def _ret_body(*refs, L, BB, HB, NC, DK, DV, has_s0, has_prev):
    lg_ref, q_ref, k_ref, v_ref, z_ref, cos_ref, sin_ref, gain_ref = refs[:8]
    s0_ref = refs[8] if has_s0 else None
    rest = refs[8 + has_s0 + has_prev:]
    y_ref, so_ref = rest[:2]
    st = _State(s0_ref, so_ref, rest[2] if NC > 1 else None, NC)
    hb = pl.program_id(1)
    c = pl.program_id(2)
    st.start(c)

    cos = cos_ref[...]
    sin = sin_ref[...]
    half = DK // 2

    def rot(x):
        x1, x2 = x[:, :half], x[:, half:]
        return jnp.concatenate([x1 * cos - x2 * sin, x2 * cos + x1 * sin], axis=-1)

    row_i = _iota2(L, L, 0)
    col_i = _iota2(L, L, 1)
    rel = (row_i - col_i).astype(F32)
    idx = _iota2(L, 1, 0).astype(F32)
    for hh in range(HB):
        lg = lg_ref[hb * HB + hh]
        decay = jnp.exp(jnp.where(rel >= 0, lg * rel, NEG_INF))
        w_in = jnp.exp(lg * (idx + 1.0))
        w_out = jnp.exp(lg * (L - 1.0 - idx))
        w_all = jnp.exp(jnp.full((1, 1), L, F32) * lg)
        hq = slice(hh * DK, (hh + 1) * DK)
        hv = slice(hh * DV, (hh + 1) * DV)
        for bb in range(BB):
            rows = slice(bb * L, (bb + 1) * L)
            q = rot(q_ref[rows, hq].astype(F32))
            k = rot(k_ref[rows, hq].astype(F32)) * DK ** -0.5
            v = v_ref[rows, hv]
            s = st.get(bb, hh)
            scores = _bdot_nt(q, k) * decay
            o = _bdot(scores, v) + _bdot(q * w_in, s)
            st.put(bb, hh, w_all * s + _bdot_tn(k * w_out, v))
            y_ref[rows, hv] = _head_norm_gate(o, gain_ref[:, hv], z_ref[rows, hv], True).astype(y_ref.dtype)
    st.finish(c)


def _retention(proj, b, t, s0_all, j, cos, sin, gain_all, so_prev, n_layers, L, BB, HB, y_dtype):
    h, dk = 8, 256
    dv = gain_all.shape[-1] // h
    nc = t // L
    assert BB == 1 or nc == 1
    hg = h // HB
    rb = _row_block(nc)
    lg = jnp.log1p(-jnp.exp2(-5.0 - jnp.arange(h, dtype=F32)))
    has_s0 = s0_all is not None
    v_blk = (2 * h * dk) // (HB * dv)
    z_blk = (2 * h * dk + h * dv) // (HB * dv)
    in_specs = [pl.BlockSpec(memory_space=pltpu.SMEM),
                pl.BlockSpec((BB * L, HB * dk), rb(lambda hi: hi)),
                pl.BlockSpec((BB * L, HB * dk), rb(lambda hi: hg + hi)),
                pl.BlockSpec((BB * L, HB * dv), rb(lambda hi: v_blk + hi)),
                pl.BlockSpec((BB * L, HB * dv), rb(lambda hi: z_blk + hi)),
                pl.BlockSpec((L, dk // 2), lambda bi, hi, ci: (ci, 0)),
                pl.BlockSpec((L, dk // 2), lambda bi, hi, ci: (ci, 0)),
                pl.BlockSpec((None, 1, HB * dv), lambda bi, hi, ci: (j, 0, hi))]
    args = [lg, proj, proj, proj, proj, cos, sin, _rows3(gain_all)]
    st_spec = pl.BlockSpec((None, BB, HB, dk, dv), lambda bi, hi, ci: (j, bi, hi, 0, 0))
    if has_s0:
        in_specs.append(st_spec)
        args.append(s0_all)
    aliases = {}
    if so_prev is not None:
        in_specs.append(pl.BlockSpec(memory_space=pl.ANY))
        args.append(so_prev)
        aliases = {len(args) - 1: 1}
    return pl.pallas_call(
        functools.partial(_ret_body, L=L, BB=BB, HB=HB, NC=nc, DK=dk, DV=dv, has_s0=has_s0,
                          has_prev=so_prev is not None),
        grid=(b // BB, hg, nc),
        in_specs=in_specs,
        out_specs=[pl.BlockSpec((BB * L, HB * dv), rb(lambda hi: hi)), st_spec],
        out_shape=[jax.ShapeDtypeStruct((b * t, h * dv), y_dtype),
                   jax.ShapeDtypeStruct((n_layers, b, h, dk, dv), F32)],
        scratch_shapes=[pltpu.VMEM((BB, HB, dk, dv), F32)] if nc > 1 else [],
        input_output_aliases=aliases,
        compiler_params=_params(("arbitrary", "arbitrary", "arbitrary"), 48),
        name="retention",
    )(*args)


def _mlstm_body(*refs, L, BB, HB, NC, DK, DV, H, has_s0):
    bg_ref, q_ref, k_ref, v_ref, og_ref, z_ref, gt_ref, gain_ref = refs[:8]
    c0_ref, n0_ref, m0_ref = refs[8:11] if has_s0 else (None, None, None)
    rest = refs[8 + 3 * has_s0:]
    y_ref, co_ref, no_ref, mo_ref = rest[:4]
    scr = rest[4:] if NC > 1 else (None, None, None)
    st_c = _State(c0_ref, co_ref, scr[0], NC)
    st_n = _State(n0_ref, no_ref, scr[1], NC)
    st_m = _State(m0_ref, mo_ref, scr[2], NC)
    hb = pl.program_id(1)
    c = pl.program_id(2)
    for st in (st_c, st_n, st_m):
        st.start(c)

    row_i = _iota2(L, L, 0)
    col_i = _iota2(L, L, 1)
    eye = row_i == col_i
    causal = row_i >= col_i
    lane = _iota2(L, 128, 1)
    for bb in range(BB):
        rows = slice(bb * L, (bb + 1) * L)
        gt = gt_ref[rows, :]
        for hh in range(HB):
            head = hb * HB + hh
            hq = slice(hh * DK, (hh + 1) * DK)
            hv = slice(hh * DV, (hh + 1) * DV)
            ig = _pick_lane(gt, lane, head) + bg_ref[head]
            fg = _pick_lane(gt, lane, H + head) + bg_ref[H + head]
            lf = jax.nn.log_sigmoid(fg)
            b_col, b_row = _cumsum_col_row(lf, row_i, col_i, eye)
            i_row = _col_to_row(ig, eye)
            q = q_ref[rows, hq].astype(F32) * DK ** -0.5
            k = k_ref[rows, hq].astype(F32)
            v = v_ref[rows, hv]
            cm = st_c.get(bb, hh)
            nv = st_n.get(bb, hh)
            m_prev = st_m.get(bb, hh)
            dlog = jnp.where(causal, b_col - b_row + i_row, NEG_INF)
            inter = b_col + m_prev
            mt = jnp.maximum(inter, jnp.max(dlog, axis=1, keepdims=True))
            s = _bdot_nt(q, k) * jnp.exp(dlog - mt)
            wi = jnp.exp(inter - mt)
            num = _bdot(s, v) + wi * _bdot(q, cm)
            den = jnp.sum(s, axis=1, keepdims=True) + wi * jnp.sum(q * nv, axis=1, keepdims=True)
            ht = num / jnp.maximum(jnp.abs(den), jnp.exp(-mt))
            m_new = mt[L - 1:L, :]
            b_last = b_col[L - 1:L, :]
            w_last = jnp.exp(b_last - b_col + ig - m_new)
            dec = jnp.exp(b_last + m_prev - m_new)
            kw = k * w_last
            st_c.put(bb, hh, dec * cm + _bdot_tn(kw, v))
            st_n.put(bb, hh, dec * nv + jnp.sum(kw, axis=0, keepdims=True))
            st_m.put(bb, hh, m_new)
            hcell = ht * jax.nn.sigmoid(og_ref[rows, hv].astype(F32))
            y_ref[rows, hv] = _head_norm_gate(hcell, gain_ref[:, hv], z_ref[rows, hv], True).astype(y_ref.dtype)
    for st in (st_c, st_n, st_m):
        st.finish(c)


def _mlstm(proj, gates, b, t, c0_all, n0_all, m0_all, j, bgate_all, gain_all, L, BB, HB, y_dtype):
    h, dk = 8, 256
    dv = gain_all.shape[-1] // h
    nc = t // L
    assert BB == 1 or nc == 1
    hg = h // HB
    rb = _row_block(nc)
    has_s0 = c0_all is not None
    v_off = 2 * h * dk
    wv = HB * dv
    in_specs = [pl.BlockSpec(memory_space=pltpu.SMEM),
                pl.BlockSpec((BB * L, HB * dk), rb(lambda hi: hi)),
                pl.BlockSpec((BB * L, HB * dk), rb(lambda hi: hg + hi)),
                pl.BlockSpec((BB * L, wv), rb(lambda hi: v_off // wv + hi)),
                pl.BlockSpec((BB * L, wv), rb(lambda hi: (v_off + h * dv) // wv + hi)),
                pl.BlockSpec((BB * L, wv), rb(lambda hi: (v_off + 2 * h * dv) // wv + hi)),
                pl.BlockSpec((BB * L, LANES), rb(lambda hi: 0)),
                pl.BlockSpec((None, 1, wv), lambda bi, hi, ci: (j, 0, hi))]
    args = [bgate_all[j], proj, proj, proj, proj, proj, gates, _rows3(gain_all)]
    c_spec = pl.BlockSpec((None, BB, HB, dk, dv), lambda bi, hi, ci: (j, bi, hi, 0, 0))
    n_spec = pl.BlockSpec((None, BB, HB, 1, dk), lambda bi, hi, ci: (j, bi, hi, 0, 0))
    m_spec = pl.BlockSpec((None, BB, HB, 1, 1), lambda bi, hi, ci: (j, bi, hi, 0, 0))
    nl = 1
    if has_s0:
        assert c0_all.shape[0] == nl
        in_specs += [c_spec, n_spec, m_spec]
        args += [c0_all, n0_all.reshape(nl, b, h, 1, dk), m0_all.reshape(nl, b, h, 1, 1)]
    scratch = [pltpu.VMEM((BB, HB, dk, dv), F32), pltpu.VMEM((BB, HB, 1, dk), F32),
               pltpu.VMEM((BB, HB, 1, 1), F32)] if nc > 1 else []
    y, co, no, mo = pl.pallas_call(
        functools.partial(_mlstm_body, L=L, BB=BB, HB=HB, NC=nc, DK=dk, DV=dv, H=h, has_s0=has_s0),
        grid=(b // BB, hg, nc),
        in_specs=in_specs,
        out_specs=[pl.BlockSpec((BB * L, wv), rb(lambda hi: hi)), c_spec, n_spec, m_spec],
        out_shape=[jax.ShapeDtypeStruct((b * t, h * dv), y_dtype),
                   jax.ShapeDtypeStruct((nl, b, h, dk, dv), F32),
                   jax.ShapeDtypeStruct((nl, b, h, 1, dk), F32),
                   jax.ShapeDtypeStruct((nl, b, h, 1, 1), F32)],
        scratch_shapes=scratch,
        compiler_params=_params(("arbitrary", "arbitrary", "arbitrary"), 48),
        name="mlstm",
    )(*args)
    return y, co, no.reshape(nl, b, h, dk), mo.reshape(nl, b, h)


def _merge_masks(row_i, col_i, L):
    masks = []
    s = 1
    while s < L:
        masks.append(((row_i // (2 * s)) == (col_i // (2 * s))) & ((row_i // s) > (col_i // s)))
        s *= 2
    return masks


def _unit_lower_inverse(a_strict, eye, masks, mm):
    x = jnp.where(eye, 1.0, 0.0) - jnp.where(masks[0], a_strict, 0.0)
    for mask in masks[1:]:
        e = jnp.where(mask, a_strict, 0.0)
        x = x - mm(x, mm(e, x))
    return x


def _vpu_mm(a, b):
    out = a[:, 0:1] * b[0:1, :]
    for kk in range(1, a.shape[1]):
        out = out + a[:, kk:kk + 1] * b[kk:kk + 1, :]
    return out


def _bmm(a, b):
    return jnp.einsum("gik,gkj->gij", a.astype(BF16), b.astype(BF16), preferred_element_type=F32)


def _bmm_nt(a, b):
    return jnp.einsum("gik,gjk->gij", a.astype(BF16), b.astype(BF16), preferred_element_type=F32)


def _bmm_tn(a, b):
    return jnp.einsum("gki,gkj->gij", a.astype(BF16), b.astype(BF16), preferred_element_type=F32)


def _gdn_dims(conv_w_all, a_log_all, gain_all):
    dk = dv = gain_all.shape[-1]
    hv = a_log_all.shape[-1]
    cdim = conv_w_all.shape[-1]
    hk = (cdim - hv * dv) // (2 * dk)
    rep = hv // hk
    assert 2 * hv <= 128 and rep * hk == hv
    return dk, dv, hv, hk, rep, cdim


def _gdn_step_body(al_ref, dt_ref, qx_ref, kx_ref, vx_ref, z_ref, gt_ref, cwq_ref, cwk_ref, cwv_ref,
                   gain_ref, cq0_ref, ck0_ref, cv0_ref, s0_ref, y_ref, so_ref, eq_scr, ek_scr, ev_scr,
                   *, L, BB, HBK, DK, DV, REP, HV):
    hb = pl.program_id(1)
    taps = CONV_W - 1

    def conv(x_ref, c0_ref, scr, cw_ref, bb):
        scr[CONV_PAD - taps:CONV_PAD, :] = c0_ref[bb]
        scr[CONV_PAD:CONV_PAD + L, :] = x_ref[bb * L:(bb + 1) * L, :].astype(F32)
        acc = scr[pl.ds(CONV_PAD - taps, L), :] * cw_ref[0:1, :]
        for w in range(1, CONV_W):
            acc = acc + scr[pl.ds(CONV_PAD - taps + w, L), :] * cw_ref[w:w + 1, :]
        return _silu(acc)

    row_i = _iota2(L, L, 0)
    col_i = _iota2(L, L, 1)
    eye = row_i == col_i
    incl = row_i >= col_i
    strict = row_i > col_i
    lane = _iota2(L, 128, 1)
    masks = _merge_masks(row_i, col_i, L)
    for bb in range(BB):
        rows = slice(bb * L, (bb + 1) * L)
        cq = conv(qx_ref, cq0_ref, eq_scr, cwq_ref, bb)
        ck = conv(kx_ref, ck0_ref, ek_scr, cwk_ref, bb)
        cv = conv(vx_ref, cv0_ref, ev_scr, cwv_ref, bb)
        gt = gt_ref[rows, :]
        for kh in range(HBK):
            q = cq[:, kh * DK:(kh + 1) * DK]
            k = ck[:, kh * DK:(kh + 1) * DK]
            q = q * lax.rsqrt(jnp.sum(q * q, axis=-1, keepdims=True) + EPS) * DK ** -0.5
            k = k * lax.rsqrt(jnp.sum(k * k, axis=-1, keepdims=True) + EPS)
            kk = _bdot_nt(k, k)
            qk = _bdot_nt(q, k)
            for r in range(REP):
                vh = kh * REP + r
                head = (hb * HBK + kh) * REP + r
                hv = slice(vh * DV, (vh + 1) * DV)
                v = cv[:, hv]
                beta = jax.nn.sigmoid(_pick_lane(gt, lane, head))
                a_neg = -jnp.exp(jnp.full((1, 1), al_ref[head], F32))
                g = a_neg * jax.nn.softplus(_pick_lane(gt, lane, HV + head) + dt_ref[head])
                g_col, g_row = _cumsum_col_row(g, row_i, col_i, eye)
                decay = jnp.exp(jnp.where(incl, g_col - g_row, NEG_INF))
                a = jnp.where(strict, beta * kk * decay, 0.0)
                x = _unit_lower_inverse(a, eye, masks, _vpu_mm)
                s = s0_ref[bb, vh]
                eg = jnp.exp(g_col)
                rhs = beta * v - (beta * eg) * _bdot(k, s)
                u = _vpu_mm(x, rhs)
                o = eg * _bdot(q, s) + _vpu_mm(qk * decay, u)
                g_last = g_col[L - 1:L, :]
                kw = k * jnp.exp(g_last - g_col)
                so_ref[bb, vh] = jnp.exp(g_last) * s + _bdot_tn(kw, u)
                y_ref[rows, hv] = _head_norm_gate(o, gain_ref[...], z_ref[rows, hv], False).astype(y_ref.dtype)


def _gdn_step(proj, gates, b, t, s0_all, conv0_all, j, conv_w_all, a_log_all, dt_bias_all, gain_all,
              BB, HBK, y_dtype):
    dk, dv, hv, hk, rep, cdim = _gdn_dims(conv_w_all, a_log_all, gain_all)
    L = t
    assert L <= 8, "this form keeps the chunk's triangular solve on the vector unit"
    hg = hk // HBK
    wq, wv = HBK * dk, HBK * rep * dv
    k_blk = (hk * dk) // wq
    v_blk = (2 * hk * dk) // wv
    z_blk = cdim // wv
    rb = _row_block(1)

    def cspec(width, off):
        return pl.BlockSpec((None, CONV_W, width), lambda bi, hi, ci: (j, 0, off + hi))

    def c0spec(width, off):
        return pl.BlockSpec((None, BB, CONV_W - 1, width), lambda bi, hi, ci: (j, bi, 0, off + hi))

    st_spec = pl.BlockSpec((None, BB, HBK * rep, dk, dv), lambda bi, hi, ci: (j, bi, hi, 0, 0))
    in_specs = [pl.BlockSpec(memory_space=pltpu.SMEM),
                pl.BlockSpec(memory_space=pltpu.SMEM),
                pl.BlockSpec((BB * L, wq), rb(lambda hi: hi)),
                pl.BlockSpec((BB * L, wq), rb(lambda hi: k_blk + hi)),
                pl.BlockSpec((BB * L, wv), rb(lambda hi: v_blk + hi)),
                pl.BlockSpec((BB * L, wv), rb(lambda hi: z_blk + hi)),
                pl.BlockSpec((BB * L, LANES), rb(lambda hi: 0)),
                cspec(wq, 0), cspec(wq, k_blk), cspec(wv, v_blk),
                pl.BlockSpec((None, 1, dv), lambda bi, hi, ci: (j, 0, 0)),
                c0spec(wq, 0), c0spec(wq, k_blk), c0spec(wv, v_blk), st_spec]
    args = [a_log_all[j], dt_bias_all[j], proj, proj, proj, proj, gates,
            conv_w_all, conv_w_all, conv_w_all, _rows3(gain_all),
            conv0_all, conv0_all, conv0_all, s0_all]
    y, so = pl.pallas_call(
        functools.partial(_gdn_step_body, L=L, BB=BB, HBK=HBK, DK=dk, DV=dv, REP=rep, HV=hv),
        grid=(b // BB, hg, 1),
        in_specs=in_specs,
        out_specs=[pl.BlockSpec((BB * L, wv), rb(lambda hi: hi)), st_spec],
        out_shape=[jax.ShapeDtypeStruct((b * t, hv * dv), y_dtype),
                   jax.ShapeDtypeStruct((1, b, hv, dk, dv), F32)],
        scratch_shapes=[pltpu.VMEM((L + CONV_PAD, wq), F32), pltpu.VMEM((L + CONV_PAD, wq), F32),
                        pltpu.VMEM((L + CONV_PAD, wv), F32)],
        compiler_params=_params(("arbitrary", "arbitrary", "arbitrary"), 48),
        name="gdn_step",
    )(*args)
    return y, so


def _gdn_seq_body(al_ref, dt_ref, qx_ref, kx_ref, vx_ref, z_ref, gt_ref, cwq_ref, cwk_ref, cwv_ref,
                  gain_ref, y_ref, so_ref, s_scr, eq_scr, ek_scr, ev_scr,
                  *, TB, C, HBK, NTB, DK, DV, REP, HV):
    hb = pl.program_id(1)
    tb = pl.program_id(2)
    G = TB // C
    taps = CONV_W - 1

    @pl.when(tb == 0)
    def _():
        for scr in (eq_scr, ek_scr, ev_scr):
            scr[0:CONV_PAD, :] = jnp.zeros((CONV_PAD, scr.shape[1]), F32)
        s_scr[...] = jnp.zeros_like(s_scr)

    def conv(x_ref, scr, cw_ref):
        scr[CONV_PAD:CONV_PAD + TB, :] = x_ref[...].astype(F32)
        acc = scr[pl.ds(CONV_PAD - taps, TB), :] * cw_ref[0:1, :]
        for w in range(1, CONV_W):
            acc = acc + scr[pl.ds(CONV_PAD - taps + w, TB), :] * cw_ref[w:w + 1, :]
        if NTB > 1:
            scr[0:CONV_PAD, :] = scr[TB:TB + CONV_PAD, :]
        return _silu(acc)

    cq = conv(qx_ref, eq_scr, cwq_ref)
    ck = conv(kx_ref, ek_scr, cwk_ref)
    cv = conv(vx_ref, ev_scr, cwv_ref)

    row_i = _iota2(C, C, 0)
    col_i = _iota2(C, C, 1)
    eye = row_i == col_i
    incl = row_i >= col_i
    PW = REP * C
    prow = _iota2(C, PW, 0)
    plane = _iota2(C, PW, 1)
    pcol = plane % C
    phead = plane // C
    p_eye = prow == pcol
    p_incl = prow >= pcol
    p_strict = prow > pcol
    p_masks = _merge_masks(prow, pcol, C)
    bd_mask = (_iota2(PW, PW, 0) // C) == (_iota2(PW, PW, 1) // C)

    def block_diag(xp):
        return jnp.where(bd_mask, jnp.concatenate([xp] * REP, axis=1), 0.0)

    def packed_mm(ap, bp):
        return _bmm(ap, block_diag(bp))

    def pack_cols(cols):
        out = cols[0]
        for r in range(1, REP):
            out = jnp.where(phead >= r, cols[r], out)
        return out

    lane = _iota2(TB, 128, 1)
    gt = gt_ref[...]
    for kh in range(HBK):
        q = cq[:, kh * DK:(kh + 1) * DK]
        k = ck[:, kh * DK:(kh + 1) * DK]
        q = q * lax.rsqrt(jnp.sum(q * q, axis=-1, keepdims=True) + EPS) * DK ** -0.5
        k = k * lax.rsqrt(jnp.sum(k * k, axis=-1, keepdims=True) + EPS)
        q3 = q.reshape(G, C, DK)
        k3 = k.reshape(G, C, DK)
        kq = _bmm_nt(jnp.concatenate([k3, q3], axis=1), k3)
        kk, qk = kq[:, :C, :], kq[:, C:, :]
        betas, g_cols, g_rows = [], [], []
        for r in range(REP):
            head = (hb * HBK + kh) * REP + r
            betas.append(jax.nn.sigmoid(_pick_lane(gt, lane, head)).reshape(G, C, 1))
            a_neg = -jnp.exp(jnp.full((1, 1), al_ref[head], F32))
            g = (a_neg * jax.nn.softplus(_pick_lane(gt, lane, HV + head) + dt_ref[head])).reshape(G, C, 1)
            g_lanes = jnp.sum(jnp.where(eye, g, 0.0), axis=1, keepdims=True)
            g_cols.append(jnp.sum(jnp.where(incl, g_lanes, 0.0), axis=2, keepdims=True))
            g_rows.append(jnp.sum(jnp.where(row_i <= col_i, g, 0.0), axis=1, keepdims=True))
        decay_p = jnp.exp(jnp.where(p_incl, pack_cols(g_cols) - jnp.concatenate(g_rows, axis=-1), NEG_INF))
        a_p = jnp.where(p_strict, pack_cols(betas) * jnp.concatenate([kk] * REP, axis=-1) * decay_p, 0.0)
        x_p = _unit_lower_inverse(a_p, p_eye, p_masks, packed_mm)
        for r in range(REP):
            vh = kh * REP + r
            hv = slice(vh * DV, (vh + 1) * DV)
            beta, g_col = betas[r], g_cols[r]
            x = x_p[:, :, r * C:(r + 1) * C]
            decay = decay_p[:, :, r * C:(r + 1) * C]
            v3 = cv[:, hv].reshape(G, C, DV)
            eg = jnp.exp(g_col)
            wu = _bmm(x, jnp.concatenate([(beta * eg) * k3, beta * v3], axis=-1))
            qo = _bmm(qk * decay, wu)
            o0 = qo[:, :, DK:]
            g_last = g_col[:, C - 1:C, :]
            mb = _bmm_tn(k3 * jnp.exp(g_last - g_col), wu)
            b_eff = mb[:, :, DK:]
            lhs = jnp.concatenate([eg * q3 - qo[:, :, :DK], mb[:, :, :DK]], axis=1).astype(BF16)
            eg_last = jnp.exp(g_last)
            s = s_scr[vh]
            outs = []
            for c in range(G):
                both = jnp.dot(lhs[c], s.astype(BF16), preferred_element_type=F32)
                outs.append(both[:C] + o0[c])
                s = eg_last[c] * s - both[C:] + b_eff[c]
            s_scr[vh] = s
            o = jnp.concatenate(outs, axis=0) if G > 1 else outs[0]
            y_ref[:, hv] = _head_norm_gate(o, gain_ref[...], z_ref[:, hv], False).astype(y_ref.dtype)

    @pl.when(tb == NTB - 1)
    def _():
        so_ref[...] = s_scr[...]


def _gdn_seq(proj, gates, b, t, j, conv_w_all, a_log_all, dt_bias_all, gain_all, TB, C, HBK, y_dtype):
    dk, dv, hv, hk, rep, cdim = _gdn_dims(conv_w_all, a_log_all, gain_all)
    assert dk == dv
    ntb = t // TB
    hg = hk // HBK
    wq, wv = HBK * dk, HBK * rep * dv
    k_blk = (hk * dk) // wq
    v_blk = (2 * hk * dk) // wv
    z_blk = cdim // wv
    rb = _row_block(ntb)

    def cspec(width, off):
        return pl.BlockSpec((None, CONV_W, width), lambda bi, hi, ti: (j, 0, off + hi))

    in_specs = [pl.BlockSpec(memory_space=pltpu.SMEM),
                pl.BlockSpec(memory_space=pltpu.SMEM),
                pl.BlockSpec((TB, wq), rb(lambda hi: hi)),
                pl.BlockSpec((TB, wq), rb(lambda hi: k_blk + hi)),
                pl.BlockSpec((TB, wv), rb(lambda hi: v_blk + hi)),
                pl.BlockSpec((TB, wv), rb(lambda hi: z_blk + hi)),
                pl.BlockSpec((TB, LANES), rb(lambda hi: 0)),
                cspec(wq, 0), cspec(wq, k_blk), cspec(wv, v_blk),
                pl.BlockSpec((None, 1, dv), lambda bi, hi, ti: (j, 0, 0))]
    args = [a_log_all[j], dt_bias_all[j], proj, proj, proj, proj, gates,
            conv_w_all, conv_w_all, conv_w_all, _rows3(gain_all)]
    st_spec = pl.BlockSpec((None, None, HBK * rep, dk, dv), lambda bi, hi, ti: (0, bi, hi, 0, 0))
    return pl.pallas_call(
        functools.partial(_gdn_seq_body, TB=TB, C=C, HBK=HBK, NTB=ntb, DK=dk, DV=dv, REP=rep, HV=hv),
        grid=(b, hg, ntb),
        in_specs=in_specs,
        out_specs=[pl.BlockSpec((TB, wv), rb(lambda hi: hi)), st_spec],
        out_shape=[jax.ShapeDtypeStruct((b * t, hv * dv), y_dtype),
                   jax.ShapeDtypeStruct((1, b, hv, dk, dv), F32)],
        scratch_shapes=[pltpu.VMEM((HBK * rep, dk, dv), F32),
                        pltpu.VMEM((TB + CONV_PAD, wq), F32), pltpu.VMEM((TB + CONV_PAD, wq), F32),
                        pltpu.VMEM((TB + CONV_PAD, wv), F32)],
        compiler_params=_params(("arbitrary", "arbitrary", "arbitrary"), 48),
        name="gdn_seq",
    )(*args)


def _rope_tables(pos, dk):
    half = dk // 2
    inv = ROPE_BASE ** (-jnp.arange(half, dtype=F32) / half)
    ang = pos.astype(F32)[:, None] * inv[None, :]
    return jnp.cos(ang), jnp.sin(ang)


def _trunk(x, p, states, pos, w, cfg):
    (norm_pre, norm_post, ple_proj, ple_gate, ret_w_in, ret_head_norm, ret_w_out,
     mlstm_w_in, mlstm_b_gate, mlstm_head_norm, mlstm_w_out,
     gdn_w_in, gdn_conv_w, gdn_a_log, gdn_dt_bias, gdn_head_norm, gdn_w_out) = w
    ret_s, ml_c, ml_n, ml_m, gdn_s, gdn_conv = states
    b, t, d = x.shape
    depth = norm_pre.shape[0]
    n_ret = ret_w_in.shape[0]
    m = b * t
    tm, ydt, pdt = cfg["tm"], cfg["y_dtype"], cfg["proj_dtype"]
    cos, sin = _rope_tables(pos, 256)
    r = x.reshape(m, d)
    p2 = p.reshape(depth, m, p.shape[-1])
    ret_out = None
    outs = {}
    keep = CONV_W - 1
    for i in range(depth):
        kind, j = i % 3, i // 3
        if kind == 0:
            proj, _, _ = _inproj(r, norm_pre, i, ret_w_in, j, tm, 512, pdt)
            y, ret_out = _retention(proj, b, t, ret_s, j, cos, sin, ret_head_norm, ret_out, n_ret,
                                    cfg["ret_L"], cfg["ret_BB"], cfg["ret_HB"], ydt)
            w_out = ret_w_out
        elif kind == 1:
            proj, gates, _ = _inproj(r, norm_pre, i, mlstm_w_in, j, tm, 512, pdt)
            y, outs["c"], outs["n"], outs["m"] = _mlstm(
                proj, gates, b, t, ml_c, ml_n, ml_m, j, mlstm_b_gate, mlstm_head_norm,
                cfg["ml_L"], cfg["ml_BB"], cfg["ml_HB"], ydt)
            w_out = mlstm_w_out
        else:
            cdim = gdn_conv_w.shape[-1]
            from_tail = pdt != F32
            assert t >= keep and (not from_tail or (t % tm == 0 and keep <= TAIL_ROWS))
            proj, gates, tail = _inproj(r, norm_pre, i, gdn_w_in, j, tm, 512, pdt, want_tail=from_tail)
            if gdn_s is None:
                y, outs["gs"] = _gdn_seq(proj, gates, b, t, j, gdn_conv_w, gdn_a_log, gdn_dt_bias,
                                         gdn_head_norm, cfg["gdn_TB"], cfg["gdn_L"], cfg["gdn_HBK"], ydt)
            else:
                y, outs["gs"] = _gdn_step(proj, gates, b, t, gdn_s, gdn_conv, j, gdn_conv_w, gdn_a_log,
                                          gdn_dt_bias, gdn_head_norm, cfg["gdn_BB"], cfg["gdn_HBK"], ydt)
            if from_tail:
                last_tiles = tail.reshape(b, t // tm, TAIL_ROWS, -1)[:, -1]
                outs["gc"] = last_tiles[:, TAIL_ROWS - keep:, :cdim][None]
            else:
                outs["gc"] = proj.reshape(b, t, -1)[:, t - keep:, :cdim][None]
            w_out = gdn_w_out
        mix = _outproj(y, w_out, j, tm, cfg["tn_out"])
        r = _post_ple(mix, r, norm_post, p2, ple_proj, ple_gate, i, cfg["tm_ple"], 512)
    return (r.reshape(b, t, d), ret_out, outs["c"], outs["n"], outs["m"], outs["gs"], outs["gc"])


_PROMPT_CFG = dict(tm=2048, tn_out=512, tm_ple=1024, y_dtype=BF16, proj_dtype=BF16,
                   ret_L=256, ret_BB=1, ret_HB=1, ml_L=256, ml_BB=1, ml_HB=1,
                   gdn_TB=512, gdn_L=64, gdn_HBK=1)
_SAMPLE_CFG = dict(tm=512, tn_out=512, tm_ple=512, y_dtype=F32, proj_dtype=F32,
                   ret_L=4, ret_BB=2, ret_HB=4, ml_L=4, ml_BB=2, ml_HB=4, gdn_BB=2, gdn_HBK=4)


def kernel(x_prompt, x_sample, state_ret_S, state_mlstm_C, state_mlstm_n, state_mlstm_m, state_gdn_S, state_gdn_conv, p_prompt, p_sample, norm_pre, norm_post, ple_proj, ple_gate, ret_w_in, ret_head_norm, ret_w_out, mlstm_w_in, mlstm_b_gate, mlstm_head_norm, mlstm_w_out, gdn_w_in, gdn_conv_w, gdn_a_log, gdn_dt_bias, gdn_head_norm, gdn_w_out):
    w = (norm_pre, norm_post, ple_proj, ple_gate, ret_w_in, ret_head_norm, ret_w_out,
         mlstm_w_in, mlstm_b_gate, mlstm_head_norm, mlstm_w_out,
         gdn_w_in, gdn_conv_w, gdn_a_log, gdn_dt_bias, gdn_head_norm, gdn_w_out)
    pos_p = jnp.arange(x_prompt.shape[1])
    yp, ret_p, mc_p, mn_p, mm_p, gs_p, gc_p = _trunk(
        x_prompt, p_prompt, (None,) * 6, pos_p, w, _PROMPT_CFG)
    pos_s = PAST_LEN + jnp.arange(x_sample.shape[1])
    ys, ret_s, mc_s, mn_s, mm_s, gs_s, gc_s = _trunk(
        x_sample, p_sample,
        (state_ret_S, state_mlstm_C, state_mlstm_n, state_mlstm_m, state_gdn_S, state_gdn_conv),
        pos_s, w, _SAMPLE_CFG)
    return (yp, ys, ret_p, mc_p, mn_p, mm_p, gs_p, gc_p, ret_s, mc_s, mn_s, mm_s, gs_s, gc_s)
```

```python
import functools

import jax
import jax.numpy as jnp
from jax import lax
from jax.experimental import pallas as pl
from jax.experimental.pallas import tpu as pltpu

F32 = jnp.float32
BF16 = jnp.bfloat16
EPS = 1e-6
ROPE_BASE = 10000.0
CONV_W = 4
PAST_LEN = 16384
MIB = 1024 * 1024
NEG_INF = float("-inf")
CONV_PAD = 8
LANES = 128
TAIL_ROWS = 8


def _params(sem, vmem_mib):
    return pltpu.CompilerParams(dimension_semantics=sem, vmem_limit_bytes=vmem_mib * MIB)


def _bdot(a, b):
    return jnp.dot(a.astype(BF16), b.astype(BF16), preferred_element_type=F32)


def _bdot_nt(a, b):
    return lax.dot_general(a.astype(BF16), b.astype(BF16), (((1,), (1,)), ((), ())),
                           preferred_element_type=F32)


def _bdot_tn(a, b):
    return lax.dot_general(a.astype(BF16), b.astype(BF16), (((0,), (0,)), ((), ())),
                           preferred_element_type=F32)


def _silu(x):
    return x * jax.nn.sigmoid(x)


def _rows3(table):
    return table.reshape(table.shape[0], 1, table.shape[1])


SLAB = 256


def _row_slabs(tm):
    step = min(SLAB, tm)
    return [slice(s, s + step) for s in range(0, tm, step)]


def _once_per_row_tile(block_shape, index_map):
    return pl.BlockSpec(block_shape, index_map, pipeline_mode=pl.Buffered(1))


def _inproj_body(x_ref, g_ref, w_ref, *rest, w_is_nk, n_main, has_gates, has_tail):
    outs = list(rest[:1 + has_gates + has_tail])
    h_ref = rest[-1]
    o_ref = outs.pop(0)
    gates_ref = outs.pop(0) if has_gates else None
    tail_ref = outs.pop(0) if has_tail else None
    c = pl.program_id(1)

    @pl.when(c == 0)
    def _():
        for rows in _row_slabs(x_ref.shape[0]):
            x = x_ref[rows, :]
            ms = jnp.mean(x * x, axis=-1, keepdims=True)
            h_ref[rows, :] = (x * lax.rsqrt(ms + EPS) * g_ref[...]).astype(BF16)

    w = w_ref[...].astype(BF16)
    contract = (((1,), (1,)), ((), ())) if w_is_nk else (((1,), (0,)), ((), ()))
    acc = lax.dot_general(h_ref[...], w, contract, preferred_element_type=F32)

    def write_main():
        o_ref[...] = acc.astype(o_ref.dtype)
        if has_tail:
            tail_ref[...] = acc[acc.shape[0] - TAIL_ROWS:, :]

    if has_gates:
        pl.when(c < n_main)(write_main)

        @pl.when(c == n_main)
        def _():
            gates_ref[...] = acc[:, :LANES]
    else:
        write_main()


def _inproj(x2d, g_all, layer, w_all, j, tm, tn, out_dtype, want_tail=False):
    m, d = x2d.shape
    n = w_all.shape[-1]
    n_main = n // tn
    has_gates = n % tn != 0
    assert n - n_main * tn <= LANES
    w_is_nk = n % LANES != 0
    if w_is_nk:
        w_all = jnp.swapaxes(w_all, 1, 2)
        w_spec = pl.BlockSpec((None, tn, d), lambda i, c: (j, c, 0))
    else:
        w_spec = pl.BlockSpec((None, d, tn), lambda i, c: (j, 0, c))

    def main_col(c):
        return jnp.minimum(c, n_main - 1)

    out_specs = [pl.BlockSpec((tm, tn), lambda i, c: (i, main_col(c)))]
    out_shape = [jax.ShapeDtypeStruct((m, n_main * tn), out_dtype)]
    if has_gates:
        out_specs.append(pl.BlockSpec((tm, LANES), lambda i, c: (i, 0)))
        out_shape.append(jax.ShapeDtypeStruct((m, LANES), F32))
    if want_tail:
        out_specs.append(pl.BlockSpec((None, TAIL_ROWS, tn), lambda i, c: (i, 0, main_col(c))))
        out_shape.append(jax.ShapeDtypeStruct((m // tm, TAIL_ROWS, n_main * tn), F32))
    res = pl.pallas_call(
        functools.partial(_inproj_body, w_is_nk=w_is_nk, n_main=n_main, has_gates=has_gates,
                          has_tail=want_tail),
        grid=(m // tm, n_main + has_gates),
        in_specs=[_once_per_row_tile((tm, d), lambda i, c: (i, 0)),
                  pl.BlockSpec((None, 1, d), lambda i, c: (layer, 0, 0)),
                  w_spec],
        out_specs=out_specs,
        out_shape=out_shape,
        scratch_shapes=[pltpu.VMEM((tm, d), BF16)],
        compiler_params=_params(("arbitrary", "arbitrary"), 56),
        name="inproj",
    )(x2d, _rows3(g_all), w_all)
    res = list(res)
    proj = res.pop(0)
    gates = res.pop(0) if has_gates else None
    tail = res.pop(0) if want_tail else None
    return proj, gates, tail


def _outproj_body(y_ref, w_ref, o_ref):
    o_ref[...] = jnp.dot(y_ref[...].astype(BF16), w_ref[...].astype(BF16), preferred_element_type=F32)


def _outproj(y2d, w_all, j, tm, tn):
    m, k = y2d.shape
    n = w_all.shape[-1]
    return pl.pallas_call(
        _outproj_body,
        grid=(m // tm, n // tn),
        in_specs=[_once_per_row_tile((tm, k), lambda i, c: (i, 0)),
                  pl.BlockSpec((None, k, tn), lambda i, c: (j, 0, c))],
        out_specs=pl.BlockSpec((tm, tn), lambda i, c: (i, c)),
        out_shape=jax.ShapeDtypeStruct((m, n), F32),
        compiler_params=_params(("arbitrary", "arbitrary"), 56),
        name="outproj",
    )(y2d, w_all)


def _post_ple_body(mix_ref, r_ref, g_ref, p_ref, proj_ref, gate_ref, o_ref, r1_ref, r1b_ref, *, tn):
    c = pl.program_id(1)

    @pl.when(c == 0)
    def _():
        for rows in _row_slabs(mix_ref.shape[0]):
            mix = mix_ref[rows, :]
            ms = jnp.mean(mix * mix, axis=-1, keepdims=True)
            r1 = r_ref[rows, :] + mix * lax.rsqrt(ms + EPS) * g_ref[...]
            r1b_ref[rows, :] = r1.astype(BF16)
            for cc in range(r1.shape[1] // tn):
                r1_ref[cc, rows, :] = r1[:, cc * tn:(cc + 1) * tn]

    gate = jnp.dot(r1b_ref[...], gate_ref[...].astype(BF16), preferred_element_type=F32)
    emb = _bdot(p_ref[...], proj_ref[...])
    o_ref[...] = r1_ref[c] + emb * jax.nn.sigmoid(gate)


def _post_ple(mix, r, g_all, p_all, proj_all, gate_all, layer, tm, tn):
    m, d = r.shape
    pd = p_all.shape[-1]
    return pl.pallas_call(
        functools.partial(_post_ple_body, tn=tn),
        grid=(m // tm, d // tn),
        in_specs=[_once_per_row_tile((tm, d), lambda i, c: (i, 0)),
                  _once_per_row_tile((tm, d), lambda i, c: (i, 0)),
                  pl.BlockSpec((None, 1, d), lambda i, c: (layer, 0, 0)),
                  pl.BlockSpec((None, tm, pd), lambda i, c: (layer, i, 0)),
                  pl.BlockSpec((None, pd, tn), lambda i, c: (layer, 0, c)),
                  pl.BlockSpec((None, d, tn), lambda i, c: (layer, 0, c))],
        out_specs=pl.BlockSpec((tm, tn), lambda i, c: (i, c)),
        out_shape=jax.ShapeDtypeStruct((m, d), F32),
        scratch_shapes=[pltpu.VMEM((d // tn, tm, tn), F32), pltpu.VMEM((tm, d), BF16)],
        compiler_params=_params(("arbitrary", "arbitrary"), 56),
        name="post_ple",
    )(mix, r, _rows3(g_all), p_all, proj_all, gate_all)


def _iota2(n, m, dim):
    return lax.broadcasted_iota(jnp.int32, (n, m), dim)


def _col_to_row(col, eye):
    return jnp.sum(jnp.where(eye, col, 0.0), axis=0, keepdims=True)


def _cumsum_col_row(col, row_i, col_i, eye):
    row = _col_to_row(col, eye)
    c_col = jnp.sum(jnp.where(col_i <= row_i, row, 0.0), axis=1, keepdims=True)
    c_row = jnp.sum(jnp.where(row_i <= col_i, col, 0.0), axis=0, keepdims=True)
    return c_col, c_row


def _pick_lane(blk, lane_iota, idx):
    return jnp.sum(jnp.where(lane_iota == idx, blk, 0.0), axis=1, keepdims=True)


def _head_norm_gate(o, gain, z, center):
    if center:
        o = o - jnp.mean(o, axis=-1, keepdims=True)
    y = o * lax.rsqrt(jnp.mean(o * o, axis=-1, keepdims=True) + EPS) * gain
    return y * _silu(z.astype(F32))


def _row_block(nc):
    return lambda col: (lambda bi, hi, ci: (bi * nc + ci, col(hi)))


class _State:
    def __init__(self, in_ref, out_ref, scr, nc):
        self.in_ref, self.out_ref, self.scr, self.nc = in_ref, out_ref, scr, nc

    def start(self, chunk):
        if self.nc > 1:
            @pl.when(chunk == 0)
            def _():
                if self.in_ref is None:
                    self.scr[...] = jnp.zeros_like(self.scr)
                else:
                    self.scr[...] = self.in_ref[...]

    def get(self, bb, hh):
        if self.nc > 1:
            return self.scr[bb, hh]
        if self.in_ref is None:
            return jnp.zeros(self.out_ref.shape[2:], F32)
        return self.in_ref[bb, hh]

    def put(self, bb, hh, val):
        if self.nc > 1:
            self.scr[bb, hh] = val
        else:
            self.out_ref[bb, hh] = val

    def finish(self, chunk):
        if self.nc > 1:
            @pl.when(chunk == self.nc - 1)
            def _():
                self.out_ref[...] = self.scr[...]


def _ret_body(*refs, L, BB, HB, NC, DK, DV, has_s0, has_prev):
    lg_ref, q_ref, k_ref, v_ref, z_ref, cos_ref, sin_ref, gain_ref = refs[:8]
    s0_ref = refs[8] if has_s0 else None
    rest = refs[8 + has_s0 + has_prev:]
    y_ref, so_ref = rest[:2]
    st = _State(s0_ref, so_ref, rest[2] if NC > 1 else None, NC)
    hb = pl.program_id(1)
    c = pl.program_id(2)
    st.start(c)

    cos = cos_ref[...]
    sin = sin_ref[...]
    half = DK // 2

    def rot(x):
        x1, x2 = x[:, :half], x[:, half:]
        return jnp.concatenate([x1 * cos - x2 * sin, x2 * cos + x1 * sin], axis=-1)

    row_i = _iota2(L, L, 0)
    col_i = _iota2(L, L, 1)
    rel = (row_i - col_i).astype(F32)
    idx = _iota2(L, 1, 0).astype(F32)
    for hh in range(HB):
        lg = lg_ref[hb * HB + hh]
        decay = jnp.exp(jnp.where(rel >= 0, lg * rel, NEG_INF))
        w_in = jnp.exp(lg * (idx + 1.0))
        w_out = jnp.exp(lg * (L - 1.0 - idx))
        w_all = jnp.exp(jnp.full((1, 1), L, F32) * lg)
        hq = slice(hh * DK, (hh + 1) * DK)
        hv = slice(hh * DV, (hh + 1) * DV)
        for bb in range(BB):
            rows = slice(bb * L, (bb + 1) * L)
            q = rot(q_ref[rows, hq].astype(F32))
            k = rot(k_ref[rows, hq].astype(F32)) * DK ** -0.5
            v = v_ref[rows, hv]
            s = st.get(bb, hh)
            scores = _bdot_nt(q, k) * decay
            o = _bdot(scores, v) + _bdot(q * w_in, s)
            st.put(bb, hh, w_all * s + _bdot_tn(k * w_out, v))
            y_ref[rows, hv] = _head_norm_gate(o, gain_ref[:, hv], z_ref[rows, hv], True).astype(y_ref.dtype)
    st.finish(c)


def _retention(proj, b, t, s0_all, j, cos, sin, gain_all, so_prev, n_layers, L, BB, HB, y_dtype):
    h, dk = 8, 256
    dv = gain_all.shape[-1] // h
    nc = t // L
    assert BB == 1 or nc == 1
    hg = h // HB
    rb = _row_block(nc)
    lg = jnp.log1p(-jnp.exp2(-5.0 - jnp.arange(h, dtype=F32)))
    has_s0 = s0_all is not None
    v_blk = (2 * h * dk) // (HB * dv)
    z_blk = (2 * h * dk + h * dv) // (HB * dv)
    in_specs = [pl.BlockSpec(memory_space=pltpu.SMEM),
                pl.BlockSpec((BB * L, HB * dk), rb(lambda hi: hi)),
                pl.BlockSpec((BB * L, HB * dk), rb(lambda hi: hg + hi)),
                pl.BlockSpec((BB * L, HB * dv), rb(lambda hi: v_blk + hi)),
                pl.BlockSpec((BB * L, HB * dv), rb(lambda hi: z_blk + hi)),
                pl.BlockSpec((L, dk // 2), lambda bi, hi, ci: (ci, 0)),
                pl.BlockSpec((L, dk // 2), lambda bi, hi, ci: (ci, 0)),
                pl.BlockSpec((None, 1, HB * dv), lambda bi, hi, ci: (j, 0, hi))]
    args = [lg, proj, proj, proj, proj, cos, sin, _rows3(gain_all)]
    st_spec = pl.BlockSpec((None, BB, HB, dk, dv), lambda bi, hi, ci: (j, bi, hi, 0, 0))
    if has_s0:
        in_specs.append(st_spec)
        args.append(s0_all)
    aliases = {}
    if so_prev is not None:
        in_specs.append(pl.BlockSpec(memory_space=pl.ANY))
        args.append(so_prev)
        aliases = {len(args) - 1: 1}
    return pl.pallas_call(
        functools.partial(_ret_body, L=L, BB=BB, HB=HB, NC=nc, DK=dk, DV=dv, has_s0=has_s0,
                          has_prev=so_prev is not None),
        grid=(b // BB, hg, nc),
        in_specs=in_specs,
        out_specs=[pl.BlockSpec((BB * L, HB * dv), rb(lambda hi: hi)), st_spec],
        out_shape=[jax.ShapeDtypeStruct((b * t, h * dv), y_dtype),
                   jax.ShapeDtypeStruct((n_layers, b, h, dk, dv), F32)],
        scratch_shapes=[pltpu.VMEM((BB, HB, dk, dv), F32)] if nc > 1 else [],
        input_output_aliases=aliases,
        compiler_params=_params(("arbitrary", "arbitrary", "arbitrary"), 48),
        name="retention",
    )(*args)


def _mlstm_body(*refs, L, BB, HB, NC, DK, DV, H, has_s0):
    bg_ref, q_ref, k_ref, v_ref, og_ref, z_ref, gt_ref, gain_ref = refs[:8]
    c0_ref, n0_ref, m0_ref = refs[8:11] if has_s0 else (None, None, None)
    rest = refs[8 + 3 * has_s0:]
    y_ref, co_ref, no_ref, mo_ref = rest[:4]
    scr = rest[4:] if NC > 1 else (None, None, None)
    st_c = _State(c0_ref, co_ref, scr[0], NC)
    st_n = _State(n0_ref, no_ref, scr[1], NC)
    st_m = _State(m0_ref, mo_ref, scr[2], NC)
    hb = pl.program_id(1)
    c = pl.program_id(2)
    for st in (st_c, st_n, st_m):
        st.start(c)

    row_i = _iota2(L, L, 0)
    col_i = _iota2(L, L, 1)
    eye = row_i == col_i
    causal = row_i >= col_i
    lane = _iota2(L, 128, 1)
    for bb in range(BB):
        rows = slice(bb * L, (bb + 1) * L)
        gt = gt_ref[rows, :]
        for hh in range(HB):
            head = hb * HB + hh
            hq = slice(hh * DK, (hh + 1) * DK)
            hv = slice(hh * DV, (hh + 1) * DV)
            ig = _pick_lane(gt, lane, head) + bg_ref[head]
            fg = _pick_lane(gt, lane, H + head) + bg_ref[H + head]
            lf = jax.nn.log_sigmoid(fg)
            b_col, b_row = _cumsum_col_row(lf, row_i, col_i, eye)
            i_row = _col_to_row(ig, eye)
            q = q_ref[rows, hq].astype(F32) * DK ** -0.5
            k = k_ref[rows, hq].astype(F32)
            v = v_ref[rows, hv]
            cm = st_c.get(bb, hh)
            nv = st_n.get(bb, hh)
            m_prev = st_m.get(bb, hh)
            dlog = jnp.where(causal, b_col - b_row + i_row, NEG_INF)
            inter = b_col + m_prev
            mt = jnp.maximum(inter, jnp.max(dlog, axis=1, keepdims=True))
            s = _bdot_nt(q, k) * jnp.exp(dlog - mt)
            wi = jnp.exp(inter - mt)
            num = _bdot(s, v) + wi * _bdot(q, cm)
            den = jnp.sum(s, axis=1, keepdims=True) + wi * jnp.sum(q * nv, axis=1, keepdims=True)
            ht = num / jnp.maximum(jnp.abs(den), jnp.exp(-mt))
            m_new = mt[L - 1:L, :]
            b_last = b_col[L - 1:L, :]
            w_last = jnp.exp(b_last - b_col + ig - m_new)
            dec = jnp.exp(b_last + m_prev - m_new)
            kw = k * w_last
            st_c.put(bb, hh, dec * cm + _bdot_tn(kw, v))
            st_n.put(bb, hh, dec * nv + jnp.sum(kw, axis=0, keepdims=True))
            st_m.put(bb, hh, m_new)
            hcell = ht * jax.nn.sigmoid(og_ref[rows, hv].astype(F32))
            y_ref[rows, hv] = _head_norm_gate(hcell, gain_ref[:, hv], z_ref[rows, hv], True).astype(y_ref.dtype)
    for st in (st_c, st_n, st_m):
        st.finish(c)


def _mlstm(proj, gates, b, t, c0_all, n0_all, m0_all, j, bgate_all, gain_all, L, BB, HB, y_dtype):
    h, dk = 8, 256
    dv = gain_all.shape[-1] // h
    nc = t // L
    assert BB == 1 or nc == 1
    hg = h // HB
    rb = _row_block(nc)
    has_s0 = c0_all is not None
    v_off = 2 * h * dk
    wv = HB * dv
    in_specs = [pl.BlockSpec(memory_space=pltpu.SMEM),
                pl.BlockSpec((BB * L, HB * dk), rb(lambda hi: hi)),
                pl.BlockSpec((BB * L, HB * dk), rb(lambda hi: hg + hi)),
                pl.BlockSpec((BB * L, wv), rb(lambda hi: v_off // wv + hi)),
                pl.BlockSpec((BB * L, wv), rb(lambda hi: (v_off + h * dv) // wv + hi)),
                pl.BlockSpec((BB * L, wv), rb(lambda hi: (v_off + 2 * h * dv) // wv + hi)),
                pl.BlockSpec((BB * L, LANES), rb(lambda hi: 0)),
                pl.BlockSpec((None, 1, wv), lambda bi, hi, ci: (j, 0, hi))]
    args = [bgate_all[j], proj, proj, proj, proj, proj, gates, _rows3(gain_all)]
    c_spec = pl.BlockSpec((None, BB, HB, dk, dv), lambda bi, hi, ci: (j, bi, hi, 0, 0))
    n_spec = pl.BlockSpec((None, BB, HB, 1, dk), lambda bi, hi, ci: (j, bi, hi, 0, 0))
    m_spec = pl.BlockSpec((None, BB, HB, 1, 1), lambda bi, hi, ci: (j, bi, hi, 0, 0))
    nl = 1
    if has_s0:
        assert c0_all.shape[0] == nl
        in_specs += [c_spec, n_spec, m_spec]
        args += [c0_all, n0_all.reshape(nl, b, h, 1, dk), m0_all.reshape(nl, b, h, 1, 1)]
    scratch = [pltpu.VMEM((BB, HB, dk, dv), F32), pltpu.VMEM((BB, HB, 1, dk), F32),
               pltpu.VMEM((BB, HB, 1, 1), F32)] if nc > 1 else []
    y, co, no, mo = pl.pallas_call(
        functools.partial(_mlstm_body, L=L, BB=BB, HB=HB, NC=nc, DK=dk, DV=dv, H=h, has_s0=has_s0),
        grid=(b // BB, hg, nc),
        in_specs=in_specs,
        out_specs=[pl.BlockSpec((BB * L, wv), rb(lambda hi: hi)), c_spec, n_spec, m_spec],
        out_shape=[jax.ShapeDtypeStruct((b * t, h * dv), y_dtype),
                   jax.ShapeDtypeStruct((nl, b, h, dk, dv), F32),
                   jax.ShapeDtypeStruct((nl, b, h, 1, dk), F32),
                   jax.ShapeDtypeStruct((nl, b, h, 1, 1), F32)],
        scratch_shapes=scratch,
        compiler_params=_params(("arbitrary", "arbitrary", "arbitrary"), 48),
        name="mlstm",
    )(*args)
    return y, co, no.reshape(nl, b, h, dk), mo.reshape(nl, b, h)


def _merge_masks(row_i, col_i, L):
    masks = []
    s = 1
    while s < L:
        masks.append(((row_i // (2 * s)) == (col_i // (2 * s))) & ((row_i // s) > (col_i // s)))
        s *= 2
    return masks


def _unit_lower_inverse(a_strict, eye, masks, mm):
    x = jnp.where(eye, 1.0, 0.0) - jnp.where(masks[0], a_strict, 0.0)
    for mask in masks[1:]:
        e = jnp.where(mask, a_strict, 0.0)
        x = x - mm(x, mm(e, x))
    return x


def _vpu_mm(a, b):
    out = a[:, 0:1] * b[0:1, :]
    for kk in range(1, a.shape[1]):
        out = out + a[:, kk:kk + 1] * b[kk:kk + 1, :]
    return out


def _bmm(a, b):
    return jnp.einsum("gik,gkj->gij", a.astype(BF16), b.astype(BF16), preferred_element_type=F32)


def _bmm_nt(a, b):
    return jnp.einsum("gik,gjk->gij", a.astype(BF16), b.astype(BF16), preferred_element_type=F32)


def _bmm_tn(a, b):
    return jnp.einsum("gki,gkj->gij", a.astype(BF16), b.astype(BF16), preferred_element_type=F32)


def _gdn_dims(conv_w_all, a_log_all, gain_all):
    dk = dv = gain_all.shape[-1]
    hv = a_log_all.shape[-1]
    cdim = conv_w_all.shape[-1]
    hk = (cdim - hv * dv) // (2 * dk)
    rep = hv // hk
    assert 2 * hv <= 128 and rep * hk == hv
    return dk, dv, hv, hk, rep, cdim


def _gdn_step_body(alv_ref, dtv_ref, qx_ref, kx_ref, vx_ref, z_ref, gt_ref, cwq_ref, cwk_ref, cwv_ref,
                   gain_ref, cq0_ref, ck0_ref, cv0_ref, s0_ref, y_ref, so_ref,
                   eq_scr, ek_scr, ev_scr, q_st, k_st, v_st, z_st, qs_st, ks_st, kw_st, u_st, y_st,
                   *, L, BB, HK, REP, DK, DV):
    HV = HK * REP
    R = HV * L
    taps = CONV_W - 1

    def conv(x_ref, c0_ref, scr, cw_ref, bb):
        scr[CONV_PAD - taps:CONV_PAD, :] = c0_ref[bb]
        scr[CONV_PAD:CONV_PAD + L, :] = x_ref[bb * L:(bb + 1) * L, :].astype(F32)
        acc = scr[pl.ds(CONV_PAD - taps, L), :] * cw_ref[0:1, :]
        for w in range(1, CONV_W):
            acc = acc + scr[pl.ds(CONV_PAD - taps + w, L), :] * cw_ref[w:w + 1, :]
        return _silu(acc)

    row_i = _iota2(R, R, 0)
    col_i = _iota2(R, R, 1)
    same_head = (row_i // L) == (col_i // L)
    eye = row_i == col_i
    incl = same_head & (row_i >= col_i)
    strict = same_head & (row_i > col_i)
    masks = [same_head & m for m in _merge_masks(row_i % L, col_i % L, L)]
    lane = _iota2(R, LANES, 1)
    head_of_row = _iota2(R, LANES, 0) // L
    sel_beta = lane == head_of_row
    sel_decay = lane == head_of_row + HV
    tril = (_iota2(L, L, 0) >= _iota2(L, L, 1)).astype(F32)

    def stack_cols(x, sel):
        x8 = jnp.concatenate([x] * (8 // L), axis=0)
        tiled = jnp.concatenate([x8] * (R // 8), axis=0)
        return jnp.sum(jnp.where(sel, tiled, 0.0), axis=1, keepdims=True)

    for bb in range(BB):
        rows = slice(bb * L, (bb + 1) * L)
        cq = conv(qx_ref, cq0_ref, eq_scr, cwq_ref, bb)
        ck = conv(kx_ref, ck0_ref, ek_scr, cwk_ref, bb)
        cv = conv(vx_ref, cv0_ref, ev_scr, cwv_ref, bb)
        for vh in range(HV):
            st = slice(vh * L, (vh + 1) * L)
            kh = vh // REP
            q_st[st, :] = cq[:, kh * DK:(kh + 1) * DK]
            k_st[st, :] = ck[:, kh * DK:(kh + 1) * DK]
            v_st[st, :] = cv[:, vh * DV:(vh + 1) * DV]
            z_st[st, :] = z_ref[rows, vh * DV:(vh + 1) * DV].astype(F32)
        q = q_st[...]
        k = k_st[...]
        q = q * lax.rsqrt(jnp.sum(q * q, axis=-1, keepdims=True) + EPS) * DK ** -0.5
        k = k * lax.rsqrt(jnp.sum(k * k, axis=-1, keepdims=True) + EPS)
        q_st[...] = q
        k_st[...] = k

        gt = gt_ref[rows, :]
        beta = stack_cols(jax.nn.sigmoid(gt), sel_beta)
        g = -jnp.exp(alv_ref[...]) * jax.nn.softplus(gt + dtv_ref[...])
        g_cum = _vpu_mm(tril, g)
        g_col = stack_cols(g_cum, sel_decay)
        g_last = stack_cols(jnp.broadcast_to(g_cum[L - 1:L, :], (L, LANES)), sel_decay)
        g_row = _col_to_row(g_col, eye)
        decay = jnp.exp(jnp.where(incl, g_col - g_row, NEG_INF))
        kq = _bdot_nt(jnp.concatenate([k, q], axis=0), k)
        kk, qk = kq[:R], kq[R:]
        a = jnp.where(strict, beta * kk * decay, 0.0)
        x = _unit_lower_inverse(a, eye, masks, _bdot)

        for vh in range(HV):
            st = slice(vh * L, (vh + 1) * L)
            qk_rows = jnp.concatenate([q_st[st, :], k_st[st, :]], axis=0)
            both = _bdot(qk_rows, s0_ref[bb, vh])
            qs_st[st, :] = both[:L]
            ks_st[st, :] = both[L:]
        eg = jnp.exp(g_col)
        rhs = beta * v_st[...] - (beta * eg) * ks_st[...]
        u = _bdot(x, rhs)
        o = eg * qs_st[...] + _bdot(qk * decay, u)
        y_st[...] = _head_norm_gate(o, gain_ref[...], z_st[...], False)
        kw_st[...] = k * jnp.exp(g_last - g_col)
        u_st[...] = u
        eg_last = jnp.exp(g_last)
        for vh in range(HV):
            st = slice(vh * L, (vh + 1) * L)
            y_ref[rows, vh * DV:(vh + 1) * DV] = y_st[st, :].astype(y_ref.dtype)
            so_ref[bb, vh] = (eg_last[vh * L:vh * L + 1, :] * s0_ref[bb, vh]
                              + _bdot_tn(kw_st[st, :], u_st[st, :]))


def _gdn_step(proj, gates, b, t, s0_all, conv0_all, j, conv_w_all, a_log_all, dt_bias_all, gain_all,
              BB, y_dtype):
    dk, dv, hv, hk, rep, cdim = _gdn_dims(conv_w_all, a_log_all, gain_all)
    L = t
    R = hv * L
    assert 8 % L == 0 and R % 8 == 0 and dk == dv
    hg = 1
    wq, wv = hk * dk, hv * dv
    k_blk = 1
    v_blk = (2 * hk * dk) // wv
    z_blk = cdim // wv
    rb = _row_block(1)
    lanes_of = lambda vec: jnp.zeros((1, LANES), F32).at[0, hv:2 * hv].set(vec.astype(F32))

    def cspec(width, off):
        return pl.BlockSpec((None, CONV_W, width), lambda bi, hi, ci: (j, 0, off + hi))

    def c0spec(width, off):
        return pl.BlockSpec((None, BB, CONV_W - 1, width), lambda bi, hi, ci: (j, bi, 0, off + hi))

    st_spec = pl.BlockSpec((None, BB, hv, dk, dv), lambda bi, hi, ci: (j, bi, hi, 0, 0))
    in_specs = [pl.BlockSpec((1, LANES), lambda bi, hi, ci: (0, 0)),
                pl.BlockSpec((1, LANES), lambda bi, hi, ci: (0, 0)),
                pl.BlockSpec((BB * L, wq), rb(lambda hi: hi)),
                pl.BlockSpec((BB * L, wq), rb(lambda hi: k_blk + hi)),
                pl.BlockSpec((BB * L, wv), rb(lambda hi: v_blk + hi)),
                pl.BlockSpec((BB * L, wv), rb(lambda hi: z_blk + hi)),
                pl.BlockSpec((BB * L, LANES), rb(lambda hi: 0)),
                cspec(wq, 0), cspec(wq, k_blk), cspec(wv, v_blk),
                pl.BlockSpec((None, 1, dv), lambda bi, hi, ci: (j, 0, 0)),
                c0spec(wq, 0), c0spec(wq, k_blk), c0spec(wv, v_blk), st_spec]
    args = [lanes_of(a_log_all[j]), lanes_of(dt_bias_all[j]), proj, proj, proj, proj, gates,
            conv_w_all, conv_w_all, conv_w_all, _rows3(gain_all),
            conv0_all, conv0_all, conv0_all, s0_all]
    stacked = [pltpu.VMEM((R, dk), F32)] * 9
    y, so = pl.pallas_call(
        functools.partial(_gdn_step_body, L=L, BB=BB, HK=hk, REP=rep, DK=dk, DV=dv),
        grid=(b // BB, hg, 1),
        in_specs=in_specs,
        out_specs=[pl.BlockSpec((BB * L, wv), rb(lambda hi: hi)), st_spec],
        out_shape=[jax.ShapeDtypeStruct((b * t, hv * dv), y_dtype),
                   jax.ShapeDtypeStruct((1, b, hv, dk, dv), F32)],
        scratch_shapes=[pltpu.VMEM((L + CONV_PAD, wq), F32), pltpu.VMEM((L + CONV_PAD, wq), F32),
                        pltpu.VMEM((L + CONV_PAD, wv), F32)] + stacked,
        compiler_params=_params(("arbitrary", "arbitrary", "arbitrary"), 48),
        name="gdn_step",
    )(*args)
    return y, so


def _gdn_seq_body(al_ref, dt_ref, qx_ref, kx_ref, vx_ref, z_ref, gt_ref, cwq_ref, cwk_ref, cwv_ref,
                  gain_ref, y_ref, so_ref, s_scr, eq_scr, ek_scr, ev_scr,
                  *, TB, C, HBK, NTB, DK, DV, REP, HV):
    hb = pl.program_id(1)
    tb = pl.program_id(2)
    G = TB // C
    taps = CONV_W - 1

    @pl.when(tb == 0)
    def _():
        for scr in (eq_scr, ek_scr, ev_scr):
            scr[0:CONV_PAD, :] = jnp.zeros((CONV_PAD, scr.shape[1]), F32)
        s_scr[...] = jnp.zeros_like(s_scr)

    def conv(x_ref, scr, cw_ref):
        scr[CONV_PAD:CONV_PAD + TB, :] = x_ref[...].astype(F32)
        acc = scr[pl.ds(CONV_PAD - taps, TB), :] * cw_ref[0:1, :]
        for w in range(1, CONV_W):
            acc = acc + scr[pl.ds(CONV_PAD - taps + w, TB), :] * cw_ref[w:w + 1, :]
        if NTB > 1:
            scr[0:CONV_PAD, :] = scr[TB:TB + CONV_PAD, :]
        return _silu(acc)

    cq = conv(qx_ref, eq_scr, cwq_ref)
    ck = conv(kx_ref, ek_scr, cwk_ref)
    cv = conv(vx_ref, ev_scr, cwv_ref)

    row_i = _iota2(C, C, 0)
    col_i = _iota2(C, C, 1)
    eye = row_i == col_i
    incl = row_i >= col_i
    PW = REP * C
    prow = _iota2(C, PW, 0)
    plane = _iota2(C, PW, 1)
    pcol = plane % C
    phead = plane // C
    p_eye = prow == pcol
    p_incl = prow >= pcol
    p_strict = prow > pcol
    p_masks = _merge_masks(prow, pcol, C)
    bd_mask = (_iota2(PW, PW, 0) // C) == (_iota2(PW, PW, 1) // C)

    def block_diag(xp):
        return jnp.where(bd_mask, jnp.concatenate([xp] * REP, axis=1), 0.0)

    def packed_mm(ap, bp):
        return _bmm(ap, block_diag(bp))

    def pack_cols(cols):
        out = cols[0]
        for r in range(1, REP):
            out = jnp.where(phead >= r, cols[r], out)
        return out

    lane = _iota2(TB, 128, 1)
    gt = gt_ref[...]
    for kh in range(HBK):
        q = cq[:, kh * DK:(kh + 1) * DK]
        k = ck[:, kh * DK:(kh + 1) * DK]
        q = q * lax.rsqrt(jnp.sum(q * q, axis=-1, keepdims=True) + EPS) * DK ** -0.5
        k = k * lax.rsqrt(jnp.sum(k * k, axis=-1, keepdims=True) + EPS)
        q3 = q.reshape(G, C, DK)
        k3 = k.reshape(G, C, DK)
        kq = _bmm_nt(jnp.concatenate([k3, q3], axis=1), k3)
        kk, qk = kq[:, :C, :], kq[:, C:, :]
        betas, g_cols, g_rows = [], [], []
        for r in range(REP):
            head = (hb * HBK + kh) * REP + r
            betas.append(jax.nn.sigmoid(_pick_lane(gt, lane, head)).reshape(G, C, 1))
            a_neg = -jnp.exp(jnp.full((1, 1), al_ref[head], F32))
            g = (a_neg * jax.nn.softplus(_pick_lane(gt, lane, HV + head) + dt_ref[head])).reshape(G, C, 1)
            g_lanes = jnp.sum(jnp.where(eye, g, 0.0), axis=1, keepdims=True)
            g_cols.append(jnp.sum(jnp.where(incl, g_lanes, 0.0), axis=2, keepdims=True))
            g_rows.append(jnp.sum(jnp.where(row_i <= col_i, g, 0.0), axis=1, keepdims=True))
        decay_p = jnp.exp(jnp.where(p_incl, pack_cols(g_cols) - jnp.concatenate(g_rows, axis=-1), NEG_INF))
        a_p = jnp.where(p_strict, pack_cols(betas) * jnp.concatenate([kk] * REP, axis=-1) * decay_p, 0.0)
        x_p = _unit_lower_inverse(a_p, p_eye, p_masks, packed_mm)
        for r in range(REP):
            vh = kh * REP + r
            hv = slice(vh * DV, (vh + 1) * DV)
            beta, g_col = betas[r], g_cols[r]
            x = x_p[:, :, r * C:(r + 1) * C]
            decay = decay_p[:, :, r * C:(r + 1) * C]
            v3 = cv[:, hv].reshape(G, C, DV)
            eg = jnp.exp(g_col)
            wu = _bmm(x, jnp.concatenate([(beta * eg) * k3, beta * v3], axis=-1))
            qo = _bmm(qk * decay, wu)
            o0 = qo[:, :, DK:]
            g_last = g_col[:, C - 1:C, :]
            mb = _bmm_tn(k3 * jnp.exp(g_last - g_col), wu)
            b_eff = mb[:, :, DK:]
            lhs = jnp.concatenate([eg * q3 - qo[:, :, :DK], mb[:, :, :DK]], axis=1).astype(BF16)
            eg_last = jnp.exp(g_last)
            s = s_scr[vh]
            outs = []
            for c in range(G):
                both = jnp.dot(lhs[c], s.astype(BF16), preferred_element_type=F32)
                outs.append(both[:C] + o0[c])
                s = eg_last[c] * s - both[C:] + b_eff[c]
            s_scr[vh] = s
            o = jnp.concatenate(outs, axis=0) if G > 1 else outs[0]
            y_ref[:, hv] = _head_norm_gate(o, gain_ref[...], z_ref[:, hv], False).astype(y_ref.dtype)

    @pl.when(tb == NTB - 1)
    def _():
        so_ref[...] = s_scr[...]


def _gdn_seq(proj, gates, b, t, j, conv_w_all, a_log_all, dt_bias_all, gain_all, TB, C, HBK, y_dtype):
    dk, dv, hv, hk, rep, cdim = _gdn_dims(conv_w_all, a_log_all, gain_all)
    assert dk == dv
    ntb = t // TB
    hg = hk // HBK
    wq, wv = HBK * dk, HBK * rep * dv
    k_blk = (hk * dk) // wq
    v_blk = (2 * hk * dk) // wv
    z_blk = cdim // wv
    rb = _row_block(ntb)

    def cspec(width, off):
        return pl.BlockSpec((None, CONV_W, width), lambda bi, hi, ti: (j, 0, off + hi))

    in_specs = [pl.BlockSpec(memory_space=pltpu.SMEM),
                pl.BlockSpec(memory_space=pltpu.SMEM),
                pl.BlockSpec((TB, wq), rb(lambda hi: hi)),
                pl.BlockSpec((TB, wq), rb(lambda hi: k_blk + hi)),
                pl.BlockSpec((TB, wv), rb(lambda hi: v_blk + hi)),
                pl.BlockSpec((TB, wv), rb(lambda hi: z_blk + hi)),
                pl.BlockSpec((TB, LANES), rb(lambda hi: 0)),
                cspec(wq, 0), cspec(wq, k_blk), cspec(wv, v_blk),
                pl.BlockSpec((None, 1, dv), lambda bi, hi, ti: (j, 0, 0))]
    args = [a_log_all[j], dt_bias_all[j], proj, proj, proj, proj, gates,
            conv_w_all, conv_w_all, conv_w_all, _rows3(gain_all)]
    st_spec = pl.BlockSpec((None, None, HBK * rep, dk, dv), lambda bi, hi, ti: (0, bi, hi, 0, 0))
    return pl.pallas_call(
        functools.partial(_gdn_seq_body, TB=TB, C=C, HBK=HBK, NTB=ntb, DK=dk, DV=dv, REP=rep, HV=hv),
        grid=(b, hg, ntb),
        in_specs=in_specs,
        out_specs=[pl.BlockSpec((TB, wv), rb(lambda hi: hi)), st_spec],
        out_shape=[jax.ShapeDtypeStruct((b * t, hv * dv), y_dtype),
                   jax.ShapeDtypeStruct((1, b, hv, dk, dv), F32)],
        scratch_shapes=[pltpu.VMEM((HBK * rep, dk, dv), F32),
                        pltpu.VMEM((TB + CONV_PAD, wq), F32), pltpu.VMEM((TB + CONV_PAD, wq), F32),
                        pltpu.VMEM((TB + CONV_PAD, wv), F32)],
        compiler_params=_params(("arbitrary", "arbitrary", "arbitrary"), 48),
        name="gdn_seq",
    )(*args)


def _rope_tables(pos, dk):
    half = dk // 2
    inv = ROPE_BASE ** (-jnp.arange(half, dtype=F32) / half)
    ang = pos.astype(F32)[:, None] * inv[None, :]
    return jnp.cos(ang), jnp.sin(ang)


def _trunk(x, p, states, pos, w, cfg):
    (norm_pre, norm_post, ple_proj, ple_gate, ret_w_in, ret_head_norm, ret_w_out,
     mlstm_w_in, mlstm_b_gate, mlstm_head_norm, mlstm_w_out,
     gdn_w_in, gdn_conv_w, gdn_a_log, gdn_dt_bias, gdn_head_norm, gdn_w_out) = w
    ret_s, ml_c, ml_n, ml_m, gdn_s, gdn_conv = states
    b, t, d = x.shape
    depth = norm_pre.shape[0]
    n_ret = ret_w_in.shape[0]
    m = b * t
    tm, ydt, pdt = cfg["tm"], cfg["y_dtype"], cfg["proj_dtype"]
    cos, sin = _rope_tables(pos, 256)
    r = x.reshape(m, d)
    p2 = p.reshape(depth, m, p.shape[-1])
    ret_out = None
    outs = {}
    keep = CONV_W - 1
    for i in range(depth):
        kind, j = i % 3, i // 3
        if kind == 0:
            proj, _, _ = _inproj(r, norm_pre, i, ret_w_in, j, tm, 512, pdt)
            y, ret_out = _retention(proj, b, t, ret_s, j, cos, sin, ret_head_norm, ret_out, n_ret,
                                    cfg["ret_L"], cfg["ret_BB"], cfg["ret_HB"], ydt)
            w_out = ret_w_out
        elif kind == 1:
            proj, gates, _ = _inproj(r, norm_pre, i, mlstm_w_in, j, tm, 512, pdt)
            y, outs["c"], outs["n"], outs["m"] = _mlstm(
                proj, gates, b, t, ml_c, ml_n, ml_m, j, mlstm_b_gate, mlstm_head_norm,
                cfg["ml_L"], cfg["ml_BB"], cfg["ml_HB"], ydt)
            w_out = mlstm_w_out
        else:
            cdim = gdn_conv_w.shape[-1]
            from_tail = pdt != F32
            assert t >= keep and (not from_tail or (t % tm == 0 and keep <= TAIL_ROWS))
            proj, gates, tail = _inproj(r, norm_pre, i, gdn_w_in, j, tm, 512, pdt, want_tail=from_tail)
            if gdn_s is None:
                y, outs["gs"] = _gdn_seq(proj, gates, b, t, j, gdn_conv_w, gdn_a_log, gdn_dt_bias,
                                         gdn_head_norm, cfg["gdn_TB"], cfg["gdn_L"], cfg["gdn_HBK"], ydt)
            else:
                y, outs["gs"] = _gdn_step(proj, gates, b, t, gdn_s, gdn_conv, j, gdn_conv_w, gdn_a_log,
                                          gdn_dt_bias, gdn_head_norm, cfg["gdn_BB"], ydt)
            if from_tail:
                last_tiles = tail.reshape(b, t // tm, TAIL_ROWS, -1)[:, -1]
                outs["gc"] = last_tiles[:, TAIL_ROWS - keep:, :cdim][None]
            else:
                outs["gc"] = proj.reshape(b, t, -1)[:, t - keep:, :cdim][None]
            w_out = gdn_w_out
        mix = _outproj(y, w_out, j, tm, cfg["tn_out"])
        r = _post_ple(mix, r, norm_post, p2, ple_proj, ple_gate, i, cfg["tm_ple"], 512)
    return (r.reshape(b, t, d), ret_out, outs["c"], outs["n"], outs["m"], outs["gs"], outs["gc"])


_PROMPT_CFG = dict(tm=2048, tn_out=512, tm_ple=1024, y_dtype=BF16, proj_dtype=BF16,
                   ret_L=256, ret_BB=1, ret_HB=2, ml_L=256, ml_BB=1, ml_HB=2,
                   gdn_TB=512, gdn_L=64, gdn_HBK=1)
_SAMPLE_CFG = dict(tm=512, tn_out=512, tm_ple=512, y_dtype=F32, proj_dtype=F32,
                   ret_L=4, ret_BB=2, ret_HB=8, ml_L=4, ml_BB=2, ml_HB=8, gdn_BB=2)


def kernel(x_prompt, x_sample, state_ret_S, state_mlstm_C, state_mlstm_n, state_mlstm_m, state_gdn_S, state_gdn_conv, p_prompt, p_sample, norm_pre, norm_post, ple_proj, ple_gate, ret_w_in, ret_head_norm, ret_w_out, mlstm_w_in, mlstm_b_gate, mlstm_head_norm, mlstm_w_out, gdn_w_in, gdn_conv_w, gdn_a_log, gdn_dt_bias, gdn_head_norm, gdn_w_out):
    w = (norm_pre, norm_post, ple_proj, ple_gate, ret_w_in, ret_head_norm, ret_w_out,
         mlstm_w_in, mlstm_b_gate, mlstm_head_norm, mlstm_w_out,
         gdn_w_in, gdn_conv_w, gdn_a_log, gdn_dt_bias, gdn_head_norm, gdn_w_out)
    pos_p = jnp.arange(x_prompt.shape[1])
    yp, ret_p, mc_p, mn_p, mm_p, gs_p, gc_p = _trunk(
        x_prompt, p_prompt, (None,) * 6, pos_p, w, _PROMPT_CFG)
    pos_s = PAST_LEN + jnp.arange(x_sample.shape[1])
    ys, ret_s, mc_s, mn_s, mm_s, gs_s, gc_s = _trunk(
        x_sample, p_sample,
        (state_ret_S, state_mlstm_C, state_mlstm_n, state_mlstm_m, state_gdn_S, state_gdn_conv),
        pos_s, w, _SAMPLE_CFG)
    return (yp, ys, ret_p, mc_p, mn_p, mm_p, gs_p, gc_p, ret_s, mc_s, mn_s, mm_s, gs_s, gc_s)
```

```python
import functools

import jax
import jax.numpy as jnp
from jax import lax
from jax.experimental import pallas as pl
from jax.experimental.pallas import tpu as pltpu

F32 = jnp.float32
BF16 = jnp.bfloat16
EPS = 1e-6
ROPE_BASE = 10000.0
CONV_W = 4
PAST_LEN = 16384
MIB = 1024 * 1024
NEG_INF = float("-inf")
CONV_PAD = 8
LANES = 128
TAIL_ROWS = 8


def _params(sem, vmem_mib):
    return pltpu.CompilerParams(dimension_semantics=sem, vmem_limit_bytes=vmem_mib * MIB)


def _bdot(a, b):
    return jnp.dot(a.astype(BF16), b.astype(BF16), preferred_element_type=F32)


def _bdot_nt(a, b):
    return lax.dot_general(a.astype(BF16), b.astype(BF16), (((1,), (1,)), ((), ())),
                           preferred_element_type=F32)


def _bdot_tn(a, b):
    return lax.dot_general(a.astype(BF16), b.astype(BF16), (((0,), (0,)), ((), ())),
                           preferred_element_type=F32)


def _sigmoid(x):
    return 0.5 * jnp.tanh(0.5 * x) + 0.5


def _silu(x):
    h = 0.5 * x
    return h * jnp.tanh(h) + h


def _rows3(table):
    return table.reshape(table.shape[0], 1, table.shape[1])


SLAB = 256


def _row_slabs(tm):
    step = min(SLAB, tm)
    return [slice(s, s + step) for s in range(0, tm, step)]


def _once_per_row_tile(block_shape, index_map):
    return pl.BlockSpec(block_shape, index_map, pipeline_mode=pl.Buffered(1))


def _inproj_body(x_ref, g_ref, w_ref, *rest, w_is_nk, n_main, gate_cols, has_tail):
    has_gates = gate_cols > 0
    outs = list(rest[:1 + has_gates + has_tail])
    h_ref = rest[-1]
    o_ref = outs.pop(0)
    gates_ref = outs.pop(0) if has_gates else None
    tail_ref = outs.pop(0) if has_tail else None
    c = pl.program_id(1)

    @pl.when(c == 0)
    def _():
        for rows in _row_slabs(x_ref.shape[0]):
            x = x_ref[rows, :]
            ms = jnp.mean(x * x, axis=-1, keepdims=True)
            h_ref[rows, :] = (x * lax.rsqrt(ms + EPS) * g_ref[...]).astype(BF16)

    w = w_ref[...].astype(BF16)
    contract = (((1,), (1,)), ((), ())) if w_is_nk else (((1,), (0,)), ((), ()))
    acc = lax.dot_general(h_ref[...], w, contract, preferred_element_type=F32)

    def write_main():
        o_ref[...] = acc.astype(o_ref.dtype)
        if has_tail:
            tail_ref[...] = acc[acc.shape[0] - TAIL_ROWS:, :]

    if has_gates:
        pl.when(c < n_main)(write_main)

        @pl.when(c == n_main)
        def _():
            lane = _iota2(acc.shape[0], LANES, 1)
            gates_ref[...] = jnp.where(lane < gate_cols, acc[:, :LANES], 0.0)
    else:
        write_main()


def _inproj(x2d, g_all, layer, w_all, j, tm, tn, out_dtype, want_tail=False):
    m, d = x2d.shape
    n = w_all.shape[-1]
    n_main = n // tn
    has_gates = n % tn != 0
    assert n - n_main * tn <= LANES
    w_is_nk = n % LANES != 0
    if w_is_nk:
        w_all = jnp.swapaxes(w_all, 1, 2)
        w_spec = pl.BlockSpec((None, tn, d), lambda i, c: (j, c, 0))
    else:
        w_spec = pl.BlockSpec((None, d, tn), lambda i, c: (j, 0, c))

    def main_col(c):
        return jnp.minimum(c, n_main - 1)

    out_specs = [pl.BlockSpec((tm, tn), lambda i, c: (i, main_col(c)))]
    out_shape = [jax.ShapeDtypeStruct((m, n_main * tn), out_dtype)]
    if has_gates:
        out_specs.append(pl.BlockSpec((tm, LANES), lambda i, c: (i, 0)))
        out_shape.append(jax.ShapeDtypeStruct((m, LANES), F32))
    if want_tail:
        out_specs.append(pl.BlockSpec((None, TAIL_ROWS, tn), lambda i, c: (i, 0, main_col(c))))
        out_shape.append(jax.ShapeDtypeStruct((m // tm, TAIL_ROWS, n_main * tn), F32))
    res = pl.pallas_call(
        functools.partial(_inproj_body, w_is_nk=w_is_nk, n_main=n_main, gate_cols=n - n_main * tn,
                          has_tail=want_tail),
        grid=(m // tm, n_main + has_gates),
        in_specs=[_once_per_row_tile((tm, d), lambda i, c: (i, 0)),
                  pl.BlockSpec((None, 1, d), lambda i, c: (layer, 0, 0)),
                  w_spec],
        out_specs=out_specs,
        out_shape=out_shape,
        scratch_shapes=[pltpu.VMEM((tm, d), BF16)],
        compiler_params=_params(("arbitrary", "arbitrary"), 56),
        name="inproj",
    )(x2d, _rows3(g_all), w_all)
    res = list(res)
    proj = res.pop(0)
    gates = res.pop(0) if has_gates else None
    tail = res.pop(0) if want_tail else None
    return proj, gates, tail


def _outproj_body(y_ref, w_ref, o_ref):
    o_ref[...] = jnp.dot(y_ref[...].astype(BF16), w_ref[...].astype(BF16), preferred_element_type=F32)


def _outproj(y2d, w_all, j, tm, tn):
    m, k = y2d.shape
    n = w_all.shape[-1]
    return pl.pallas_call(
        _outproj_body,
        grid=(m // tm, n // tn),
        in_specs=[_once_per_row_tile((tm, k), lambda i, c: (i, 0)),
                  pl.BlockSpec((None, k, tn), lambda i, c: (j, 0, c))],
        out_specs=pl.BlockSpec((tm, tn), lambda i, c: (i, c)),
        out_shape=jax.ShapeDtypeStruct((m, n), F32),
        compiler_params=_params(("arbitrary", "arbitrary"), 56),
        name="outproj",
    )(y2d, w_all)


def _post_ple_body(mix_ref, r_ref, g_ref, p_ref, proj_ref, gate_ref, o_ref, rs_ref, r1b_ref, *, tn):
    c = pl.program_id(1)

    @pl.when(c == 0)
    def _():
        for rows in _row_slabs(mix_ref.shape[0]):
            mix = mix_ref[rows, :]
            rs = lax.rsqrt(jnp.mean(mix * mix, axis=-1, keepdims=True) + EPS)
            rs_ref[rows, :] = rs
            r1b_ref[rows, :] = (r_ref[rows, :] + mix * rs * g_ref[...]).astype(BF16)

    gate = jnp.dot(r1b_ref[...], gate_ref[...].astype(BF16), preferred_element_type=F32)
    emb = _bdot(p_ref[...], proj_ref[...])
    cols = pl.ds(pl.multiple_of(c * tn, tn), tn)
    r1 = r_ref[:, cols] + mix_ref[:, cols] * rs_ref[...] * g_ref[:, cols]
    o_ref[...] = r1 + emb * _sigmoid(gate)


def _post_ple(mix, r, g_all, p_all, proj_all, gate_all, layer, tm, tn):
    m, d = r.shape
    pd = p_all.shape[-1]
    return pl.pallas_call(
        functools.partial(_post_ple_body, tn=tn),
        grid=(m // tm, d // tn),
        in_specs=[pl.BlockSpec((tm, d), lambda i, c: (i, 0)),
                  pl.BlockSpec((tm, d), lambda i, c: (i, 0)),
                  pl.BlockSpec((None, 1, d), lambda i, c: (layer, 0, 0)),
                  pl.BlockSpec((None, tm, pd), lambda i, c: (layer, i, 0)),
                  pl.BlockSpec((None, pd, tn), lambda i, c: (layer, 0, c)),
                  pl.BlockSpec((None, d, tn), lambda i, c: (layer, 0, c))],
        out_specs=pl.BlockSpec((tm, tn), lambda i, c: (i, c)),
        out_shape=jax.ShapeDtypeStruct((m, d), F32),
        scratch_shapes=[pltpu.VMEM((tm, 1), F32), pltpu.VMEM((tm, d), BF16)],
        compiler_params=_params(("arbitrary", "arbitrary"), 56),
        name="post_ple",
    )(mix, r, _rows3(g_all), p_all, proj_all, gate_all)


def _iota2(n, m, dim):
    return lax.broadcasted_iota(jnp.int32, (n, m), dim)


def _col_to_row(col, eye):
    return jnp.sum(jnp.where(eye, col, 0.0), axis=0, keepdims=True)


def _cumsum_col_row(col, row_i, col_i, eye):
    row = _col_to_row(col, eye)
    c_col = jnp.sum(jnp.where(col_i <= row_i, row, 0.0), axis=1, keepdims=True)
    c_row = jnp.sum(jnp.where(row_i <= col_i, col, 0.0), axis=0, keepdims=True)
    return c_col, c_row


def _pick_lane(blk, lane_iota, idx):
    return jnp.sum(jnp.where(lane_iota == idx, blk, 0.0), axis=1, keepdims=True)


def _head_norm_gate(o, gain, z, center):
    if center:
        o = o - jnp.mean(o, axis=-1, keepdims=True)
    y = o * lax.rsqrt(jnp.mean(o * o, axis=-1, keepdims=True) + EPS) * gain
    return y * _silu(z.astype(F32))


def _row_block(nc):
    return lambda col: (lambda bi, hi, ci: (bi * nc + ci, col(hi)))


class _State:
    def __init__(self, in_ref, out_ref, scr, nc):
        self.in_ref, self.out_ref, self.scr, self.nc = in_ref, out_ref, scr, nc

    def start(self, chunk):
        if self.nc > 1:
            @pl.when(chunk == 0)
            def _():
                if self.in_ref is None:
                    self.scr[...] = jnp.zeros_like(self.scr)
                else:
                    self.scr[...] = self.in_ref[...]

    def get(self, bb, hh):
        if self.nc > 1:
            return self.scr[bb, hh]
        if self.in_ref is None:
            return jnp.zeros(self.out_ref.shape[2:], F32)
        return self.in_ref[bb, hh]

    def put(self, bb, hh, val):
        if self.nc > 1:
            self.scr[bb, hh] = val
        else:
            self.out_ref[bb, hh] = val

    def finish(self, chunk):
        if self.nc > 1:
            @pl.when(chunk == self.nc - 1)
            def _():
                self.out_ref[...] = self.scr[...]


def _ret_body(*refs, L, BB, HB, NC, DK, DV, has_s0, has_prev):
    lg_ref, q_ref, k_ref, v_ref, z_ref, cos_ref, sin_ref, gain_ref = refs[:8]
    s0_ref = refs[8] if has_s0 else None
    rest = refs[8 + has_s0 + has_prev:]
    y_ref, so_ref = rest[:2]
    st = _State(s0_ref, so_ref, rest[2] if NC > 1 else None, NC)
    hb = pl.program_id(1)
    c = pl.program_id(2)
    st.start(c)

    cos = cos_ref[...]
    sin = sin_ref[...]
    half = DK // 2

    def rot(x):
        x1, x2 = x[:, :half], x[:, half:]
        return jnp.concatenate([x1 * cos - x2 * sin, x2 * cos + x1 * sin], axis=-1)

    row_i = _iota2(L, L, 0)
    col_i = _iota2(L, L, 1)
    rel = (row_i - col_i).astype(F32)
    idx = _iota2(L, 1, 0).astype(F32)
    for hh in range(HB):
        lg = lg_ref[hb * HB + hh]
        decay = jnp.exp(jnp.where(rel >= 0, lg * rel, NEG_INF))
        w_in = jnp.exp(lg * (idx + 1.0))
        w_out = jnp.exp(lg * (L - 1.0 - idx))
        w_all = jnp.exp(jnp.full((1, 1), L, F32) * lg)
        hq = slice(hh * DK, (hh + 1) * DK)
        hv = slice(hh * DV, (hh + 1) * DV)
        for bb in range(BB):
            rows = slice(bb * L, (bb + 1) * L)
            q = rot(q_ref[rows, hq].astype(F32))
            k = rot(k_ref[rows, hq].astype(F32)) * DK ** -0.5
            v = v_ref[rows, hv]
            s = st.get(bb, hh)
            scores = _bdot_nt(q, k) * decay
            o = _bdot(scores, v) + _bdot(q * w_in, s)
            st.put(bb, hh, w_all * s + _bdot_tn(k * w_out, v))
            y_ref[rows, hv] = _head_norm_gate(o, gain_ref[:, hv], z_ref[rows, hv], True).astype(y_ref.dtype)
    st.finish(c)


def _retention(proj, b, t, s0_all, j, cos, sin, gain_all, so_prev, n_layers, L, BB, HB, y_dtype):
    h, dk = 8, 256
    dv = gain_all.shape[-1] // h
    nc = t // L
    assert BB == 1 or nc == 1
    hg = h // HB
    rb = _row_block(nc)
    lg = jnp.log1p(-jnp.exp2(-5.0 - jnp.arange(h, dtype=F32)))
    has_s0 = s0_all is not None
    v_blk = (2 * h * dk) // (HB * dv)
    z_blk = (2 * h * dk + h * dv) // (HB * dv)
    in_specs = [pl.BlockSpec(memory_space=pltpu.SMEM),
                pl.BlockSpec((BB * L, HB * dk), rb(lambda hi: hi)),
                pl.BlockSpec((BB * L, HB * dk), rb(lambda hi: hg + hi)),
                pl.BlockSpec((BB * L, HB * dv), rb(lambda hi: v_blk + hi)),
                pl.BlockSpec((BB * L, HB * dv), rb(lambda hi: z_blk + hi)),
                pl.BlockSpec((L, dk // 2), lambda bi, hi, ci: (ci, 0)),
                pl.BlockSpec((L, dk // 2), lambda bi, hi, ci: (ci, 0)),
                pl.BlockSpec((None, 1, HB * dv), lambda bi, hi, ci: (j, 0, hi))]
    args = [lg, proj, proj, proj, proj, cos, sin, _rows3(gain_all)]
    st_spec = pl.BlockSpec((None, BB, HB, dk, dv), lambda bi, hi, ci: (j, bi, hi, 0, 0))
    if has_s0:
        in_specs.append(st_spec)
        args.append(s0_all)
    aliases = {}
    if so_prev is not None:
        in_specs.append(pl.BlockSpec(memory_space=pl.ANY))
        args.append(so_prev)
        aliases = {len(args) - 1: 1}
    return pl.pallas_call(
        functools.partial(_ret_body, L=L, BB=BB, HB=HB, NC=nc, DK=dk, DV=dv, has_s0=has_s0,
                          has_prev=so_prev is not None),
        grid=(b // BB, hg, nc),
        in_specs=in_specs,
        out_specs=[pl.BlockSpec((BB * L, HB * dv), rb(lambda hi: hi)), st_spec],
        out_shape=[jax.ShapeDtypeStruct((b * t, h * dv), y_dtype),
                   jax.ShapeDtypeStruct((n_layers, b, h, dk, dv), F32)],
        scratch_shapes=[pltpu.VMEM((BB, HB, dk, dv), F32)] if nc > 1 else [],
        input_output_aliases=aliases,
        compiler_params=_params(("arbitrary", "arbitrary", "arbitrary"), 48),
        name="retention",
    )(*args)


def _mlstm_body(*refs, L, BB, HB, NC, DK, DV, H, has_s0):
    bg_ref, q_ref, k_ref, v_ref, og_ref, z_ref, gt_ref, gain_ref = refs[:8]
    c0_ref, n0_ref, m0_ref = refs[8:11] if has_s0 else (None, None, None)
    rest = refs[8 + 3 * has_s0:]
    y_ref, co_ref, no_ref, mo_ref = rest[:4]
    scr = rest[4:] if NC > 1 else (None, None, None)
    st_c = _State(c0_ref, co_ref, scr[0], NC)
    st_n = _State(n0_ref, no_ref, scr[1], NC)
    st_m = _State(m0_ref, mo_ref, scr[2], NC)
    hb = pl.program_id(1)
    c = pl.program_id(2)
    for st in (st_c, st_n, st_m):
        st.start(c)

    row_i = _iota2(L, L, 0)
    col_i = _iota2(L, L, 1)
    eye = row_i == col_i
    causal = row_i >= col_i
    lane = _iota2(L, 128, 1)
    for bb in range(BB):
        rows = slice(bb * L, (bb + 1) * L)
        gt = gt_ref[rows, :]
        for hh in range(HB):
            head = hb * HB + hh
            hq = slice(hh * DK, (hh + 1) * DK)
            hv = slice(hh * DV, (hh + 1) * DV)
            ig = _pick_lane(gt, lane, head) + bg_ref[head]
            fg = _pick_lane(gt, lane, H + head) + bg_ref[H + head]
            lf = jax.nn.log_sigmoid(fg)
            b_col, b_row = _cumsum_col_row(lf, row_i, col_i, eye)
            i_row = _col_to_row(ig, eye)
            q = q_ref[rows, hq].astype(F32) * DK ** -0.5
            k = k_ref[rows, hq].astype(F32)
            v = v_ref[rows, hv]
            cm = st_c.get(bb, hh)
            nv = st_n.get(bb, hh)
            m_prev = st_m.get(bb, hh)
            dlog = jnp.where(causal, b_col - b_row + i_row, NEG_INF)
            inter = b_col + m_prev
            mt = jnp.maximum(inter, jnp.max(dlog, axis=1, keepdims=True))
            s = _bdot_nt(q, k) * jnp.exp(dlog - mt)
            wi = jnp.exp(inter - mt)
            num = _bdot(s, v) + wi * _bdot(q, cm)
            den = jnp.sum(s, axis=1, keepdims=True) + wi * jnp.sum(q * nv, axis=1, keepdims=True)
            ht = num / jnp.maximum(jnp.abs(den), jnp.exp(-mt))
            m_new = mt[L - 1:L, :]
            b_last = b_col[L - 1:L, :]
            w_last = jnp.exp(b_last - b_col + ig - m_new)
            dec = jnp.exp(b_last + m_prev - m_new)
            kw = k * w_last
            st_c.put(bb, hh, dec * cm + _bdot_tn(kw, v))
            st_n.put(bb, hh, dec * nv + jnp.sum(kw, axis=0, keepdims=True))
            st_m.put(bb, hh, m_new)
            hcell = ht * _sigmoid(og_ref[rows, hv].astype(F32))
            y_ref[rows, hv] = _head_norm_gate(hcell, gain_ref[:, hv], z_ref[rows, hv], True).astype(y_ref.dtype)
    for st in (st_c, st_n, st_m):
        st.finish(c)


def _mlstm(proj, gates, b, t, c0_all, n0_all, m0_all, j, bgate_all, gain_all, L, BB, HB, y_dtype):
    h, dk = 8, 256
    dv = gain_all.shape[-1] // h
    nc = t // L
    assert BB == 1 or nc == 1
    hg = h // HB
    rb = _row_block(nc)
    has_s0 = c0_all is not None
    v_off = 2 * h * dk
    wv = HB * dv
    in_specs = [pl.BlockSpec(memory_space=pltpu.SMEM),
                pl.BlockSpec((BB * L, HB * dk), rb(lambda hi: hi)),
                pl.BlockSpec((BB * L, HB * dk), rb(lambda hi: hg + hi)),
                pl.BlockSpec((BB * L, wv), rb(lambda hi: v_off // wv + hi)),
                pl.BlockSpec((BB * L, wv), rb(lambda hi: (v_off + h * dv) // wv + hi)),
                pl.BlockSpec((BB * L, wv), rb(lambda hi: (v_off + 2 * h * dv) // wv + hi)),
                pl.BlockSpec((BB * L, LANES), rb(lambda hi: 0)),
                pl.BlockSpec((None, 1, wv), lambda bi, hi, ci: (j, 0, hi))]
    args = [bgate_all[j], proj, proj, proj, proj, proj, gates, _rows3(gain_all)]
    c_spec = pl.BlockSpec((None, BB, HB, dk, dv), lambda bi, hi, ci: (j, bi, hi, 0, 0))
    n_spec = pl.BlockSpec((None, BB, HB, 1, dk), lambda bi, hi, ci: (j, bi, hi, 0, 0))
    m_spec = pl.BlockSpec((None, BB, HB, 1, 1), lambda bi, hi, ci: (j, bi, hi, 0, 0))
    nl = 1
    if has_s0:
        assert c0_all.shape[0] == nl
        in_specs += [c_spec, n_spec, m_spec]
        args += [c0_all, n0_all.reshape(nl, b, h, 1, dk), m0_all.reshape(nl, b, h, 1, 1)]
    scratch = [pltpu.VMEM((BB, HB, dk, dv), F32), pltpu.VMEM((BB, HB, 1, dk), F32),
               pltpu.VMEM((BB, HB, 1, 1), F32)] if nc > 1 else []
    y, co, no, mo = pl.pallas_call(
        functools.partial(_mlstm_body, L=L, BB=BB, HB=HB, NC=nc, DK=dk, DV=dv, H=h, has_s0=has_s0),
        grid=(b // BB, hg, nc),
        in_specs=in_specs,
        out_specs=[pl.BlockSpec((BB * L, wv), rb(lambda hi: hi)), c_spec, n_spec, m_spec],
        out_shape=[jax.ShapeDtypeStruct((b * t, h * dv), y_dtype),
                   jax.ShapeDtypeStruct((nl, b, h, dk, dv), F32),
                   jax.ShapeDtypeStruct((nl, b, h, 1, dk), F32),
                   jax.ShapeDtypeStruct((nl, b, h, 1, 1), F32)],
        scratch_shapes=scratch,
        compiler_params=_params(("arbitrary", "arbitrary", "arbitrary"), 48),
        name="mlstm",
    )(*args)
    return y, co, no.reshape(nl, b, h, dk), mo.reshape(nl, b, h)


def _merge_masks(row_i, col_i, L):
    masks = []
    s = 1
    while s < L:
        masks.append(((row_i // (2 * s)) == (col_i // (2 * s))) & ((row_i // s) > (col_i // s)))
        s *= 2
    return masks


def _unit_lower_inverse(a_strict, eye, masks, mm):
    x = jnp.where(eye, 1.0, 0.0) - jnp.where(masks[0], a_strict, 0.0)
    for mask in masks[1:]:
        e = jnp.where(mask, a_strict, 0.0)
        x = x - mm(x, mm(e, x))
    return x


def _vpu_mm(a, b):
    out = a[:, 0:1] * b[0:1, :]
    for kk in range(1, a.shape[1]):
        out = out + a[:, kk:kk + 1] * b[kk:kk + 1, :]
    return out


def _bmm(a, b):
    return jnp.einsum("gik,gkj->gij", a.astype(BF16), b.astype(BF16), preferred_element_type=F32)


def _bmm_nt(a, b):
    return jnp.einsum("gik,gjk->gij", a.astype(BF16), b.astype(BF16), preferred_element_type=F32)


def _bmm_tn(a, b):
    return jnp.einsum("gki,gkj->gij", a.astype(BF16), b.astype(BF16), preferred_element_type=F32)


def _gdn_dims(conv_w_all, a_log_all, gain_all):
    dk = dv = gain_all.shape[-1]
    hv = a_log_all.shape[-1]
    cdim = conv_w_all.shape[-1]
    hk = (cdim - hv * dv) // (2 * dk)
    rep = hv // hk
    assert 2 * hv <= 128 and rep * hk == hv
    return dk, dv, hv, hk, rep, cdim


def _gdn_step_body(alv_ref, dtv_ref, qx_ref, kx_ref, vx_ref, z_ref, gt_ref, cwq_ref, cwk_ref, cwv_ref,
                   gain_ref, cq0_ref, ck0_ref, cv0_ref, s0_ref, y_ref, so_ref,
                   eq_scr, ek_scr, ev_scr, q_st, k_st, v_st, z_st, qs_st, ks_st, kw_st, u_st, y_st,
                   *, L, BB, HK, REP, DK, DV):
    HV = HK * REP
    R = HV * L
    taps = CONV_W - 1

    def conv(x_ref, c0_ref, scr, cw_ref, bb):
        scr[CONV_PAD - taps:CONV_PAD, :] = c0_ref[bb]
        scr[CONV_PAD:CONV_PAD + L, :] = x_ref[bb * L:(bb + 1) * L, :].astype(F32)
        acc = scr[pl.ds(CONV_PAD - taps, L), :] * cw_ref[0:1, :]
        for w in range(1, CONV_W):
            acc = acc + scr[pl.ds(CONV_PAD - taps + w, L), :] * cw_ref[w:w + 1, :]
        return _silu(acc)

    row_i = _iota2(R, R, 0)
    col_i = _iota2(R, R, 1)
    same_head = (row_i // L) == (col_i // L)
    eye = row_i == col_i
    incl = same_head & (row_i >= col_i)
    strict = same_head & (row_i > col_i)
    masks = [same_head & m for m in _merge_masks(row_i % L, col_i % L, L)]
    lane = _iota2(R, LANES, 1)
    head_of_row = _iota2(R, LANES, 0) // L
    sel_beta = lane == head_of_row
    sel_decay = lane == head_of_row + HV
    tril = (_iota2(L, L, 0) >= _iota2(L, L, 1)).astype(F32)

    def stack_cols(x, sel):
        x8 = jnp.concatenate([x] * (8 // L), axis=0)
        tiled = jnp.concatenate([x8] * (R // 8), axis=0)
        return jnp.sum(jnp.where(sel, tiled, 0.0), axis=1, keepdims=True)

    for bb in range(BB):
        rows = slice(bb * L, (bb + 1) * L)
        cq = conv(qx_ref, cq0_ref, eq_scr, cwq_ref, bb)
        ck = conv(kx_ref, ck0_ref, ek_scr, cwk_ref, bb)
        cv = conv(vx_ref, cv0_ref, ev_scr, cwv_ref, bb)
        for vh in range(HV):
            st = slice(vh * L, (vh + 1) * L)
            kh = vh // REP
            q_st[st, :] = cq[:, kh * DK:(kh + 1) * DK]
            k_st[st, :] = ck[:, kh * DK:(kh + 1) * DK]
            v_st[st, :] = cv[:, vh * DV:(vh + 1) * DV]
            z_st[st, :] = z_ref[rows, vh * DV:(vh + 1) * DV].astype(F32)
        q = q_st[...]
        k = k_st[...]
        q = q * lax.rsqrt(jnp.sum(q * q, axis=-1, keepdims=True) + EPS) * DK ** -0.5
        k = k * lax.rsqrt(jnp.sum(k * k, axis=-1, keepdims=True) + EPS)
        q_st[...] = q
        k_st[...] = k

        gt = gt_ref[rows, :]
        beta = stack_cols(_sigmoid(gt), sel_beta)
        g = -jnp.exp(alv_ref[...]) * jax.nn.softplus(gt + dtv_ref[...])
        g_cum = _vpu_mm(tril, g)
        g_col = stack_cols(g_cum, sel_decay)
        g_last = stack_cols(jnp.broadcast_to(g_cum[L - 1:L, :], (L, LANES)), sel_decay)
        g_row = _col_to_row(g_col, eye)
        decay = jnp.exp(jnp.where(incl, g_col - g_row, NEG_INF))
        kq = _bdot_nt(jnp.concatenate([k, q], axis=0), k)
        kk, qk = kq[:R], kq[R:]
        a = jnp.where(strict, beta * kk * decay, 0.0)
        x = _unit_lower_inverse(a, eye, masks, _bdot)

        for vh in range(HV):
            st = slice(vh * L, (vh + 1) * L)
            qk_rows = jnp.concatenate([q_st[st, :], k_st[st, :]], axis=0)
            both = _bdot(qk_rows, s0_ref[bb, vh])
            qs_st[st, :] = both[:L]
            ks_st[st, :] = both[L:]
        eg = jnp.exp(g_col)
        rhs = beta * v_st[...] - (beta * eg) * ks_st[...]
        u = _bdot(x, rhs)
        o = eg * qs_st[...] + _bdot(qk * decay, u)
        y_st[...] = _head_norm_gate(o, gain_ref[...], z_st[...], False)
        kw_st[...] = k * jnp.exp(g_last - g_col)
        u_st[...] = u
        eg_last = jnp.exp(g_last)
        for vh in range(HV):
            st = slice(vh * L, (vh + 1) * L)
            y_ref[rows, vh * DV:(vh + 1) * DV] = y_st[st, :].astype(y_ref.dtype)
            so_ref[bb, vh] = (eg_last[vh * L:vh * L + 1, :] * s0_ref[bb, vh]
                              + _bdot_tn(kw_st[st, :], u_st[st, :]))


def _gdn_step(proj, gates, b, t, s0_all, conv0_all, j, conv_w_all, a_log_all, dt_bias_all, gain_all,
              BB, y_dtype):
    dk, dv, hv, hk, rep, cdim = _gdn_dims(conv_w_all, a_log_all, gain_all)
    L = t
    R = hv * L
    assert 8 % L == 0 and R % 8 == 0 and dk == dv
    hg = 1
    wq, wv = hk * dk, hv * dv
    k_blk = 1
    v_blk = (2 * hk * dk) // wv
    z_blk = cdim // wv
    rb = _row_block(1)
    lanes_of = lambda vec: jnp.zeros((1, LANES), F32).at[0, hv:2 * hv].set(vec.astype(F32))

    def cspec(width, off):
        return pl.BlockSpec((None, CONV_W, width), lambda bi, hi, ci: (j, 0, off + hi))

    def c0spec(width, off):
        return pl.BlockSpec((None, BB, CONV_W - 1, width), lambda bi, hi, ci: (j, bi, 0, off + hi))

    st_spec = pl.BlockSpec((None, BB, hv, dk, dv), lambda bi, hi, ci: (j, bi, hi, 0, 0))
    in_specs = [pl.BlockSpec((1, LANES), lambda bi, hi, ci: (0, 0)),
                pl.BlockSpec((1, LANES), lambda bi, hi, ci: (0, 0)),
                pl.BlockSpec((BB * L, wq), rb(lambda hi: hi)),
                pl.BlockSpec((BB * L, wq), rb(lambda hi: k_blk + hi)),
                pl.BlockSpec((BB * L, wv), rb(lambda hi: v_blk + hi)),
                pl.BlockSpec((BB * L, wv), rb(lambda hi: z_blk + hi)),
                pl.BlockSpec((BB * L, LANES), rb(lambda hi: 0)),
                cspec(wq, 0), cspec(wq, k_blk), cspec(wv, v_blk),
                pl.BlockSpec((None, 1, dv), lambda bi, hi, ci: (j, 0, 0)),
                c0spec(wq, 0), c0spec(wq, k_blk), c0spec(wv, v_blk), st_spec]
    args = [lanes_of(a_log_all[j]), lanes_of(dt_bias_all[j]), proj, proj, proj, proj, gates,
            conv_w_all, conv_w_all, conv_w_all, _rows3(gain_all),
            conv0_all, conv0_all, conv0_all, s0_all]
    stacked = [pltpu.VMEM((R, dk), F32)] * 9
    y, so = pl.pallas_call(
        functools.partial(_gdn_step_body, L=L, BB=BB, HK=hk, REP=rep, DK=dk, DV=dv),
        grid=(b // BB, hg, 1),
        in_specs=in_specs,
        out_specs=[pl.BlockSpec((BB * L, wv), rb(lambda hi: hi)), st_spec],
        out_shape=[jax.ShapeDtypeStruct((b * t, hv * dv), y_dtype),
                   jax.ShapeDtypeStruct((1, b, hv, dk, dv), F32)],
        scratch_shapes=[pltpu.VMEM((L + CONV_PAD, wq), F32), pltpu.VMEM((L + CONV_PAD, wq), F32),
                        pltpu.VMEM((L + CONV_PAD, wv), F32)] + stacked,
        compiler_params=_params(("arbitrary", "arbitrary", "arbitrary"), 48),
        name="gdn_step",
    )(*args)
    return y, so


def _gdn_seq_body(al_ref, dt_ref, qx_ref, kx_ref, vx_ref, z_ref, gt_ref, cwq_ref, cwk_ref, cwv_ref,
                  gain_ref, y_ref, so_ref, s_scr, eq_scr, ek_scr, ev_scr,
                  *, TB, C, HBK, NTB, DK, DV, REP, HV):
    hb = pl.program_id(1)
    tb = pl.program_id(2)
    G = TB // C
    taps = CONV_W - 1

    @pl.when(tb == 0)
    def _():
        for scr in (eq_scr, ek_scr, ev_scr):
            scr[0:CONV_PAD, :] = jnp.zeros((CONV_PAD, scr.shape[1]), F32)
        s_scr[...] = jnp.zeros_like(s_scr)

    def conv(x_ref, scr, cw_ref):
        scr[CONV_PAD:CONV_PAD + TB, :] = x_ref[...].astype(F32)
        acc = scr[pl.ds(CONV_PAD - taps, TB), :] * cw_ref[0:1, :]
        for w in range(1, CONV_W):
            acc = acc + scr[pl.ds(CONV_PAD - taps + w, TB), :] * cw_ref[w:w + 1, :]
        if NTB > 1:
            scr[0:CONV_PAD, :] = scr[TB:TB + CONV_PAD, :]
        return _silu(acc)

    cq = conv(qx_ref, eq_scr, cwq_ref)
    ck = conv(kx_ref, ek_scr, cwk_ref)
    cv = conv(vx_ref, ev_scr, cwv_ref)

    row_i = _iota2(C, C, 0)
    col_i = _iota2(C, C, 1)
    eye = row_i == col_i
    incl = row_i >= col_i
    PW = REP * C
    prow = _iota2(C, PW, 0)
    plane = _iota2(C, PW, 1)
    pcol = plane % C
    phead = plane // C
    p_eye = prow == pcol
    p_incl = prow >= pcol
    p_strict = prow > pcol
    p_masks = _merge_masks(prow, pcol, C)
    bd_mask = (_iota2(PW, PW, 0) // C) == (_iota2(PW, PW, 1) // C)

    def block_diag(xp):
        return jnp.where(bd_mask, jnp.concatenate([xp] * REP, axis=1), 0.0)

    def packed_mm(ap, bp):
        return _bmm(ap, block_diag(bp))

    def pack_cols(cols):
        out = cols[0]
        for r in range(1, REP):
            out = jnp.where(phead >= r, cols[r], out)
        return out

    lane = _iota2(TB, 128, 1)
    gt = gt_ref[...]
    for kh in range(HBK):
        q = cq[:, kh * DK:(kh + 1) * DK]
        k = ck[:, kh * DK:(kh + 1) * DK]
        q = q * lax.rsqrt(jnp.sum(q * q, axis=-1, keepdims=True) + EPS) * DK ** -0.5
        k = k * lax.rsqrt(jnp.sum(k * k, axis=-1, keepdims=True) + EPS)
        q3 = q.reshape(G, C, DK)
        k3 = k.reshape(G, C, DK)
        kq = _bmm_nt(jnp.concatenate([k3, q3], axis=1), k3)
        kk, qk = kq[:, :C, :], kq[:, C:, :]
        betas, g_cols, g_rows = [], [], []
        for r in range(REP):
            head = (hb * HBK + kh) * REP + r
            betas.append(_sigmoid(_pick_lane(gt, lane, head)).reshape(G, C, 1))
            a_neg = -jnp.exp(jnp.full((1, 1), al_ref[head], F32))
            g = (a_neg * jax.nn.softplus(_pick_lane(gt, lane, HV + head) + dt_ref[head])).reshape(G, C, 1)
            g_lanes = jnp.sum(jnp.where(eye, g, 0.0), axis=1, keepdims=True)
            g_cols.append(jnp.sum(jnp.where(incl, g_lanes, 0.0), axis=2, keepdims=True))
            g_rows.append(jnp.sum(jnp.where(row_i <= col_i, g, 0.0), axis=1, keepdims=True))
        decay_p = jnp.exp(jnp.where(p_incl, pack_cols(g_cols) - jnp.concatenate(g_rows, axis=-1), NEG_INF))
        a_p = jnp.where(p_strict, pack_cols(betas) * jnp.concatenate([kk] * REP, axis=-1) * decay_p, 0.0)
        x_p = _unit_lower_inverse(a_p, p_eye, p_masks, packed_mm)
        for r in range(REP):
            vh = kh * REP + r
            hv = slice(vh * DV, (vh + 1) * DV)
            beta, g_col = betas[r], g_cols[r]
            x = x_p[:, :, r * C:(r + 1) * C]
            decay = decay_p[:, :, r * C:(r + 1) * C]
            v3 = cv[:, hv].reshape(G, C, DV)
            eg = jnp.exp(g_col)
            wu = _bmm(x, jnp.concatenate([(beta * eg) * k3, beta * v3], axis=-1))
            qo = _bmm(qk * decay, wu)
            o0 = qo[:, :, DK:]
            g_last = g_col[:, C - 1:C, :]
            mb = _bmm_tn(k3 * jnp.exp(g_last - g_col), wu)
            b_eff = mb[:, :, DK:]
            lhs = jnp.concatenate([eg * q3 - qo[:, :, :DK], mb[:, :, :DK]], axis=1).astype(BF16)
            eg_last = jnp.exp(g_last)
            s = s_scr[vh]
            outs = []
            for c in range(G):
                both = jnp.dot(lhs[c], s.astype(BF16), preferred_element_type=F32)
                outs.append(both[:C] + o0[c])
                s = eg_last[c] * s - both[C:] + b_eff[c]
            s_scr[vh] = s
            o = jnp.concatenate(outs, axis=0) if G > 1 else outs[0]
            y_ref[:, hv] = _head_norm_gate(o, gain_ref[...], z_ref[:, hv], False).astype(y_ref.dtype)

    @pl.when(tb == NTB - 1)
    def _():
        so_ref[...] = s_scr[...]


def _gdn_seq(proj, gates, b, t, j, conv_w_all, a_log_all, dt_bias_all, gain_all, TB, C, HBK, y_dtype):
    dk, dv, hv, hk, rep, cdim = _gdn_dims(conv_w_all, a_log_all, gain_all)
    assert dk == dv
    ntb = t // TB
    hg = hk // HBK
    wq, wv = HBK * dk, HBK * rep * dv
    k_blk = (hk * dk) // wq
    v_blk = (2 * hk * dk) // wv
    z_blk = cdim // wv
    rb = _row_block(ntb)

    def cspec(width, off):
        return pl.BlockSpec((None, CONV_W, width), lambda bi, hi, ti: (j, 0, off + hi))

    in_specs = [pl.BlockSpec(memory_space=pltpu.SMEM),
                pl.BlockSpec(memory_space=pltpu.SMEM),
                pl.BlockSpec((TB, wq), rb(lambda hi: hi)),
                pl.BlockSpec((TB, wq), rb(lambda hi: k_blk + hi)),
                pl.BlockSpec((TB, wv), rb(lambda hi: v_blk + hi)),
                pl.BlockSpec((TB, wv), rb(lambda hi: z_blk + hi)),
                pl.BlockSpec((TB, LANES), rb(lambda hi: 0)),
                cspec(wq, 0), cspec(wq, k_blk), cspec(wv, v_blk),
                pl.BlockSpec((None, 1, dv), lambda bi, hi, ti: (j, 0, 0))]
    args = [a_log_all[j], dt_bias_all[j], proj, proj, proj, proj, gates,
            conv_w_all, conv_w_all, conv_w_all, _rows3(gain_all)]
    st_spec = pl.BlockSpec((None, None, HBK * rep, dk, dv), lambda bi, hi, ti: (0, bi, hi, 0, 0))
    return pl.pallas_call(
        functools.partial(_gdn_seq_body, TB=TB, C=C, HBK=HBK, NTB=ntb, DK=dk, DV=dv, REP=rep, HV=hv),
        grid=(b, hg, ntb),
        in_specs=in_specs,
        out_specs=[pl.BlockSpec((TB, wv), rb(lambda hi: hi)), st_spec],
        out_shape=[jax.ShapeDtypeStruct((b * t, hv * dv), y_dtype),
                   jax.ShapeDtypeStruct((1, b, hv, dk, dv), F32)],
        scratch_shapes=[pltpu.VMEM((HBK * rep, dk, dv), F32),
                        pltpu.VMEM((TB + CONV_PAD, wq), F32), pltpu.VMEM((TB + CONV_PAD, wq), F32),
                        pltpu.VMEM((TB + CONV_PAD, wv), F32)],
        compiler_params=_params(("arbitrary", "arbitrary", "arbitrary"), 48),
        name="gdn_seq",
    )(*args)


def _rope_tables(pos, dk):
    half = dk // 2
    inv = ROPE_BASE ** (-jnp.arange(half, dtype=F32) / half)
    ang = pos.astype(F32)[:, None] * inv[None, :]
    return jnp.cos(ang), jnp.sin(ang)


def _trunk(x, p, states, pos, w, cfg):
    (norm_pre, norm_post, ple_proj, ple_gate, ret_w_in, ret_head_norm, ret_w_out,
     mlstm_w_in, mlstm_b_gate, mlstm_head_norm, mlstm_w_out,
     gdn_w_in, gdn_conv_w, gdn_a_log, gdn_dt_bias, gdn_head_norm, gdn_w_out) = w
    ret_s, ml_c, ml_n, ml_m, gdn_s, gdn_conv = states
    b, t, d = x.shape
    depth = norm_pre.shape[0]
    n_ret = ret_w_in.shape[0]
    m = b * t
    tm, ydt, pdt = cfg["tm"], cfg["y_dtype"], cfg["proj_dtype"]
    cos, sin = _rope_tables(pos, 256)
    r = x.reshape(m, d)
    p2 = p.reshape(depth, m, p.shape[-1])
    ret_out = None
    outs = {}
    keep = CONV_W - 1
    for i in range(depth):
        kind, j = i % 3, i // 3
        if kind == 0:
            proj, _, _ = _inproj(r, norm_pre, i, ret_w_in, j, tm, cfg["tn_in"], pdt)
            y, ret_out = _retention(proj, b, t, ret_s, j, cos, sin, ret_head_norm, ret_out, n_ret,
                                    cfg["ret_L"], cfg["ret_BB"], cfg["ret_HB"], ydt)
            w_out = ret_w_out
        elif kind == 1:
            proj, gates, _ = _inproj(r, norm_pre, i, mlstm_w_in, j, tm, cfg["tn_in"], pdt)
            y, outs["c"], outs["n"], outs["m"] = _mlstm(
                proj, gates, b, t, ml_c, ml_n, ml_m, j, mlstm_b_gate, mlstm_head_norm,
                cfg["ml_L"], cfg["ml_BB"], cfg["ml_HB"], ydt)
            w_out = mlstm_w_out
        else:
            cdim = gdn_conv_w.shape[-1]
            from_tail = pdt != F32
            assert t >= keep and (not from_tail or (t % tm == 0 and keep <= TAIL_ROWS))
            proj, gates, tail = _inproj(r, norm_pre, i, gdn_w_in, j, tm, cfg["tn_in"], pdt, want_tail=from_tail)
            if gdn_s is None:
                y, outs["gs"] = _gdn_seq(proj, gates, b, t, j, gdn_conv_w, gdn_a_log, gdn_dt_bias,
                                         gdn_head_norm, cfg["gdn_TB"], cfg["gdn_L"], cfg["gdn_HBK"], ydt)
            else:
                y, outs["gs"] = _gdn_step(proj, gates, b, t, gdn_s, gdn_conv, j, gdn_conv_w, gdn_a_log,
                                          gdn_dt_bias, gdn_head_norm, cfg["gdn_BB"], ydt)
            if from_tail:
                last_tiles = tail.reshape(b, t // tm, TAIL_ROWS, -1)[:, -1]
                outs["gc"] = last_tiles[:, TAIL_ROWS - keep:, :cdim][None]
            else:
                outs["gc"] = proj.reshape(b, t, -1)[:, t - keep:, :cdim][None]
            w_out = gdn_w_out
        mix = _outproj(y, w_out, j, tm, cfg["tn_out"])
        r = _post_ple(mix, r, norm_post, p2, ple_proj, ple_gate, i, cfg["tm_ple"], cfg["tn_ple"])
    return (r.reshape(b, t, d), ret_out, outs["c"], outs["n"], outs["m"], outs["gs"], outs["gc"])


_PROMPT_CFG = dict(tm=2048, tn_in=512, tn_out=512, tm_ple=1024, tn_ple=256, y_dtype=BF16, proj_dtype=BF16,
                   ret_L=256, ret_BB=1, ret_HB=2, ml_L=256, ml_BB=1, ml_HB=2,
                   gdn_TB=1024, gdn_L=64, gdn_HBK=1)
_SAMPLE_CFG = dict(tm=512, tn_in=1024, tn_out=512, tm_ple=512, tn_ple=512, y_dtype=F32, proj_dtype=F32,
                   ret_L=4, ret_BB=2, ret_HB=8, ml_L=4, ml_BB=2, ml_HB=8, gdn_BB=2)


def kernel(x_prompt, x_sample, state_ret_S, state_mlstm_C, state_mlstm_n, state_mlstm_m, state_gdn_S, state_gdn_conv, p_prompt, p_sample, norm_pre, norm_post, ple_proj, ple_gate, ret_w_in, ret_head_norm, ret_w_out, mlstm_w_in, mlstm_b_gate, mlstm_head_norm, mlstm_w_out, gdn_w_in, gdn_conv_w, gdn_a_log, gdn_dt_bias, gdn_head_norm, gdn_w_out):
    w = (norm_pre, norm_post, ple_proj, ple_gate, ret_w_in, ret_head_norm, ret_w_out,
         mlstm_w_in, mlstm_b_gate, mlstm_head_norm, mlstm_w_out,
         gdn_w_in, gdn_conv_w, gdn_a_log, gdn_dt_bias, gdn_head_norm, gdn_w_out)
    pos_p = jnp.arange(x_prompt.shape[1])
    yp, ret_p, mc_p, mn_p, mm_p, gs_p, gc_p = _trunk(
        x_prompt, p_prompt, (None,) * 6, pos_p, w, _PROMPT_CFG)
    pos_s = PAST_LEN + jnp.arange(x_sample.shape[1])
    ys, ret_s, mc_s, mn_s, mm_s, gs_s, gc_s = _trunk(
        x_sample, p_sample,
        (state_ret_S, state_mlstm_C, state_mlstm_n, state_mlstm_m, state_gdn_S, state_gdn_conv),
        pos_s, w, _SAMPLE_CFG)
    return (yp, ys, ret_p, mc_p, mn_p, mm_p, gs_p, gc_p, ret_s, mc_s, mn_s, mm_s, gs_s, gc_s)
```

```python
import functools

import jax
import jax.numpy as jnp
from jax import lax
from jax.experimental import pallas as pl
from jax.experimental.pallas import tpu as pltpu

F32 = jnp.float32
BF16 = jnp.bfloat16
EPS = 1e-6
ROPE_BASE = 10000.0
CONV_W = 4
PAST_LEN = 16384
MIB = 1024 * 1024
NEG_INF = float("-inf")
CONV_PAD = 8
LANES = 128
TAIL_ROWS = 8
VMEM_LIMIT_MATMUL_MIB = 56
VMEM_LIMIT_MIXER_MIB = 48


def _params(sem, vmem_mib):
    return pltpu.CompilerParams(dimension_semantics=sem, vmem_limit_bytes=vmem_mib * MIB)


def _bdot(a, b):
    return jnp.dot(a.astype(BF16), b.astype(BF16), preferred_element_type=F32)


def _bdot_nt(a, b):
    return lax.dot_general(a.astype(BF16), b.astype(BF16), (((1,), (1,)), ((), ())),
                           preferred_element_type=F32)


def _bdot_tn(a, b):
    return lax.dot_general(a.astype(BF16), b.astype(BF16), (((0,), (0,)), ((), ())),
                           preferred_element_type=F32)


def _sigmoid(x):
    return 0.5 * jnp.tanh(0.5 * x) + 0.5


def _silu(x):
    h = 0.5 * x
    return h * jnp.tanh(h) + h


def _rows3(table):
    return table.reshape(table.shape[0], 1, table.shape[1])


SLAB = 256


def _row_slabs(tm):
    step = min(SLAB, tm)
    return [slice(s, s + step) for s in range(0, tm, step)]


def _once_per_row_tile(block_shape, index_map):
    return pl.BlockSpec(block_shape, index_map, pipeline_mode=pl.Buffered(1))


def _inproj_body(x_ref, g_ref, w_ref, *rest, w_is_nk, n_main, gate_cols, has_tail):
    has_gates = gate_cols > 0
    outs = list(rest[:1 + has_gates + has_tail])
    h_ref = rest[-1]
    o_ref = outs.pop(0)
    gates_ref = outs.pop(0) if has_gates else None
    tail_ref = outs.pop(0) if has_tail else None
    c = pl.program_id(1)

    @pl.when(c == 0)
    def _():
        for rows in _row_slabs(x_ref.shape[0]):
            x = x_ref[rows, :]
            ms = jnp.mean(x * x, axis=-1, keepdims=True)
            h_ref[rows, :] = (x * lax.rsqrt(ms + EPS) * g_ref[...]).astype(BF16)

    w = w_ref[...].astype(BF16)
    contract = (((1,), (1,)), ((), ())) if w_is_nk else (((1,), (0,)), ((), ()))
    acc = lax.dot_general(h_ref[...], w, contract, preferred_element_type=F32)

    def write_main():
        o_ref[...] = acc.astype(o_ref.dtype)
        if has_tail:
            tail_ref[...] = acc[acc.shape[0] - TAIL_ROWS:, :]

    if has_gates:
        pl.when(c < n_main)(write_main)

        @pl.when(c == n_main)
        def _():
            lane = _iota2(acc.shape[0], LANES, 1)
            gates_ref[...] = jnp.where(lane < gate_cols, acc[:, :LANES], 0.0)
    else:
        write_main()


def _inproj(x2d, g_all, layer, w_all, j, tm, tn, out_dtype, want_tail=False):
    m, d = x2d.shape
    n = w_all.shape[-1]
    n_main = n // tn
    has_gates = n % tn != 0
    assert n - n_main * tn <= LANES
    w_is_nk = n % LANES != 0
    if w_is_nk:
        w_all = jnp.swapaxes(w_all, 1, 2)
        w_spec = pl.BlockSpec((None, tn, d), lambda i, c: (j, c, 0))
    else:
        w_spec = pl.BlockSpec((None, d, tn), lambda i, c: (j, 0, c))

    def main_col(c):
        return jnp.minimum(c, n_main - 1)

    out_specs = [pl.BlockSpec((tm, tn), lambda i, c: (i, main_col(c)))]
    out_shape = [jax.ShapeDtypeStruct((m, n_main * tn), out_dtype)]
    if has_gates:
        out_specs.append(pl.BlockSpec((tm, LANES), lambda i, c: (i, 0)))
        out_shape.append(jax.ShapeDtypeStruct((m, LANES), F32))
    if want_tail:
        out_specs.append(pl.BlockSpec((None, TAIL_ROWS, tn), lambda i, c: (i, 0, main_col(c))))
        out_shape.append(jax.ShapeDtypeStruct((m // tm, TAIL_ROWS, n_main * tn), F32))
    res = pl.pallas_call(
        functools.partial(_inproj_body, w_is_nk=w_is_nk, n_main=n_main, gate_cols=n - n_main * tn,
                          has_tail=want_tail),
        grid=(m // tm, n_main + has_gates),
        in_specs=[_once_per_row_tile((tm, d), lambda i, c: (i, 0)),
                  pl.BlockSpec((None, 1, d), lambda i, c: (layer, 0, 0)),
                  w_spec],
        out_specs=out_specs,
        out_shape=out_shape,
        scratch_shapes=[pltpu.VMEM((tm, d), BF16)],
        compiler_params=_params(("arbitrary", "arbitrary"), VMEM_LIMIT_MATMUL_MIB),
        name="inproj",
    )(x2d, _rows3(g_all), w_all)
    res = list(res)
    proj = res.pop(0)
    gates = res.pop(0) if has_gates else None
    tail = res.pop(0) if want_tail else None
    return proj, gates, tail


def _outproj_body(y_ref, w_ref, o_ref):
    o_ref[...] = jnp.dot(y_ref[...].astype(BF16), w_ref[...].astype(BF16), preferred_element_type=F32)


def _outproj(y2d, w_all, j, tm, tn):
    m, k = y2d.shape
    n = w_all.shape[-1]
    return pl.pallas_call(
        _outproj_body,
        grid=(m // tm, n // tn),
        in_specs=[_once_per_row_tile((tm, k), lambda i, c: (i, 0)),
                  pl.BlockSpec((None, k, tn), lambda i, c: (j, 0, c))],
        out_specs=pl.BlockSpec((tm, tn), lambda i, c: (i, c)),
        out_shape=jax.ShapeDtypeStruct((m, n), F32),
        compiler_params=_params(("arbitrary", "arbitrary"), VMEM_LIMIT_MATMUL_MIB),
        name="outproj",
    )(y2d, w_all)


def _post_ple_body(mix_ref, r_ref, g_ref, p_ref, proj_ref, gate_ref, o_ref, rs_ref, r1b_ref, *, tn):
    c = pl.program_id(1)

    @pl.when(c == 0)
    def _():
        for rows in _row_slabs(mix_ref.shape[0]):
            mix = mix_ref[rows, :]
            rs = lax.rsqrt(jnp.mean(mix * mix, axis=-1, keepdims=True) + EPS)
            rs_ref[rows, :] = rs
            r1b_ref[rows, :] = (r_ref[rows, :] + mix * rs * g_ref[...]).astype(BF16)

    gate = jnp.dot(r1b_ref[...], gate_ref[...].astype(BF16), preferred_element_type=F32)
    emb = _bdot(p_ref[...], proj_ref[...])
    cols = pl.ds(pl.multiple_of(c * tn, tn), tn)
    r1 = r_ref[:, cols] + mix_ref[:, cols] * rs_ref[...] * g_ref[:, cols]
    o_ref[...] = r1 + emb * _sigmoid(gate)


def _post_ple(mix, r, g_all, p_all, proj_all, gate_all, layer, tm, tn):
    m, d = r.shape
    pd = p_all.shape[-1]
    return pl.pallas_call(
        functools.partial(_post_ple_body, tn=tn),
        grid=(m // tm, d // tn),
        in_specs=[pl.BlockSpec((tm, d), lambda i, c: (i, 0)),
                  pl.BlockSpec((tm, d), lambda i, c: (i, 0)),
                  pl.BlockSpec((None, 1, d), lambda i, c: (layer, 0, 0)),
                  pl.BlockSpec((None, tm, pd), lambda i, c: (layer, i, 0)),
                  pl.BlockSpec((None, pd, tn), lambda i, c: (layer, 0, c)),
                  pl.BlockSpec((None, d, tn), lambda i, c: (layer, 0, c))],
        out_specs=pl.BlockSpec((tm, tn), lambda i, c: (i, c)),
        out_shape=jax.ShapeDtypeStruct((m, d), F32),
        scratch_shapes=[pltpu.VMEM((tm, 1), F32), pltpu.VMEM((tm, d), BF16)],
        compiler_params=_params(("arbitrary", "arbitrary"), VMEM_LIMIT_MATMUL_MIB),
        name="post_ple",
    )(mix, r, _rows3(g_all), p_all, proj_all, gate_all)


def _iota2(n, m, dim):
    return lax.broadcasted_iota(jnp.int32, (n, m), dim)


def _col_to_row(col, eye):
    return jnp.sum(jnp.where(eye, col, 0.0), axis=0, keepdims=True)


def _cumsum_col_row(col, row_i, col_i, eye):
    row = _col_to_row(col, eye)
    c_col = jnp.sum(jnp.where(col_i <= row_i, row, 0.0), axis=1, keepdims=True)
    c_row = jnp.sum(jnp.where(row_i <= col_i, col, 0.0), axis=0, keepdims=True)
    return c_col, c_row


def _pick_lane(blk, lane_iota, idx):
    return jnp.sum(jnp.where(lane_iota == idx, blk, 0.0), axis=1, keepdims=True)


def _head_norm_gate(o, gain, z, center):
    if center:
        o = o - jnp.mean(o, axis=-1, keepdims=True)
    y = o * lax.rsqrt(jnp.mean(o * o, axis=-1, keepdims=True) + EPS) * gain
    return y * _silu(z.astype(F32))


def _row_block(nc):
    return lambda col: (lambda bi, hi, ci: (bi * nc + ci, col(hi)))


class _State:
    def __init__(self, in_ref, out_ref, scr, nc):
        self.in_ref, self.out_ref, self.scr, self.nc = in_ref, out_ref, scr, nc

    def start(self, chunk):
        if self.nc > 1:
            @pl.when(chunk == 0)
            def _():
                if self.in_ref is None:
                    self.scr[...] = jnp.zeros_like(self.scr)
                else:
                    self.scr[...] = self.in_ref[...]

    def get(self, bb, hh):
        if self.nc > 1:
            return self.scr[bb, hh]
        if self.in_ref is None:
            return jnp.zeros(self.out_ref.shape[2:], F32)
        return self.in_ref[bb, hh]

    def put(self, bb, hh, val):
        if self.nc > 1:
            self.scr[bb, hh] = val
        else:
            self.out_ref[bb, hh] = val

    def finish(self, chunk):
        if self.nc > 1:
            @pl.when(chunk == self.nc - 1)
            def _():
                self.out_ref[...] = self.scr[...]


def _ret_body(*refs, L, BB, HB, NC, DK, DV, has_s0, has_prev):
    lg_ref, q_ref, k_ref, v_ref, z_ref, cos_ref, sin_ref, gain_ref = refs[:8]
    s0_ref = refs[8] if has_s0 else None
    rest = refs[8 + has_s0 + has_prev:]
    y_ref, so_ref = rest[:2]
    st = _State(s0_ref, so_ref, rest[2] if NC > 1 else None, NC)
    hb = pl.program_id(1)
    c = pl.program_id(2)
    st.start(c)

    cos = cos_ref[...]
    sin = sin_ref[...]
    half = DK // 2

    def rot(x):
        x1, x2 = x[:, :half], x[:, half:]
        return jnp.concatenate([x1 * cos - x2 * sin, x2 * cos + x1 * sin], axis=-1)

    row_i = _iota2(L, L, 0)
    col_i = _iota2(L, L, 1)
    rel = (row_i - col_i).astype(F32)
    idx = _iota2(L, 1, 0).astype(F32)
    for hh in range(HB):
        lg = lg_ref[hb * HB + hh]
        decay = jnp.exp(jnp.where(rel >= 0, lg * rel, NEG_INF))
        w_in = jnp.exp(lg * (idx + 1.0))
        w_out = jnp.exp(lg * (L - 1.0 - idx))
        w_all = jnp.exp(jnp.full((1, 1), L, F32) * lg)
        hq = slice(hh * DK, (hh + 1) * DK)
        hv = slice(hh * DV, (hh + 1) * DV)
        for bb in range(BB):
            rows = slice(bb * L, (bb + 1) * L)
            q = rot(q_ref[rows, hq].astype(F32))
            k = rot(k_ref[rows, hq].astype(F32)) * DK ** -0.5
            v = v_ref[rows, hv]
            s = st.get(bb, hh)
            scores = _bdot_nt(q, k) * decay
            o = _bdot(scores, v) + _bdot(q * w_in, s)
            st.put(bb, hh, w_all * s + _bdot_tn(k * w_out, v))
            y_ref[rows, hv] = _head_norm_gate(o, gain_ref[:, hv], z_ref[rows, hv], True).astype(y_ref.dtype)
    st.finish(c)


def _retention(proj, b, t, s0_all, j, cos, sin, gain_all, so_prev, n_layers, L, BB, HB, y_dtype):
    h, dk = 8, 256
    dv = gain_all.shape[-1] // h
    nc = t // L
    assert BB == 1 or nc == 1
    hg = h // HB
    rb = _row_block(nc)
    lg = jnp.log1p(-jnp.exp2(-5.0 - jnp.arange(h, dtype=F32)))
    has_s0 = s0_all is not None
    v_blk = (2 * h * dk) // (HB * dv)
    z_blk = (2 * h * dk + h * dv) // (HB * dv)
    in_specs = [pl.BlockSpec(memory_space=pltpu.SMEM),
                pl.BlockSpec((BB * L, HB * dk), rb(lambda hi: hi)),
                pl.BlockSpec((BB * L, HB * dk), rb(lambda hi: hg + hi)),
                pl.BlockSpec((BB * L, HB * dv), rb(lambda hi: v_blk + hi)),
                pl.BlockSpec((BB * L, HB * dv), rb(lambda hi: z_blk + hi)),
                pl.BlockSpec((L, dk // 2), lambda bi, hi, ci: (ci, 0)),
                pl.BlockSpec((L, dk // 2), lambda bi, hi, ci: (ci, 0)),
                pl.BlockSpec((None, 1, HB * dv), lambda bi, hi, ci: (j, 0, hi))]
    args = [lg, proj, proj, proj, proj, cos, sin, _rows3(gain_all)]
    st_spec = pl.BlockSpec((None, BB, HB, dk, dv), lambda bi, hi, ci: (j, bi, hi, 0, 0))
    if has_s0:
        in_specs.append(st_spec)
        args.append(s0_all)
    aliases = {}
    if so_prev is not None:
        in_specs.append(pl.BlockSpec(memory_space=pl.ANY))
        args.append(so_prev)
        aliases = {len(args) - 1: 1}
    return pl.pallas_call(
        functools.partial(_ret_body, L=L, BB=BB, HB=HB, NC=nc, DK=dk, DV=dv, has_s0=has_s0,
                          has_prev=so_prev is not None),
        grid=(b // BB, hg, nc),
        in_specs=in_specs,
        out_specs=[pl.BlockSpec((BB * L, HB * dv), rb(lambda hi: hi)), st_spec],
        out_shape=[jax.ShapeDtypeStruct((b * t, h * dv), y_dtype),
                   jax.ShapeDtypeStruct((n_layers, b, h, dk, dv), F32)],
        scratch_shapes=[pltpu.VMEM((BB, HB, dk, dv), F32)] if nc > 1 else [],
        input_output_aliases=aliases,
        compiler_params=_params(("arbitrary", "arbitrary", "arbitrary"), VMEM_LIMIT_MIXER_MIB),
        name="retention",
    )(*args)


def _mlstm_body(*refs, L, BB, HB, NC, DK, DV, H, has_s0):
    bg_ref, q_ref, k_ref, v_ref, og_ref, z_ref, gt_ref, gain_ref = refs[:8]
    c0_ref, n0_ref, m0_ref = refs[8:11] if has_s0 else (None, None, None)
    rest = refs[8 + 3 * has_s0:]
    y_ref, co_ref, no_ref, mo_ref = rest[:4]
    scr = rest[4:] if NC > 1 else (None, None, None)
    st_c = _State(c0_ref, co_ref, scr[0], NC)
    st_n = _State(n0_ref, no_ref, scr[1], NC)
    st_m = _State(m0_ref, mo_ref, scr[2], NC)
    hb = pl.program_id(1)
    c = pl.program_id(2)
    for st in (st_c, st_n, st_m):
        st.start(c)

    row_i = _iota2(L, L, 0)
    col_i = _iota2(L, L, 1)
    eye = row_i == col_i
    causal = row_i >= col_i
    lane = _iota2(L, 128, 1)
    for bb in range(BB):
        rows = slice(bb * L, (bb + 1) * L)
        gt = gt_ref[rows, :]
        for hh in range(HB):
            head = hb * HB + hh
            hq = slice(hh * DK, (hh + 1) * DK)
            hv = slice(hh * DV, (hh + 1) * DV)
            ig = _pick_lane(gt, lane, head) + bg_ref[head]
            fg = _pick_lane(gt, lane, H + head) + bg_ref[H + head]
            lf = jax.nn.log_sigmoid(fg)
            b_col, b_row = _cumsum_col_row(lf, row_i, col_i, eye)
            i_row = _col_to_row(ig, eye)
            q = q_ref[rows, hq].astype(F32) * DK ** -0.5
            k = k_ref[rows, hq].astype(F32)
            v = v_ref[rows, hv]
            cm = st_c.get(bb, hh)
            nv = st_n.get(bb, hh)
            m_prev = st_m.get(bb, hh)
            dlog = jnp.where(causal, b_col - b_row + i_row, NEG_INF)
            inter = b_col + m_prev
            mt = jnp.maximum(inter, jnp.max(dlog, axis=1, keepdims=True))
            s = _bdot_nt(q, k) * jnp.exp(dlog - mt)
            wi = jnp.exp(inter - mt)
            num = _bdot(s, v) + wi * _bdot(q, cm)
            den = jnp.sum(s, axis=1, keepdims=True) + wi * jnp.sum(q * nv, axis=1, keepdims=True)
            ht = num / jnp.maximum(jnp.abs(den), jnp.exp(-mt))
            m_new = mt[L - 1:L, :]
            b_last = b_col[L - 1:L, :]
            w_last = jnp.exp(b_last - b_col + ig - m_new)
            dec = jnp.exp(b_last + m_prev - m_new)
            kw = k * w_last
            st_c.put(bb, hh, dec * cm + _bdot_tn(kw, v))
            st_n.put(bb, hh, dec * nv + jnp.sum(kw, axis=0, keepdims=True))
            st_m.put(bb, hh, m_new)
            hcell = ht * _sigmoid(og_ref[rows, hv].astype(F32))
            y_ref[rows, hv] = _head_norm_gate(hcell, gain_ref[:, hv], z_ref[rows, hv], True).astype(y_ref.dtype)
    for st in (st_c, st_n, st_m):
        st.finish(c)


def _mlstm(proj, gates, b, t, c0_all, n0_all, m0_all, j, bgate_all, gain_all, L, BB, HB, y_dtype):
    h, dk = 8, 256
    dv = gain_all.shape[-1] // h
    nc = t // L
    assert BB == 1 or nc == 1
    hg = h // HB
    rb = _row_block(nc)
    has_s0 = c0_all is not None
    v_off = 2 * h * dk
    wv = HB * dv
    in_specs = [pl.BlockSpec(memory_space=pltpu.SMEM),
                pl.BlockSpec((BB * L, HB * dk), rb(lambda hi: hi)),
                pl.BlockSpec((BB * L, HB * dk), rb(lambda hi: hg + hi)),
                pl.BlockSpec((BB * L, wv), rb(lambda hi: v_off // wv + hi)),
                pl.BlockSpec((BB * L, wv), rb(lambda hi: (v_off + h * dv) // wv + hi)),
                pl.BlockSpec((BB * L, wv), rb(lambda hi: (v_off + 2 * h * dv) // wv + hi)),
                pl.BlockSpec((BB * L, LANES), rb(lambda hi: 0)),
                pl.BlockSpec((None, 1, wv), lambda bi, hi, ci: (j, 0, hi))]
    args = [bgate_all[j], proj, proj, proj, proj, proj, gates, _rows3(gain_all)]
    c_spec = pl.BlockSpec((None, BB, HB, dk, dv), lambda bi, hi, ci: (j, bi, hi, 0, 0))
    n_spec = pl.BlockSpec((None, BB, HB, 1, dk), lambda bi, hi, ci: (j, bi, hi, 0, 0))
    m_spec = pl.BlockSpec((None, BB, HB, 1, 1), lambda bi, hi, ci: (j, bi, hi, 0, 0))
    nl = 1
    if has_s0:
        assert c0_all.shape[0] == nl
        in_specs += [c_spec, n_spec, m_spec]
        args += [c0_all, n0_all.reshape(nl, b, h, 1, dk), m0_all.reshape(nl, b, h, 1, 1)]
    scratch = [pltpu.VMEM((BB, HB, dk, dv), F32), pltpu.VMEM((BB, HB, 1, dk), F32),
               pltpu.VMEM((BB, HB, 1, 1), F32)] if nc > 1 else []
    y, co, no, mo = pl.pallas_call(
        functools.partial(_mlstm_body, L=L, BB=BB, HB=HB, NC=nc, DK=dk, DV=dv, H=h, has_s0=has_s0),
        grid=(b // BB, hg, nc),
        in_specs=in_specs,
        out_specs=[pl.BlockSpec((BB * L, wv), rb(lambda hi: hi)), c_spec, n_spec, m_spec],
        out_shape=[jax.ShapeDtypeStruct((b * t, h * dv), y_dtype),
                   jax.ShapeDtypeStruct((nl, b, h, dk, dv), F32),
                   jax.ShapeDtypeStruct((nl, b, h, 1, dk), F32),
                   jax.ShapeDtypeStruct((nl, b, h, 1, 1), F32)],
        scratch_shapes=scratch,
        compiler_params=_params(("arbitrary", "arbitrary", "arbitrary"), VMEM_LIMIT_MIXER_MIB),
        name="mlstm",
    )(*args)
    return y, co, no.reshape(nl, b, h, dk), mo.reshape(nl, b, h)


def _merge_masks(row_i, col_i, L):
    masks = []
    s = 1
    while s < L:
        masks.append(((row_i // (2 * s)) == (col_i // (2 * s))) & ((row_i // s) > (col_i // s)))
        s *= 2
    return masks


def _unit_lower_inverse(a_strict, eye, masks, mm):
    x = jnp.where(eye, 1.0, 0.0) - jnp.where(masks[0], a_strict, 0.0)
    for mask in masks[1:]:
        e = jnp.where(mask, a_strict, 0.0)
        x = x - mm(x, mm(e, x))
    return x


def _vpu_mm(a, b):
    out = a[:, 0:1] * b[0:1, :]
    for kk in range(1, a.shape[1]):
        out = out + a[:, kk:kk + 1] * b[kk:kk + 1, :]
    return out


def _bmm(a, b):
    return jnp.einsum("gik,gkj->gij", a.astype(BF16), b.astype(BF16), preferred_element_type=F32)


def _bmm_nt(a, b):
    return jnp.einsum("gik,gjk->gij", a.astype(BF16), b.astype(BF16), preferred_element_type=F32)


def _bmm_tn(a, b):
    return jnp.einsum("gki,gkj->gij", a.astype(BF16), b.astype(BF16), preferred_element_type=F32)


def _gdn_dims(conv_w_all, a_log_all, gain_all):
    dk = dv = gain_all.shape[-1]
    hv = a_log_all.shape[-1]
    cdim = conv_w_all.shape[-1]
    hk = (cdim - hv * dv) // (2 * dk)
    rep = hv // hk
    assert 2 * hv <= 128 and rep * hk == hv
    return dk, dv, hv, hk, rep, cdim


def _gdn_step_body(alv_ref, dtv_ref, qx_ref, kx_ref, vx_ref, z_ref, gt_ref, cwq_ref, cwk_ref, cwv_ref,
                   gain_ref, cq0_ref, ck0_ref, cv0_ref, s0_ref, y_ref, so_ref,
                   eq_scr, ek_scr, ev_scr, q_st, k_st, v_st, z_st, qs_st, ks_st, kw_st, u_st, y_st,
                   *, L, BB, HK, REP, DK, DV):
    HV = HK * REP
    R = HV * L
    taps = CONV_W - 1

    def conv(x_ref, c0_ref, scr, cw_ref, bb):
        scr[CONV_PAD - taps:CONV_PAD, :] = c0_ref[bb]
        scr[CONV_PAD:CONV_PAD + L, :] = x_ref[bb * L:(bb + 1) * L, :].astype(F32)
        acc = scr[pl.ds(CONV_PAD - taps, L), :] * cw_ref[0:1, :]
        for w in range(1, CONV_W):
            acc = acc + scr[pl.ds(CONV_PAD - taps + w, L), :] * cw_ref[w:w + 1, :]
        return _silu(acc)

    row_i = _iota2(R, R, 0)
    col_i = _iota2(R, R, 1)
    same_head = (row_i // L) == (col_i // L)
    eye = row_i == col_i
    incl = same_head & (row_i >= col_i)
    strict = same_head & (row_i > col_i)
    masks = [same_head & m for m in _merge_masks(row_i % L, col_i % L, L)]
    lane = _iota2(R, LANES, 1)
    head_of_row = _iota2(R, LANES, 0) // L
    sel_beta = lane == head_of_row
    sel_decay = lane == head_of_row + HV
    tril = (_iota2(L, L, 0) >= _iota2(L, L, 1)).astype(F32)

    def stack_cols(x, sel):
        x8 = jnp.concatenate([x] * (8 // L), axis=0)
        tiled = jnp.concatenate([x8] * (R // 8), axis=0)
        return jnp.sum(jnp.where(sel, tiled, 0.0), axis=1, keepdims=True)

    for bb in range(BB):
        rows = slice(bb * L, (bb + 1) * L)
        cq = conv(qx_ref, cq0_ref, eq_scr, cwq_ref, bb)
        ck = conv(kx_ref, ck0_ref, ek_scr, cwk_ref, bb)
        cv = conv(vx_ref, cv0_ref, ev_scr, cwv_ref, bb)
        for vh in range(HV):
            st = slice(vh * L, (vh + 1) * L)
            kh = vh // REP
            q_st[st, :] = cq[:, kh * DK:(kh + 1) * DK]
            k_st[st, :] = ck[:, kh * DK:(kh + 1) * DK]
            v_st[st, :] = cv[:, vh * DV:(vh + 1) * DV]
            z_st[st, :] = z_ref[rows, vh * DV:(vh + 1) * DV].astype(F32)
        q = q_st[...]
        k = k_st[...]
        q = q * lax.rsqrt(jnp.sum(q * q, axis=-1, keepdims=True) + EPS) * DK ** -0.5
        k = k * lax.rsqrt(jnp.sum(k * k, axis=-1, keepdims=True) + EPS)
        q_st[...] = q
        k_st[...] = k

        gt = gt_ref[rows, :]
        beta = stack_cols(_sigmoid(gt), sel_beta)
        g = -jnp.exp(alv_ref[...]) * jax.nn.softplus(gt + dtv_ref[...])
        g_cum = _vpu_mm(tril, g)
        g_col = stack_cols(g_cum, sel_decay)
        g_last = stack_cols(jnp.broadcast_to(g_cum[L - 1:L, :], (L, LANES)), sel_decay)
        g_row = _col_to_row(g_col, eye)
        decay = jnp.exp(jnp.where(incl, g_col - g_row, NEG_INF))
        kq = _bdot_nt(jnp.concatenate([k, q], axis=0), k)
        kk, qk = kq[:R], kq[R:]
        a = jnp.where(strict, beta * kk * decay, 0.0)
        x = _unit_lower_inverse(a, eye, masks, _bdot)

        for vh in range(HV):
            st = slice(vh * L, (vh + 1) * L)
            qk_rows = jnp.concatenate([q_st[st, :], k_st[st, :]], axis=0)
            both = _bdot(qk_rows, s0_ref[bb, vh])
            qs_st[st, :] = both[:L]
            ks_st[st, :] = both[L:]
        eg = jnp.exp(g_col)
        rhs = beta * v_st[...] - (beta * eg) * ks_st[...]
        u = _bdot(x, rhs)
        o = eg * qs_st[...] + _bdot(qk * decay, u)
        y_st[...] = _head_norm_gate(o, gain_ref[...], z_st[...], False)
        kw_st[...] = k * jnp.exp(g_last - g_col)
        u_st[...] = u
        eg_last = jnp.exp(g_last)
        for vh in range(HV):
            st = slice(vh * L, (vh + 1) * L)
            y_ref[rows, vh * DV:(vh + 1) * DV] = y_st[st, :].astype(y_ref.dtype)
            so_ref[bb, vh] = (eg_last[vh * L:vh * L + 1, :] * s0_ref[bb, vh]
                              + _bdot_tn(kw_st[st, :], u_st[st, :]))


def _gdn_step(proj, gates, b, t, s0_all, conv0_all, j, conv_w_all, a_log_all, dt_bias_all, gain_all,
              BB, y_dtype):
    dk, dv, hv, hk, rep, cdim = _gdn_dims(conv_w_all, a_log_all, gain_all)
    L = t
    R = hv * L
    assert 8 % L == 0 and R % 8 == 0 and dk == dv
    hg = 1
    wq, wv = hk * dk, hv * dv
    k_blk = 1
    v_blk = (2 * hk * dk) // wv
    z_blk = cdim // wv
    rb = _row_block(1)
    lanes_of = lambda vec: jnp.zeros((1, LANES), F32).at[0, hv:2 * hv].set(vec.astype(F32))

    def cspec(width, off):
        return pl.BlockSpec((None, CONV_W, width), lambda bi, hi, ci: (j, 0, off + hi))

    def c0spec(width, off):
        return pl.BlockSpec((None, BB, CONV_W - 1, width), lambda bi, hi, ci: (j, bi, 0, off + hi))

    st_spec = pl.BlockSpec((None, BB, hv, dk, dv), lambda bi, hi, ci: (j, bi, hi, 0, 0))
    in_specs = [pl.BlockSpec((1, LANES), lambda bi, hi, ci: (0, 0)),
                pl.BlockSpec((1, LANES), lambda bi, hi, ci: (0, 0)),
                pl.BlockSpec((BB * L, wq), rb(lambda hi: hi)),
                pl.BlockSpec((BB * L, wq), rb(lambda hi: k_blk + hi)),
                pl.BlockSpec((BB * L, wv), rb(lambda hi: v_blk + hi)),
                pl.BlockSpec((BB * L, wv), rb(lambda hi: z_blk + hi)),
                pl.BlockSpec((BB * L, LANES), rb(lambda hi: 0)),
                cspec(wq, 0), cspec(wq, k_blk), cspec(wv, v_blk),
                pl.BlockSpec((None, 1, dv), lambda bi, hi, ci: (j, 0, 0)),
                c0spec(wq, 0), c0spec(wq, k_blk), c0spec(wv, v_blk), st_spec]
    args = [lanes_of(a_log_all[j]), lanes_of(dt_bias_all[j]), proj, proj, proj, proj, gates,
            conv_w_all, conv_w_all, conv_w_all, _rows3(gain_all),
            conv0_all, conv0_all, conv0_all, s0_all]
    stacked = [pltpu.VMEM((R, dk), F32)] * 9
    y, so = pl.pallas_call(
        functools.partial(_gdn_step_body, L=L, BB=BB, HK=hk, REP=rep, DK=dk, DV=dv),
        grid=(b // BB, hg, 1),
        in_specs=in_specs,
        out_specs=[pl.BlockSpec((BB * L, wv), rb(lambda hi: hi)), st_spec],
        out_shape=[jax.ShapeDtypeStruct((b * t, hv * dv), y_dtype),
                   jax.ShapeDtypeStruct((1, b, hv, dk, dv), F32)],
        scratch_shapes=[pltpu.VMEM((L + CONV_PAD, wq), F32), pltpu.VMEM((L + CONV_PAD, wq), F32),
                        pltpu.VMEM((L + CONV_PAD, wv), F32)] + stacked,
        compiler_params=_params(("arbitrary", "arbitrary", "arbitrary"), VMEM_LIMIT_MIXER_MIB),
        name="gdn_step",
    )(*args)
    return y, so


def _gdn_seq_body(al_ref, dt_ref, qx_ref, kx_ref, vx_ref, z_ref, gt_ref, cwq_ref, cwk_ref, cwv_ref,
                  gain_ref, y_ref, so_ref, s_scr, eq_scr, ek_scr, ev_scr,
                  *, TB, C, HBK, NTB, DK, DV, REP, HV):
    hb = pl.program_id(1)
    tb = pl.program_id(2)
    G = TB // C
    taps = CONV_W - 1

    @pl.when(tb == 0)
    def _():
        for scr in (eq_scr, ek_scr, ev_scr):
            scr[0:CONV_PAD, :] = jnp.zeros((CONV_PAD, scr.shape[1]), F32)
        s_scr[...] = jnp.zeros_like(s_scr)

    def conv(x_ref, scr, cw_ref):
        scr[CONV_PAD:CONV_PAD + TB, :] = x_ref[...].astype(F32)
        acc = scr[pl.ds(CONV_PAD - taps, TB), :] * cw_ref[0:1, :]
        for w in range(1, CONV_W):
            acc = acc + scr[pl.ds(CONV_PAD - taps + w, TB), :] * cw_ref[w:w + 1, :]
        if NTB > 1:
            scr[0:CONV_PAD, :] = scr[TB:TB + CONV_PAD, :]
        return _silu(acc)

    cq = conv(qx_ref, eq_scr, cwq_ref)
    ck = conv(kx_ref, ek_scr, cwk_ref)
    cv = conv(vx_ref, ev_scr, cwv_ref)

    row_i = _iota2(C, C, 0)
    col_i = _iota2(C, C, 1)
    eye = row_i == col_i
    incl = row_i >= col_i
    PW = REP * C
    prow = _iota2(C, PW, 0)
    plane = _iota2(C, PW, 1)
    pcol = plane % C
    phead = plane // C
    p_eye = prow == pcol
    p_incl = prow >= pcol
    p_strict = prow > pcol
    p_masks = _merge_masks(prow, pcol, C)
    bd_mask = (_iota2(PW, PW, 0) // C) == (_iota2(PW, PW, 1) // C)

    def block_diag(xp):
        return jnp.where(bd_mask, jnp.concatenate([xp] * REP, axis=1), 0.0)

    def packed_mm(ap, bp):
        return _bmm(ap, block_diag(bp))

    def pack_cols(cols):
        out = cols[0]
        for r in range(1, REP):
            out = jnp.where(phead >= r, cols[r], out)
        return out

    lane = _iota2(TB, 128, 1)
    gt = gt_ref[...]
    for kh in range(HBK):
        q = cq[:, kh * DK:(kh + 1) * DK]
        k = ck[:, kh * DK:(kh + 1) * DK]
        q = q * lax.rsqrt(jnp.sum(q * q, axis=-1, keepdims=True) + EPS) * DK ** -0.5
        k = k * lax.rsqrt(jnp.sum(k * k, axis=-1, keepdims=True) + EPS)
        q3 = q.reshape(G, C, DK)
        k3 = k.reshape(G, C, DK)
        kq = _bmm_nt(jnp.concatenate([k3, q3], axis=1), k3)
        kk, qk = kq[:, :C, :], kq[:, C:, :]
        betas, g_cols, g_rows = [], [], []
        for r in range(REP):
            head = (hb * HBK + kh) * REP + r
            betas.append(_sigmoid(_pick_lane(gt, lane, head)).reshape(G, C, 1))
            a_neg = -jnp.exp(jnp.full((1, 1), al_ref[head], F32))
            g = (a_neg * jax.nn.softplus(_pick_lane(gt, lane, HV + head) + dt_ref[head])).reshape(G, C, 1)
            g_lanes = jnp.sum(jnp.where(eye, g, 0.0), axis=1, keepdims=True)
            g_cols.append(jnp.sum(jnp.where(incl, g_lanes, 0.0), axis=2, keepdims=True))
            g_rows.append(jnp.sum(jnp.where(row_i <= col_i, g, 0.0), axis=1, keepdims=True))
        decay_p = jnp.exp(jnp.where(p_incl, pack_cols(g_cols) - jnp.concatenate(g_rows, axis=-1), NEG_INF))
        a_p = jnp.where(p_strict, pack_cols(betas) * jnp.concatenate([kk] * REP, axis=-1) * decay_p, 0.0)
        x_p = _unit_lower_inverse(a_p, p_eye, p_masks, packed_mm)
        for r in range(REP):
            vh = kh * REP + r
            hv = slice(vh * DV, (vh + 1) * DV)
            beta, g_col = betas[r], g_cols[r]
            x = x_p[:, :, r * C:(r + 1) * C]
            decay = decay_p[:, :, r * C:(r + 1) * C]
            v3 = cv[:, hv].reshape(G, C, DV)
            eg = jnp.exp(g_col)
            wu = _bmm(x, jnp.concatenate([(beta * eg) * k3, beta * v3], axis=-1))
            qo = _bmm(qk * decay, wu)
            o0 = qo[:, :, DK:]
            g_last = g_col[:, C - 1:C, :]
            mb = _bmm_tn(k3 * jnp.exp(g_last - g_col), wu)
            b_eff = mb[:, :, DK:]
            lhs = jnp.concatenate([eg * q3 - qo[:, :, :DK], mb[:, :, :DK]], axis=1).astype(BF16)
            eg_last = jnp.exp(g_last)
            s = s_scr[vh]
            outs = []
            for c in range(G):
                both = jnp.dot(lhs[c], s.astype(BF16), preferred_element_type=F32)
                outs.append(both[:C] + o0[c])
                s = eg_last[c] * s - both[C:] + b_eff[c]
            s_scr[vh] = s
            o = jnp.concatenate(outs, axis=0) if G > 1 else outs[0]
            y_ref[:, hv] = _head_norm_gate(o, gain_ref[...], z_ref[:, hv], False).astype(y_ref.dtype)

    @pl.when(tb == NTB - 1)
    def _():
        so_ref[...] = s_scr[...]


def _gdn_seq(proj, gates, b, t, j, conv_w_all, a_log_all, dt_bias_all, gain_all, TB, C, HBK, y_dtype):
    dk, dv, hv, hk, rep, cdim = _gdn_dims(conv_w_all, a_log_all, gain_all)
    assert dk == dv
    ntb = t // TB
    hg = hk // HBK
    wq, wv = HBK * dk, HBK * rep * dv
    k_blk = (hk * dk) // wq
    v_blk = (2 * hk * dk) // wv
    z_blk = cdim // wv
    rb = _row_block(ntb)

    def cspec(width, off):
        return pl.BlockSpec((None, CONV_W, width), lambda bi, hi, ti: (j, 0, off + hi))

    in_specs = [pl.BlockSpec(memory_space=pltpu.SMEM),
                pl.BlockSpec(memory_space=pltpu.SMEM),
                pl.BlockSpec((TB, wq), rb(lambda hi: hi)),
                pl.BlockSpec((TB, wq), rb(lambda hi: k_blk + hi)),
                pl.BlockSpec((TB, wv), rb(lambda hi: v_blk + hi)),
                pl.BlockSpec((TB, wv), rb(lambda hi: z_blk + hi)),
                pl.BlockSpec((TB, LANES), rb(lambda hi: 0)),
                cspec(wq, 0), cspec(wq, k_blk), cspec(wv, v_blk),
                pl.BlockSpec((None, 1, dv), lambda bi, hi, ti: (j, 0, 0))]
    args = [a_log_all[j], dt_bias_all[j], proj, proj, proj, proj, gates,
            conv_w_all, conv_w_all, conv_w_all, _rows3(gain_all)]
    st_spec = pl.BlockSpec((None, None, HBK * rep, dk, dv), lambda bi, hi, ti: (0, bi, hi, 0, 0))
    return pl.pallas_call(
        functools.partial(_gdn_seq_body, TB=TB, C=C, HBK=HBK, NTB=ntb, DK=dk, DV=dv, REP=rep, HV=hv),
        grid=(b, hg, ntb),
        in_specs=in_specs,
        out_specs=[pl.BlockSpec((TB, wv), rb(lambda hi: hi)), st_spec],
        out_shape=[jax.ShapeDtypeStruct((b * t, hv * dv), y_dtype),
                   jax.ShapeDtypeStruct((1, b, hv, dk, dv), F32)],
        scratch_shapes=[pltpu.VMEM((HBK * rep, dk, dv), F32),
                        pltpu.VMEM((TB + CONV_PAD, wq), F32), pltpu.VMEM((TB + CONV_PAD, wq), F32),
                        pltpu.VMEM((TB + CONV_PAD, wv), F32)],
        compiler_params=_params(("arbitrary", "arbitrary", "arbitrary"), VMEM_LIMIT_MIXER_MIB),
        name="gdn_seq",
    )(*args)


def _rope_tables(pos, dk):
    half = dk // 2
    inv = ROPE_BASE ** (-jnp.arange(half, dtype=F32) / half)
    ang = pos.astype(F32)[:, None] * inv[None, :]
    return jnp.cos(ang), jnp.sin(ang)


def _trunk(x, p, states, pos, w, cfg):
    (norm_pre, norm_post, ple_proj, ple_gate, ret_w_in, ret_head_norm, ret_w_out,
     mlstm_w_in, mlstm_b_gate, mlstm_head_norm, mlstm_w_out,
     gdn_w_in, gdn_conv_w, gdn_a_log, gdn_dt_bias, gdn_head_norm, gdn_w_out) = w
    ret_s, ml_c, ml_n, ml_m, gdn_s, gdn_conv = states
    b, t, d = x.shape
    depth = norm_pre.shape[0]
    n_ret = ret_w_in.shape[0]
    m = b * t
    tm, ydt, pdt = cfg["tm"], cfg["y_dtype"], cfg["proj_dtype"]
    cos, sin = _rope_tables(pos, 256)
    r = x.reshape(m, d)
    p2 = p.reshape(depth, m, p.shape[-1])
    ret_out = None
    outs = {}
    keep = CONV_W - 1
    for i in range(depth):
        kind, j = i % 3, i // 3
        if kind == 0:
            proj, _, _ = _inproj(r, norm_pre, i, ret_w_in, j, tm, cfg["tn_in"], pdt)
            y, ret_out = _retention(proj, b, t, ret_s, j, cos, sin, ret_head_norm, ret_out, n_ret,
                                    cfg["ret_L"], cfg["ret_BB"], cfg["ret_HB"], ydt)
            w_out = ret_w_out
        elif kind == 1:
            proj, gates, _ = _inproj(r, norm_pre, i, mlstm_w_in, j, tm, cfg["tn_in"], pdt)
            y, outs["c"], outs["n"], outs["m"] = _mlstm(
                proj, gates, b, t, ml_c, ml_n, ml_m, j, mlstm_b_gate, mlstm_head_norm,
                cfg["ml_L"], cfg["ml_BB"], cfg["ml_HB"], ydt)
            w_out = mlstm_w_out
        else:
            cdim = gdn_conv_w.shape[-1]
            from_tail = pdt != F32
            assert t >= keep and (not from_tail or (t % tm == 0 and keep <= TAIL_ROWS))
            proj, gates, tail = _inproj(r, norm_pre, i, gdn_w_in, j, tm, cfg["tn_in"], pdt, want_tail=from_tail)
            if gdn_s is None:
                y, outs["gs"] = _gdn_seq(proj, gates, b, t, j, gdn_conv_w, gdn_a_log, gdn_dt_bias,
                                         gdn_head_norm, cfg["gdn_TB"], cfg["gdn_L"], cfg["gdn_HBK"], ydt)
            else:
                y, outs["gs"] = _gdn_step(proj, gates, b, t, gdn_s, gdn_conv, j, gdn_conv_w, gdn_a_log,
                                          gdn_dt_bias, gdn_head_norm, cfg["gdn_BB"], ydt)
            if from_tail:
                last_tiles = tail.reshape(b, t // tm, TAIL_ROWS, -1)[:, -1]
                outs["gc"] = last_tiles[:, TAIL_ROWS - keep:, :cdim][None]
            else:
                outs["gc"] = proj.reshape(b, t, -1)[:, t - keep:, :cdim][None]
            w_out = gdn_w_out
        mix = _outproj(y, w_out, j, tm, cfg["tn_out"])
        r = _post_ple(mix, r, norm_post, p2, ple_proj, ple_gate, i, cfg["tm_ple"], cfg["tn_ple"])
    return (r.reshape(b, t, d), ret_out, outs["c"], outs["n"], outs["m"], outs["gs"], outs["gc"])


_PROMPT_CFG = dict(tm=2048, tn_in=512, tn_out=512, tm_ple=512, tn_ple=1024, y_dtype=BF16, proj_dtype=BF16,
                   ret_L=256, ret_BB=1, ret_HB=2, ml_L=512, ml_BB=1, ml_HB=1,
                   gdn_TB=1024, gdn_L=64, gdn_HBK=1)
_SAMPLE_CFG = dict(tm=512, tn_in=1024, tn_out=512, tm_ple=512, tn_ple=512, y_dtype=F32, proj_dtype=F32,
                   ret_L=4, ret_BB=2, ret_HB=8, ml_L=4, ml_BB=2, ml_HB=8, gdn_BB=2)


def kernel(x_prompt, x_sample, state_ret_S, state_mlstm_C, state_mlstm_n, state_mlstm_m, state_gdn_S, state_gdn_conv, p_prompt, p_sample, norm_pre, norm_post, ple_proj, ple_gate, ret_w_in, ret_head_norm, ret_w_out, mlstm_w_in, mlstm_b_gate, mlstm_head_norm, mlstm_w_out, gdn_w_in, gdn_conv_w, gdn_a_log, gdn_dt_bias, gdn_head_norm, gdn_w_out):
    w = (norm_pre, norm_post, ple_proj, ple_gate, ret_w_in, ret_head_norm, ret_w_out,
         mlstm_w_in, mlstm_b_gate, mlstm_head_norm, mlstm_w_out,
         gdn_w_in, gdn_conv_w, gdn_a_log, gdn_dt_bias, gdn_head_norm, gdn_w_out)
    pos_p = jnp.arange(x_prompt.shape[1])
    yp, ret_p, mc_p, mn_p, mm_p, gs_p, gc_p = _trunk(
        x_prompt, p_prompt, (None,) * 6, pos_p, w, _PROMPT_CFG)
    pos_s = PAST_LEN + jnp.arange(x_sample.shape[1])
    ys, ret_s, mc_s, mn_s, mm_s, gs_s, gc_s = _trunk(
        x_sample, p_sample,
        (state_ret_S, state_mlstm_C, state_mlstm_n, state_mlstm_m, state_gdn_S, state_gdn_conv),
        pos_s, w, _SAMPLE_CFG)
    return (yp, ys, ret_p, mc_p, mn_p, mm_p, gs_p, gc_p, ret_s, mc_s, mn_s, mm_s, gs_s, gc_s)
```

```python
import functools

import jax
import jax.numpy as jnp
from jax import lax
from jax.experimental import pallas as pl
from jax.experimental.pallas import tpu as pltpu

F32 = jnp.float32
BF16 = jnp.bfloat16
EPS = 1e-6
ROPE_BASE = 10000.0
CONV_W = 4
PAST_LEN = 16384
MIB = 1024 * 1024
NEG_INF = float("-inf")
CONV_PAD = 8
LANES = 128
TAIL_ROWS = 8
VMEM_LIMIT_MATMUL_MIB = 58
VMEM_LIMIT_MIXER_MIB = 48


def _params(sem, vmem_mib):
    return pltpu.CompilerParams(dimension_semantics=sem, vmem_limit_bytes=vmem_mib * MIB)


def _bdot(a, b):
    return jnp.dot(a.astype(BF16), b.astype(BF16), preferred_element_type=F32)


def _bdot_nt(a, b):
    return lax.dot_general(a.astype(BF16), b.astype(BF16), (((1,), (1,)), ((), ())),
                           preferred_element_type=F32)


def _bdot_tn(a, b):
    return lax.dot_general(a.astype(BF16), b.astype(BF16), (((0,), (0,)), ((), ())),
                           preferred_element_type=F32)


def _sigmoid(x):
    return 0.5 * jnp.tanh(0.5 * x) + 0.5


def _silu(x):
    h = 0.5 * x
    return h * jnp.tanh(h) + h


def _rows3(table):
    return table.reshape(table.shape[0], 1, table.shape[1])


SLAB = 256


def _row_slabs(tm):
    step = min(SLAB, tm)
    return [slice(s, s + step) for s in range(0, tm, step)]


def _once_per_row_tile(block_shape, index_map):
    return pl.BlockSpec(block_shape, index_map, pipeline_mode=pl.Buffered(1))


def _inproj_body(x_ref, g_ref, w_ref, *rest, w_is_nk, n_main, gate_cols, has_tail):
    has_gates = gate_cols > 0
    outs = list(rest[:1 + has_gates + has_tail])
    h_ref = rest[-1]
    o_ref = outs.pop(0)
    gates_ref = outs.pop(0) if has_gates else None
    tail_ref = outs.pop(0) if has_tail else None
    c = pl.program_id(1)

    @pl.when(c == 0)
    def _():
        for rows in _row_slabs(x_ref.shape[0]):
            x = x_ref[rows, :]
            ms = jnp.mean(x * x, axis=-1, keepdims=True)
            h_ref[rows, :] = (x * lax.rsqrt(ms + EPS) * g_ref[...]).astype(BF16)

    w = w_ref[...].astype(BF16)
    contract = (((1,), (1,)), ((), ())) if w_is_nk else (((1,), (0,)), ((), ()))
    acc = lax.dot_general(h_ref[...], w, contract, preferred_element_type=F32)

    def write_main():
        o_ref[...] = acc.astype(o_ref.dtype)
        if has_tail:
            tail_ref[...] = acc[acc.shape[0] - TAIL_ROWS:, :]

    if has_gates:
        pl.when(c < n_main)(write_main)

        @pl.when(c == n_main)
        def _():
            lane = _iota2(acc.shape[0], LANES, 1)
            gates_ref[...] = jnp.where(lane < gate_cols, acc[:, :LANES], 0.0)
    else:
        write_main()


def _inproj(x2d, g_all, layer, w_all, j, tm, tn, out_dtype, want_tail=False):
    m, d = x2d.shape
    n = w_all.shape[-1]
    n_main = n // tn
    has_gates = n % tn != 0
    assert n - n_main * tn <= LANES
    w_is_nk = n % LANES != 0
    if w_is_nk:
        w_all = jnp.swapaxes(w_all, 1, 2)
        w_spec = pl.BlockSpec((None, tn, d), lambda i, c: (j, c, 0))
    else:
        w_spec = pl.BlockSpec((None, d, tn), lambda i, c: (j, 0, c))

    def main_col(c):
        return jnp.minimum(c, n_main - 1)

    out_specs = [pl.BlockSpec((tm, tn), lambda i, c: (i, main_col(c)))]
    out_shape = [jax.ShapeDtypeStruct((m, n_main * tn), out_dtype)]
    if has_gates:
        out_specs.append(pl.BlockSpec((tm, LANES), lambda i, c: (i, 0)))
        out_shape.append(jax.ShapeDtypeStruct((m, LANES), F32))
    if want_tail:
        out_specs.append(pl.BlockSpec((None, TAIL_ROWS, tn), lambda i, c: (i, 0, main_col(c))))
        out_shape.append(jax.ShapeDtypeStruct((m // tm, TAIL_ROWS, n_main * tn), F32))
    res = pl.pallas_call(
        functools.partial(_inproj_body, w_is_nk=w_is_nk, n_main=n_main, gate_cols=n - n_main * tn,
                          has_tail=want_tail),
        grid=(m // tm, n_main + has_gates),
        in_specs=[_once_per_row_tile((tm, d), lambda i, c: (i, 0)),
                  pl.BlockSpec((None, 1, d), lambda i, c: (layer, 0, 0)),
                  w_spec],
        out_specs=out_specs,
        out_shape=out_shape,
        scratch_shapes=[pltpu.VMEM((tm, d), BF16)],
        compiler_params=_params(("arbitrary", "arbitrary"), VMEM_LIMIT_MATMUL_MIB),
        name="inproj",
    )(x2d, _rows3(g_all), w_all)
    res = list(res)
    proj = res.pop(0)
    gates = res.pop(0) if has_gates else None
    tail = res.pop(0) if want_tail else None
    return proj, gates, tail


def _outproj_body(y_ref, w_ref, o_ref):
    o_ref[...] = jnp.dot(y_ref[...].astype(BF16), w_ref[...].astype(BF16), preferred_element_type=F32)


def _outproj(y2d, w_all, j, tm, tn):
    m, k = y2d.shape
    n = w_all.shape[-1]
    return pl.pallas_call(
        _outproj_body,
        grid=(m // tm, n // tn),
        in_specs=[_once_per_row_tile((tm, k), lambda i, c: (i, 0)),
                  pl.BlockSpec((None, k, tn), lambda i, c: (j, 0, c))],
        out_specs=pl.BlockSpec((tm, tn), lambda i, c: (i, c)),
        out_shape=jax.ShapeDtypeStruct((m, n), F32),
        compiler_params=_params(("arbitrary", "arbitrary"), VMEM_LIMIT_MATMUL_MIB),
        name="outproj",
    )(y2d, w_all)


def _post_ple_body(mix_ref, r_ref, g_ref, p_ref, proj_ref, gate_ref, o_ref, rs_ref, r1b_ref, *, tn):
    c = pl.program_id(1)

    @pl.when(c == 0)
    def _():
        for rows in _row_slabs(mix_ref.shape[0]):
            mix = mix_ref[rows, :]
            rs = lax.rsqrt(jnp.mean(mix * mix, axis=-1, keepdims=True) + EPS)
            rs_ref[rows, :] = rs
            r1b_ref[rows, :] = (r_ref[rows, :] + mix * rs * g_ref[...]).astype(BF16)

    gate = jnp.dot(r1b_ref[...], gate_ref[...].astype(BF16), preferred_element_type=F32)
    emb = _bdot(p_ref[...], proj_ref[...])
    cols = pl.ds(pl.multiple_of(c * tn, tn), tn)
    r1 = r_ref[:, cols] + mix_ref[:, cols] * rs_ref[...] * g_ref[:, cols]
    o_ref[...] = r1 + emb * _sigmoid(gate)


def _post_ple(mix, r, g_all, p_all, proj_all, gate_all, layer, tm, tn):
    m, d = r.shape
    pd = p_all.shape[-1]
    return pl.pallas_call(
        functools.partial(_post_ple_body, tn=tn),
        grid=(m // tm, d // tn),
        in_specs=[pl.BlockSpec((tm, d), lambda i, c: (i, 0)),
                  pl.BlockSpec((tm, d), lambda i, c: (i, 0)),
                  pl.BlockSpec((None, 1, d), lambda i, c: (layer, 0, 0)),
                  pl.BlockSpec((None, tm, pd), lambda i, c: (layer, i, 0)),
                  pl.BlockSpec((None, pd, tn), lambda i, c: (layer, 0, c)),
                  pl.BlockSpec((None, d, tn), lambda i, c: (layer, 0, c))],
        out_specs=pl.BlockSpec((tm, tn), lambda i, c: (i, c)),
        out_shape=jax.ShapeDtypeStruct((m, d), F32),
        scratch_shapes=[pltpu.VMEM((tm, 1), F32), pltpu.VMEM((tm, d), BF16)],
        compiler_params=_params(("arbitrary", "arbitrary"), VMEM_LIMIT_MATMUL_MIB),
        name="post_ple",
    )(mix, r, _rows3(g_all), p_all, proj_all, gate_all)


def _iota2(n, m, dim):
    return lax.broadcasted_iota(jnp.int32, (n, m), dim)


def _col_to_row(col, eye):
    return jnp.sum(jnp.where(eye, col, 0.0), axis=0, keepdims=True)


def _cumsum_col_row(col, row_i, col_i, eye):
    row = _col_to_row(col, eye)
    c_col = jnp.sum(jnp.where(col_i <= row_i, row, 0.0), axis=1, keepdims=True)
    c_row = jnp.sum(jnp.where(row_i <= col_i, col, 0.0), axis=0, keepdims=True)
    return c_col, c_row


def _pick_lane(blk, lane_iota, idx):
    return jnp.sum(jnp.where(lane_iota == idx, blk, 0.0), axis=1, keepdims=True)


def _head_norm_gate(o, gain, z, center):
    if center:
        o = o - jnp.mean(o, axis=-1, keepdims=True)
    y = o * lax.rsqrt(jnp.mean(o * o, axis=-1, keepdims=True) + EPS) * gain
    return y * _silu(z.astype(F32))


def _row_block(nc):
    return lambda col: (lambda bi, hi, ci: (bi * nc + ci, col(hi)))


class _State:
    def __init__(self, in_ref, out_ref, scr, nc):
        self.in_ref, self.out_ref, self.scr, self.nc = in_ref, out_ref, scr, nc

    def start(self, chunk):
        if self.nc > 1:
            @pl.when(chunk == 0)
            def _():
                if self.in_ref is None:
                    self.scr[...] = jnp.zeros_like(self.scr)
                else:
                    self.scr[...] = self.in_ref[...]

    def get(self, bb, hh):
        if self.nc > 1:
            return self.scr[bb, hh]
        if self.in_ref is None:
            return jnp.zeros(self.out_ref.shape[2:], F32)
        return self.in_ref[bb, hh]

    def put(self, bb, hh, val):
        if self.nc > 1:
            self.scr[bb, hh] = val
        else:
            self.out_ref[bb, hh] = val

    def finish(self, chunk):
        if self.nc > 1:
            @pl.when(chunk == self.nc - 1)
            def _():
                self.out_ref[...] = self.scr[...]


def _ret_body(*refs, L, BB, HB, NC, DK, DV, has_s0, has_prev):
    lg_ref, q_ref, k_ref, v_ref, z_ref, cos_ref, sin_ref, gain_ref = refs[:8]
    s0_ref = refs[8] if has_s0 else None
    rest = refs[8 + has_s0 + has_prev:]
    y_ref, so_ref = rest[:2]
    st = _State(s0_ref, so_ref, rest[2] if NC > 1 else None, NC)
    hb = pl.program_id(1)
    c = pl.program_id(2)
    st.start(c)

    cos = cos_ref[...]
    sin = sin_ref[...]
    half = DK // 2

    def rot(x):
        x1, x2 = x[:, :half], x[:, half:]
        return jnp.concatenate([x1 * cos - x2 * sin, x2 * cos + x1 * sin], axis=-1)

    row_i = _iota2(L, L, 0)
    col_i = _iota2(L, L, 1)
    rel = (row_i - col_i).astype(F32)
    idx = _iota2(L, 1, 0).astype(F32)
    for hh in range(HB):
        lg = lg_ref[hb * HB + hh]
        decay = jnp.exp(jnp.where(rel >= 0, lg * rel, NEG_INF))
        w_in = jnp.exp(lg * (idx + 1.0))
        w_out = jnp.exp(lg * (L - 1.0 - idx))
        w_all = jnp.exp(jnp.full((1, 1), L, F32) * lg)
        hq = slice(hh * DK, (hh + 1) * DK)
        hv = slice(hh * DV, (hh + 1) * DV)
        for bb in range(BB):
            rows = slice(bb * L, (bb + 1) * L)
            q = rot(q_ref[rows, hq].astype(F32))
            k = rot(k_ref[rows, hq].astype(F32)) * DK ** -0.5
            v = v_ref[rows, hv]
            s = st.get(bb, hh)
            scores = _bdot_nt(q, k) * decay
            o = _bdot(scores, v) + _bdot(q * w_in, s)
            st.put(bb, hh, w_all * s + _bdot_tn(k * w_out, v))
            y_ref[rows, hv] = _head_norm_gate(o, gain_ref[:, hv], z_ref[rows, hv], True).astype(y_ref.dtype)
    st.finish(c)


def _retention(proj, b, t, s0_all, j, cos, sin, gain_all, so_prev, n_layers, L, BB, HB, y_dtype):
    h, dk = 8, 256
    dv = gain_all.shape[-1] // h
    nc = t // L
    assert BB == 1 or nc == 1
    hg = h // HB
    rb = _row_block(nc)
    lg = jnp.log1p(-jnp.exp2(-5.0 - jnp.arange(h, dtype=F32)))
    has_s0 = s0_all is not None
    v_blk = (2 * h * dk) // (HB * dv)
    z_blk = (2 * h * dk + h * dv) // (HB * dv)
    in_specs = [pl.BlockSpec(memory_space=pltpu.SMEM),
                pl.BlockSpec((BB * L, HB * dk), rb(lambda hi: hi)),
                pl.BlockSpec((BB * L, HB * dk), rb(lambda hi: hg + hi)),
                pl.BlockSpec((BB * L, HB * dv), rb(lambda hi: v_blk + hi)),
                pl.BlockSpec((BB * L, HB * dv), rb(lambda hi: z_blk + hi)),
                pl.BlockSpec((L, dk // 2), lambda bi, hi, ci: (ci, 0)),
                pl.BlockSpec((L, dk // 2), lambda bi, hi, ci: (ci, 0)),
                pl.BlockSpec((None, 1, HB * dv), lambda bi, hi, ci: (j, 0, hi))]
    args = [lg, proj, proj, proj, proj, cos, sin, _rows3(gain_all)]
    st_spec = pl.BlockSpec((None, BB, HB, dk, dv), lambda bi, hi, ci: (j, bi, hi, 0, 0))
    if has_s0:
        in_specs.append(st_spec)
        args.append(s0_all)
    aliases = {}
    if so_prev is not None:
        in_specs.append(pl.BlockSpec(memory_space=pl.ANY))
        args.append(so_prev)
        aliases = {len(args) - 1: 1}
    return pl.pallas_call(
        functools.partial(_ret_body, L=L, BB=BB, HB=HB, NC=nc, DK=dk, DV=dv, has_s0=has_s0,
                          has_prev=so_prev is not None),
        grid=(b // BB, hg, nc),
        in_specs=in_specs,
        out_specs=[pl.BlockSpec((BB * L, HB * dv), rb(lambda hi: hi)), st_spec],
        out_shape=[jax.ShapeDtypeStruct((b * t, h * dv), y_dtype),
                   jax.ShapeDtypeStruct((n_layers, b, h, dk, dv), F32)],
        scratch_shapes=[pltpu.VMEM((BB, HB, dk, dv), F32)] if nc > 1 else [],
        input_output_aliases=aliases,
        compiler_params=_params(("arbitrary", "arbitrary", "arbitrary"), VMEM_LIMIT_MIXER_MIB),
        name="retention",
    )(*args)


def _mlstm_body(*refs, L, BB, HB, NC, DK, DV, H, has_s0):
    bg_ref, q_ref, k_ref, v_ref, og_ref, z_ref, gt_ref, gain_ref = refs[:8]
    c0_ref, n0_ref, m0_ref = refs[8:11] if has_s0 else (None, None, None)
    rest = refs[8 + 3 * has_s0:]
    y_ref, co_ref, no_ref, mo_ref = rest[:4]
    scr = rest[4:] if NC > 1 else (None, None, None)
    st_c = _State(c0_ref, co_ref, scr[0], NC)
    st_n = _State(n0_ref, no_ref, scr[1], NC)
    st_m = _State(m0_ref, mo_ref, scr[2], NC)
    hb = pl.program_id(1)
    c = pl.program_id(2)
    for st in (st_c, st_n, st_m):
        st.start(c)

    row_i = _iota2(L, L, 0)
    col_i = _iota2(L, L, 1)
    eye = row_i == col_i
    causal = row_i >= col_i
    lane = _iota2(L, 128, 1)
    for bb in range(BB):
        rows = slice(bb * L, (bb + 1) * L)
        gt = gt_ref[rows, :]
        for hh in range(HB):
            head = hb * HB + hh
            hq = slice(hh * DK, (hh + 1) * DK)
            hv = slice(hh * DV, (hh + 1) * DV)
            ig = _pick_lane(gt, lane, head) + bg_ref[head]
            fg = _pick_lane(gt, lane, H + head) + bg_ref[H + head]
            lf = jax.nn.log_sigmoid(fg)
            b_col, b_row = _cumsum_col_row(lf, row_i, col_i, eye)
            i_row = _col_to_row(ig, eye)
            q = q_ref[rows, hq].astype(F32) * DK ** -0.5
            k = k_ref[rows, hq].astype(F32)
            v = v_ref[rows, hv]
            cm = st_c.get(bb, hh)
            nv = st_n.get(bb, hh)
            m_prev = st_m.get(bb, hh)
            dlog = jnp.where(causal, b_col - b_row + i_row, NEG_INF)
            inter = b_col + m_prev
            mt = jnp.maximum(inter, jnp.max(dlog, axis=1, keepdims=True))
            s = _bdot_nt(q, k) * jnp.exp(dlog - mt)
            wi = jnp.exp(inter - mt)
            num = _bdot(s, v) + wi * _bdot(q, cm)
            den = jnp.sum(s, axis=1, keepdims=True) + wi * jnp.sum(q * nv, axis=1, keepdims=True)
            ht = num / jnp.maximum(jnp.abs(den), jnp.exp(-mt))
            m_new = mt[L - 1:L, :]
            b_last = b_col[L - 1:L, :]
            w_last = jnp.exp(b_last - b_col + ig - m_new)
            dec = jnp.exp(b_last + m_prev - m_new)
            kw = k * w_last
            st_c.put(bb, hh, dec * cm + _bdot_tn(kw, v))
            st_n.put(bb, hh, dec * nv + jnp.sum(kw, axis=0, keepdims=True))
            st_m.put(bb, hh, m_new)
            hcell = ht * _sigmoid(og_ref[rows, hv].astype(F32))
            y_ref[rows, hv] = _head_norm_gate(hcell, gain_ref[:, hv], z_ref[rows, hv], True).astype(y_ref.dtype)
    for st in (st_c, st_n, st_m):
        st.finish(c)


def _mlstm(proj, gates, b, t, c0_all, n0_all, m0_all, j, bgate_all, gain_all, L, BB, HB, y_dtype):
    h, dk = 8, 256
    dv = gain_all.shape[-1] // h
    nc = t // L
    assert BB == 1 or nc == 1
    hg = h // HB
    rb = _row_block(nc)
    has_s0 = c0_all is not None
    v_off = 2 * h * dk
    wv = HB * dv
    in_specs = [pl.BlockSpec(memory_space=pltpu.SMEM),
                pl.BlockSpec((BB * L, HB * dk), rb(lambda hi: hi)),
                pl.BlockSpec((BB * L, HB * dk), rb(lambda hi: hg + hi)),
                pl.BlockSpec((BB * L, wv), rb(lambda hi: v_off // wv + hi)),
                pl.BlockSpec((BB * L, wv), rb(lambda hi: (v_off + h * dv) // wv + hi)),
                pl.BlockSpec((BB * L, wv), rb(lambda hi: (v_off + 2 * h * dv) // wv + hi)),
                pl.BlockSpec((BB * L, LANES), rb(lambda hi: 0)),
                pl.BlockSpec((None, 1, wv), lambda bi, hi, ci: (j, 0, hi))]
    args = [bgate_all[j], proj, proj, proj, proj, proj, gates, _rows3(gain_all)]
    c_spec = pl.BlockSpec((None, BB, HB, dk, dv), lambda bi, hi, ci: (j, bi, hi, 0, 0))
    n_spec = pl.BlockSpec((None, BB, HB, 1, dk), lambda bi, hi, ci: (j, bi, hi, 0, 0))
    m_spec = pl.BlockSpec((None, BB, HB, 1, 1), lambda bi, hi, ci: (j, bi, hi, 0, 0))
    nl = 1
    if has_s0:
        assert c0_all.shape[0] == nl
        in_specs += [c_spec, n_spec, m_spec]
        args += [c0_all, n0_all.reshape(nl, b, h, 1, dk), m0_all.reshape(nl, b, h, 1, 1)]
    scratch = [pltpu.VMEM((BB, HB, dk, dv), F32), pltpu.VMEM((BB, HB, 1, dk), F32),
               pltpu.VMEM((BB, HB, 1, 1), F32)] if nc > 1 else []
    y, co, no, mo = pl.pallas_call(
        functools.partial(_mlstm_body, L=L, BB=BB, HB=HB, NC=nc, DK=dk, DV=dv, H=h, has_s0=has_s0),
        grid=(b // BB, hg, nc),
        in_specs=in_specs,
        out_specs=[pl.BlockSpec((BB * L, wv), rb(lambda hi: hi)), c_spec, n_spec, m_spec],
        out_shape=[jax.ShapeDtypeStruct((b * t, h * dv), y_dtype),
                   jax.ShapeDtypeStruct((nl, b, h, dk, dv), F32),
                   jax.ShapeDtypeStruct((nl, b, h, 1, dk), F32),
                   jax.ShapeDtypeStruct((nl, b, h, 1, 1), F32)],
        scratch_shapes=scratch,
        compiler_params=_params(("arbitrary", "arbitrary", "arbitrary"), VMEM_LIMIT_MIXER_MIB),
        name="mlstm",
    )(*args)
    return y, co, no.reshape(nl, b, h, dk), mo.reshape(nl, b, h)


def _merge_masks(row_i, col_i, L):
    masks = []
    s = 1
    while s < L:
        masks.append(((row_i // (2 * s)) == (col_i // (2 * s))) & ((row_i // s) > (col_i // s)))
        s *= 2
    return masks


def _unit_lower_inverse(a_strict, eye, masks, mm):
    x = jnp.where(eye, 1.0, 0.0) - jnp.where(masks[0], a_strict, 0.0)
    for mask in masks[1:]:
        e = jnp.where(mask, a_strict, 0.0)
        x = x - mm(x, mm(e, x))
    return x


def _vpu_mm(a, b):
    out = a[:, 0:1] * b[0:1, :]
    for kk in range(1, a.shape[1]):
        out = out + a[:, kk:kk + 1] * b[kk:kk + 1, :]
    return out


def _bmm(a, b):
    return jnp.einsum("gik,gkj->gij", a.astype(BF16), b.astype(BF16), preferred_element_type=F32)


def _bmm_nt(a, b):
    return jnp.einsum("gik,gjk->gij", a.astype(BF16), b.astype(BF16), preferred_element_type=F32)


def _bmm_tn(a, b):
    return jnp.einsum("gki,gkj->gij", a.astype(BF16), b.astype(BF16), preferred_element_type=F32)


def _gdn_dims(conv_w_all, a_log_all, gain_all):
    dk = dv = gain_all.shape[-1]
    hv = a_log_all.shape[-1]
    cdim = conv_w_all.shape[-1]
    hk = (cdim - hv * dv) // (2 * dk)
    rep = hv // hk
    assert 2 * hv <= 128 and rep * hk == hv
    return dk, dv, hv, hk, rep, cdim


def _gdn_step_body(alv_ref, dtv_ref, qx_ref, kx_ref, vx_ref, z_ref, gt_ref, cwq_ref, cwk_ref, cwv_ref,
                   gain_ref, cq0_ref, ck0_ref, cv0_ref, s0_ref, y_ref, so_ref,
                   eq_scr, ek_scr, ev_scr, q_st, k_st, v_st, z_st, qs_st, ks_st, kw_st, u_st, y_st,
                   *, L, BB, HK, REP, DK, DV):
    HV = HK * REP
    R = HV * L
    taps = CONV_W - 1

    def conv(x_ref, c0_ref, scr, cw_ref, bb):
        scr[CONV_PAD - taps:CONV_PAD, :] = c0_ref[bb]
        scr[CONV_PAD:CONV_PAD + L, :] = x_ref[bb * L:(bb + 1) * L, :].astype(F32)
        acc = scr[pl.ds(CONV_PAD - taps, L), :] * cw_ref[0:1, :]
        for w in range(1, CONV_W):
            acc = acc + scr[pl.ds(CONV_PAD - taps + w, L), :] * cw_ref[w:w + 1, :]
        return _silu(acc)

    row_i = _iota2(R, R, 0)
    col_i = _iota2(R, R, 1)
    same_head = (row_i // L) == (col_i // L)
    eye = row_i == col_i
    incl = same_head & (row_i >= col_i)
    strict = same_head & (row_i > col_i)
    masks = [same_head & m for m in _merge_masks(row_i % L, col_i % L, L)]
    lane = _iota2(R, LANES, 1)
    head_of_row = _iota2(R, LANES, 0) // L
    sel_beta = lane == head_of_row
    sel_decay = lane == head_of_row + HV
    tril = (_iota2(L, L, 0) >= _iota2(L, L, 1)).astype(F32)

    def stack_cols(x, sel):
        x8 = jnp.concatenate([x] * (8 // L), axis=0)
        tiled = jnp.concatenate([x8] * (R // 8), axis=0)
        return jnp.sum(jnp.where(sel, tiled, 0.0), axis=1, keepdims=True)

    for bb in range(BB):
        rows = slice(bb * L, (bb + 1) * L)
        cq = conv(qx_ref, cq0_ref, eq_scr, cwq_ref, bb)
        ck = conv(kx_ref, ck0_ref, ek_scr, cwk_ref, bb)
        cv = conv(vx_ref, cv0_ref, ev_scr, cwv_ref, bb)
        for vh in range(HV):
            st = slice(vh * L, (vh + 1) * L)
            kh = vh // REP
            q_st[st, :] = cq[:, kh * DK:(kh + 1) * DK]
            k_st[st, :] = ck[:, kh * DK:(kh + 1) * DK]
            v_st[st, :] = cv[:, vh * DV:(vh + 1) * DV]
            z_st[st, :] = z_ref[rows, vh * DV:(vh + 1) * DV].astype(F32)
        q = q_st[...]
        k = k_st[...]
        q = q * lax.rsqrt(jnp.sum(q * q, axis=-1, keepdims=True) + EPS) * DK ** -0.5
        k = k * lax.rsqrt(jnp.sum(k * k, axis=-1, keepdims=True) + EPS)
        q_st[...] = q
        k_st[...] = k

        gt = gt_ref[rows, :]
        beta = stack_cols(_sigmoid(gt), sel_beta)
        g = -jnp.exp(alv_ref[...]) * jax.nn.softplus(gt + dtv_ref[...])
        g_cum = _vpu_mm(tril, g)
        g_col = stack_cols(g_cum, sel_decay)
        g_last = stack_cols(jnp.broadcast_to(g_cum[L - 1:L, :], (L, LANES)), sel_decay)
        g_row = _col_to_row(g_col, eye)
        decay = jnp.exp(jnp.where(incl, g_col - g_row, NEG_INF))
        kq = _bdot_nt(jnp.concatenate([k, q], axis=0), k)
        kk, qk = kq[:R], kq[R:]
        a = jnp.where(strict, beta * kk * decay, 0.0)
        x = _unit_lower_inverse(a, eye, masks, _bdot)

        for vh in range(HV):
            st = slice(vh * L, (vh + 1) * L)
            qk_rows = jnp.concatenate([q_st[st, :], k_st[st, :]], axis=0)
            both = _bdot(qk_rows, s0_ref[bb, vh])
            qs_st[st, :] = both[:L]
            ks_st[st, :] = both[L:]
        eg = jnp.exp(g_col)
        rhs = beta * v_st[...] - (beta * eg) * ks_st[...]
        u = _bdot(x, rhs)
        o = eg * qs_st[...] + _bdot(qk * decay, u)
        y_st[...] = _head_norm_gate(o, gain_ref[...], z_st[...], False)
        kw_st[...] = k * jnp.exp(g_last - g_col)
        u_st[...] = u
        eg_last = jnp.exp(g_last)
        for vh in range(HV):
            st = slice(vh * L, (vh + 1) * L)
            y_ref[rows, vh * DV:(vh + 1) * DV] = y_st[st, :].astype(y_ref.dtype)
            so_ref[bb, vh] = (eg_last[vh * L:vh * L + 1, :] * s0_ref[bb, vh]
                              + _bdot_tn(kw_st[st, :], u_st[st, :]))


def _gdn_step(proj, gates, b, t, s0_all, conv0_all, j, conv_w_all, a_log_all, dt_bias_all, gain_all,
              BB, y_dtype):
    dk, dv, hv, hk, rep, cdim = _gdn_dims(conv_w_all, a_log_all, gain_all)
    L = t
    R = hv * L
    assert 8 % L == 0 and R % 8 == 0 and dk == dv
    hg = 1
    wq, wv = hk * dk, hv * dv
    k_blk = 1
    v_blk = (2 * hk * dk) // wv
    z_blk = cdim // wv
    rb = _row_block(1)
    lanes_of = lambda vec: jnp.zeros((1, LANES), F32).at[0, hv:2 * hv].set(vec.astype(F32))

    def cspec(width, off):
        return pl.BlockSpec((None, CONV_W, width), lambda bi, hi, ci: (j, 0, off + hi))

    def c0spec(width, off):
        return pl.BlockSpec((None, BB, CONV_W - 1, width), lambda bi, hi, ci: (j, bi, 0, off + hi))

    st_spec = pl.BlockSpec((None, BB, hv, dk, dv), lambda bi, hi, ci: (j, bi, hi, 0, 0))
    in_specs = [pl.BlockSpec((1, LANES), lambda bi, hi, ci: (0, 0)),
                pl.BlockSpec((1, LANES), lambda bi, hi, ci: (0, 0)),
                pl.BlockSpec((BB * L, wq), rb(lambda hi: hi)),
                pl.BlockSpec((BB * L, wq), rb(lambda hi: k_blk + hi)),
                pl.BlockSpec((BB * L, wv), rb(lambda hi: v_blk + hi)),
                pl.BlockSpec((BB * L, wv), rb(lambda hi: z_blk + hi)),
                pl.BlockSpec((BB * L, LANES), rb(lambda hi: 0)),
                cspec(wq, 0), cspec(wq, k_blk), cspec(wv, v_blk),
                pl.BlockSpec((None, 1, dv), lambda bi, hi, ci: (j, 0, 0)),
                c0spec(wq, 0), c0spec(wq, k_blk), c0spec(wv, v_blk), st_spec]
    args = [lanes_of(a_log_all[j]), lanes_of(dt_bias_all[j]), proj, proj, proj, proj, gates,
            conv_w_all, conv_w_all, conv_w_all, _rows3(gain_all),
            conv0_all, conv0_all, conv0_all, s0_all]
    stacked = [pltpu.VMEM((R, dk), F32)] * 9
    y, so = pl.pallas_call(
        functools.partial(_gdn_step_body, L=L, BB=BB, HK=hk, REP=rep, DK=dk, DV=dv),
        grid=(b // BB, hg, 1),
        in_specs=in_specs,
        out_specs=[pl.BlockSpec((BB * L, wv), rb(lambda hi: hi)), st_spec],
        out_shape=[jax.ShapeDtypeStruct((b * t, hv * dv), y_dtype),
                   jax.ShapeDtypeStruct((1, b, hv, dk, dv), F32)],
        scratch_shapes=[pltpu.VMEM((L + CONV_PAD, wq), F32), pltpu.VMEM((L + CONV_PAD, wq), F32),
                        pltpu.VMEM((L + CONV_PAD, wv), F32)] + stacked,
        compiler_params=_params(("arbitrary", "arbitrary", "arbitrary"), VMEM_LIMIT_MIXER_MIB),
        name="gdn_step",
    )(*args)
    return y, so


def _gdn_seq_body(al_ref, dt_ref, qx_ref, kx_ref, vx_ref, z_ref, gt_ref, cwq_ref, cwk_ref, cwv_ref,
                  gain_ref, y_ref, so_ref, s_scr, eq_scr, ek_scr, ev_scr,
                  *, TB, C, HBK, NTB, DK, DV, REP, HV):
    hb = pl.program_id(1)
    tb = pl.program_id(2)
    G = TB // C
    taps = CONV_W - 1

    @pl.when(tb == 0)
    def _():
        for scr in (eq_scr, ek_scr, ev_scr):
            scr[0:CONV_PAD, :] = jnp.zeros((CONV_PAD, scr.shape[1]), F32)
        s_scr[...] = jnp.zeros_like(s_scr)

    def conv(x_ref, scr, cw_ref):
        scr[CONV_PAD:CONV_PAD + TB, :] = x_ref[...].astype(F32)
        acc = scr[pl.ds(CONV_PAD - taps, TB), :] * cw_ref[0:1, :]
        for w in range(1, CONV_W):
            acc = acc + scr[pl.ds(CONV_PAD - taps + w, TB), :] * cw_ref[w:w + 1, :]
        if NTB > 1:
            scr[0:CONV_PAD, :] = scr[TB:TB + CONV_PAD, :]
        return _silu(acc)

    cq = conv(qx_ref, eq_scr, cwq_ref)
    ck = conv(kx_ref, ek_scr, cwk_ref)
    cv = conv(vx_ref, ev_scr, cwv_ref)

    row_i = _iota2(C, C, 0)
    col_i = _iota2(C, C, 1)
    eye = row_i == col_i
    incl = row_i >= col_i
    PW = REP * C
    prow = _iota2(C, PW, 0)
    plane = _iota2(C, PW, 1)
    pcol = plane % C
    phead = plane // C
    p_eye = prow == pcol
    p_incl = prow >= pcol
    p_strict = prow > pcol
    p_masks = _merge_masks(prow, pcol, C)
    bd_mask = (_iota2(PW, PW, 0) // C) == (_iota2(PW, PW, 1) // C)

    def block_diag(xp):
        return jnp.where(bd_mask, jnp.concatenate([xp] * REP, axis=1), 0.0)

    def packed_mm(ap, bp):
        return _bmm(ap, block_diag(bp))

    def pack_cols(cols):
        out = cols[0]
        for r in range(1, REP):
            out = jnp.where(phead >= r, cols[r], out)
        return out

    lane = _iota2(TB, 128, 1)
    gt = gt_ref[...]
    for kh in range(HBK):
        q = cq[:, kh * DK:(kh + 1) * DK]
        k = ck[:, kh * DK:(kh + 1) * DK]
        q = q * lax.rsqrt(jnp.sum(q * q, axis=-1, keepdims=True) + EPS) * DK ** -0.5
        k = k * lax.rsqrt(jnp.sum(k * k, axis=-1, keepdims=True) + EPS)
        q3 = q.reshape(G, C, DK)
        k3 = k.reshape(G, C, DK)
        kq = _bmm_nt(jnp.concatenate([k3, q3], axis=1), k3)
        kk, qk = kq[:, :C, :], kq[:, C:, :]
        betas, g_cols, g_rows = [], [], []
        for r in range(REP):
            head = (hb * HBK + kh) * REP + r
            betas.append(_sigmoid(_pick_lane(gt, lane, head)).reshape(G, C, 1))
            a_neg = -jnp.exp(jnp.full((1, 1), al_ref[head], F32))
            g = (a_neg * jax.nn.softplus(_pick_lane(gt, lane, HV + head) + dt_ref[head])).reshape(G, C, 1)
            g_lanes = jnp.sum(jnp.where(eye, g, 0.0), axis=1, keepdims=True)
            g_cols.append(jnp.sum(jnp.where(incl, g_lanes, 0.0), axis=2, keepdims=True))
            g_rows.append(jnp.sum(jnp.where(row_i <= col_i, g, 0.0), axis=1, keepdims=True))
        decay_p = jnp.exp(jnp.where(p_incl, pack_cols(g_cols) - jnp.concatenate(g_rows, axis=-1), NEG_INF))
        a_p = jnp.where(p_strict, pack_cols(betas) * jnp.concatenate([kk] * REP, axis=-1) * decay_p, 0.0)
        x_p = _unit_lower_inverse(a_p, p_eye, p_masks, packed_mm)
        for r in range(REP):
            vh = kh * REP + r
            hv = slice(vh * DV, (vh + 1) * DV)
            beta, g_col = betas[r], g_cols[r]
            x = x_p[:, :, r * C:(r + 1) * C]
            decay = decay_p[:, :, r * C:(r + 1) * C]
            v3 = cv[:, hv].reshape(G, C, DV)
            eg = jnp.exp(g_col)
            wu = _bmm(x, jnp.concatenate([(beta * eg) * k3, beta * v3], axis=-1))
            qo = _bmm(qk * decay, wu)
            o0 = qo[:, :, DK:]
            g_last = g_col[:, C - 1:C, :]
            mb = _bmm_tn(k3 * jnp.exp(g_last - g_col), wu)
            b_eff = mb[:, :, DK:]
            lhs = jnp.concatenate([eg * q3 - qo[:, :, :DK], mb[:, :, :DK]], axis=1).astype(BF16)
            eg_last = jnp.exp(g_last)
            s = s_scr[vh]
            outs = []
            for c in range(G):
                both = jnp.dot(lhs[c], s.astype(BF16), preferred_element_type=F32)
                outs.append(both[:C] + o0[c])
                s = eg_last[c] * s - both[C:] + b_eff[c]
            s_scr[vh] = s
            o = jnp.concatenate(outs, axis=0) if G > 1 else outs[0]
            y_ref[:, hv] = _head_norm_gate(o, gain_ref[...], z_ref[:, hv], False).astype(y_ref.dtype)

    @pl.when(tb == NTB - 1)
    def _():
        so_ref[...] = s_scr[...]


def _gdn_seq(proj, gates, b, t, j, conv_w_all, a_log_all, dt_bias_all, gain_all, TB, C, HBK, y_dtype):
    dk, dv, hv, hk, rep, cdim = _gdn_dims(conv_w_all, a_log_all, gain_all)
    assert dk == dv
    ntb = t // TB
    hg = hk // HBK
    wq, wv = HBK * dk, HBK * rep * dv
    k_blk = (hk * dk) // wq
    v_blk = (2 * hk * dk) // wv
    z_blk = cdim // wv
    rb = _row_block(ntb)

    def cspec(width, off):
        return pl.BlockSpec((None, CONV_W, width), lambda bi, hi, ti: (j, 0, off + hi))

    in_specs = [pl.BlockSpec(memory_space=pltpu.SMEM),
                pl.BlockSpec(memory_space=pltpu.SMEM),
                pl.BlockSpec((TB, wq), rb(lambda hi: hi)),
                pl.BlockSpec((TB, wq), rb(lambda hi: k_blk + hi)),
                pl.BlockSpec((TB, wv), rb(lambda hi: v_blk + hi)),
                pl.BlockSpec((TB, wv), rb(lambda hi: z_blk + hi)),
                pl.BlockSpec((TB, LANES), rb(lambda hi: 0)),
                cspec(wq, 0), cspec(wq, k_blk), cspec(wv, v_blk),
                pl.BlockSpec((None, 1, dv), lambda bi, hi, ti: (j, 0, 0))]
    args = [a_log_all[j], dt_bias_all[j], proj, proj, proj, proj, gates,
            conv_w_all, conv_w_all, conv_w_all, _rows3(gain_all)]
    st_spec = pl.BlockSpec((None, None, HBK * rep, dk, dv), lambda bi, hi, ti: (0, bi, hi, 0, 0))
    return pl.pallas_call(
        functools.partial(_gdn_seq_body, TB=TB, C=C, HBK=HBK, NTB=ntb, DK=dk, DV=dv, REP=rep, HV=hv),
        grid=(b, hg, ntb),
        in_specs=in_specs,
        out_specs=[pl.BlockSpec((TB, wv), rb(lambda hi: hi)), st_spec],
        out_shape=[jax.ShapeDtypeStruct((b * t, hv * dv), y_dtype),
                   jax.ShapeDtypeStruct((1, b, hv, dk, dv), F32)],
        scratch_shapes=[pltpu.VMEM((HBK * rep, dk, dv), F32),
                        pltpu.VMEM((TB + CONV_PAD, wq), F32), pltpu.VMEM((TB + CONV_PAD, wq), F32),
                        pltpu.VMEM((TB + CONV_PAD, wv), F32)],
        compiler_params=_params(("arbitrary", "arbitrary", "arbitrary"), VMEM_LIMIT_MIXER_MIB),
        name="gdn_seq",
    )(*args)


def _rope_tables(pos, dk):
    half = dk // 2
    inv = ROPE_BASE ** (-jnp.arange(half, dtype=F32) / half)
    ang = pos.astype(F32)[:, None] * inv[None, :]
    return jnp.cos(ang), jnp.sin(ang)


def _trunk(x, p, states, pos, w, cfg):
    (norm_pre, norm_post, ple_proj, ple_gate, ret_w_in, ret_head_norm, ret_w_out,
     mlstm_w_in, mlstm_b_gate, mlstm_head_norm, mlstm_w_out,
     gdn_w_in, gdn_conv_w, gdn_a_log, gdn_dt_bias, gdn_head_norm, gdn_w_out) = w
    ret_s, ml_c, ml_n, ml_m, gdn_s, gdn_conv = states
    b, t, d = x.shape
    depth = norm_pre.shape[0]
    n_ret = ret_w_in.shape[0]
    m = b * t
    tm, ydt, pdt = cfg["tm"], cfg["y_dtype"], cfg["proj_dtype"]
    cos, sin = _rope_tables(pos, 256)
    r = x.reshape(m, d)
    p2 = p.reshape(depth, m, p.shape[-1])
    ret_out = None
    outs = {}
    keep = CONV_W - 1
    for i in range(depth):
        kind, j = i % 3, i // 3
        if kind == 0:
            proj, _, _ = _inproj(r, norm_pre, i, ret_w_in, j, tm, cfg["tn_in"], pdt)
            y, ret_out = _retention(proj, b, t, ret_s, j, cos, sin, ret_head_norm, ret_out, n_ret,
                                    cfg["ret_L"], cfg["ret_BB"], cfg["ret_HB"], ydt)
            w_out = ret_w_out
        elif kind == 1:
            proj, gates, _ = _inproj(r, norm_pre, i, mlstm_w_in, j, tm, cfg["tn_in"], pdt)
            y, outs["c"], outs["n"], outs["m"] = _mlstm(
                proj, gates, b, t, ml_c, ml_n, ml_m, j, mlstm_b_gate, mlstm_head_norm,
                cfg["ml_L"], cfg["ml_BB"], cfg["ml_HB"], ydt)
            w_out = mlstm_w_out
        else:
            cdim = gdn_conv_w.shape[-1]
            from_tail = pdt != F32
            assert t >= keep and (not from_tail or (t % tm == 0 and keep <= TAIL_ROWS))
            proj, gates, tail = _inproj(r, norm_pre, i, gdn_w_in, j, tm, cfg["tn_in"], pdt, want_tail=from_tail)
            if gdn_s is None:
                y, outs["gs"] = _gdn_seq(proj, gates, b, t, j, gdn_conv_w, gdn_a_log, gdn_dt_bias,
                                         gdn_head_norm, cfg["gdn_TB"], cfg["gdn_L"], cfg["gdn_HBK"], ydt)
            else:
                y, outs["gs"] = _gdn_step(proj, gates, b, t, gdn_s, gdn_conv, j, gdn_conv_w, gdn_a_log,
                                          gdn_dt_bias, gdn_head_norm, cfg["gdn_BB"], ydt)
            if from_tail:
                last_tiles = tail.reshape(b, t // tm, TAIL_ROWS, -1)[:, -1]
                outs["gc"] = last_tiles[:, TAIL_ROWS - keep:, :cdim][None]
            else:
                outs["gc"] = proj.reshape(b, t, -1)[:, t - keep:, :cdim][None]
            w_out = gdn_w_out
        mix = _outproj(y, w_out, j, tm, cfg["tn_out"])
        r = _post_ple(mix, r, norm_post, p2, ple_proj, ple_gate, i, cfg["tm_ple"], cfg["tn_ple"])
    return (r.reshape(b, t, d), ret_out, outs["c"], outs["n"], outs["m"], outs["gs"], outs["gc"])


_PROMPT_CFG = dict(tm=2048, tn_in=512, tn_out=512, tm_ple=1024, tn_ple=512, y_dtype=BF16, proj_dtype=BF16,
                   ret_L=256, ret_BB=1, ret_HB=2, ml_L=512, ml_BB=1, ml_HB=1,
                   gdn_TB=1024, gdn_L=64, gdn_HBK=1)
_SAMPLE_CFG = dict(tm=512, tn_in=1024, tn_out=512, tm_ple=512, tn_ple=512, y_dtype=F32, proj_dtype=F32,
                   ret_L=4, ret_BB=2, ret_HB=8, ml_L=4, ml_BB=2, ml_HB=8, gdn_BB=2)


def kernel(x_prompt, x_sample, state_ret_S, state_mlstm_C, state_mlstm_n, state_mlstm_m, state_gdn_S, state_gdn_conv, p_prompt, p_sample, norm_pre, norm_post, ple_proj, ple_gate, ret_w_in, ret_head_norm, ret_w_out, mlstm_w_in, mlstm_b_gate, mlstm_head_norm, mlstm_w_out, gdn_w_in, gdn_conv_w, gdn_a_log, gdn_dt_bias, gdn_head_norm, gdn_w_out):
    w = (norm_pre, norm_post, ple_proj, ple_gate, ret_w_in, ret_head_norm, ret_w_out,
         mlstm_w_in, mlstm_b_gate, mlstm_head_norm, mlstm_w_out,
         gdn_w_in, gdn_conv_w, gdn_a_log, gdn_dt_bias, gdn_head_norm, gdn_w_out)
    pos_p = jnp.arange(x_prompt.shape[1])
    yp, ret_p, mc_p, mn_p, mm_p, gs_p, gc_p = _trunk(
        x_prompt, p_prompt, (None,) * 6, pos_p, w, _PROMPT_CFG)
    pos_s = PAST_LEN + jnp.arange(x_sample.shape[1])
    ys, ret_s, mc_s, mn_s, mm_s, gs_s, gc_s = _trunk(
        x_sample, p_sample,
        (state_ret_S, state_mlstm_C, state_mlstm_n, state_mlstm_m, state_gdn_S, state_gdn_conv),
        pos_s, w, _SAMPLE_CFG)
    return (yp, ys, ret_p, mc_p, mn_p, mm_p, gs_p, gc_p, ret_s, mc_s, mn_s, mm_s, gs_s, gc_s)
```

```python
import functools

import jax
import jax.numpy as jnp
from jax import lax
from jax.experimental import pallas as pl
from jax.experimental.pallas import tpu as pltpu

F32 = jnp.float32
BF16 = jnp.bfloat16
EPS = 1e-6
ROPE_BASE = 10000.0
CONV_W = 4
PAST_LEN = 16384
QK_HEADS = 8
QK_HEAD_DIM = 256
MIB = 1024 * 1024
NEG_INF = float("-inf")
CONV_PAD = 8
LANES = 128
TAIL_ROWS = 8
VMEM_LIMIT_MATMUL_MIB = 58
VMEM_LIMIT_MIXER_MIB = 48


def _params(sem, vmem_mib):
    return pltpu.CompilerParams(dimension_semantics=sem, vmem_limit_bytes=vmem_mib * MIB)


def _bdot(a, b):
    return jnp.dot(a.astype(BF16), b.astype(BF16), preferred_element_type=F32)


def _bdot_nt(a, b):
    return lax.dot_general(a.astype(BF16), b.astype(BF16), (((1,), (1,)), ((), ())),
                           preferred_element_type=F32)


def _bdot_tn(a, b):
    return lax.dot_general(a.astype(BF16), b.astype(BF16), (((0,), (0,)), ((), ())),
                           preferred_element_type=F32)


def _sigmoid(x):
    return 0.5 * jnp.tanh(0.5 * x) + 0.5


def _silu(x):
    h = 0.5 * x
    return h * jnp.tanh(h) + h


def _rows3(table):
    return table.reshape(table.shape[0], 1, table.shape[1])


SLAB = 256


def _row_slabs(tm):
    step = min(SLAB, tm)
    return [slice(s, s + step) for s in range(0, tm, step)]


def _once_per_row_tile(block_shape, index_map):
    return pl.BlockSpec(block_shape, index_map, pipeline_mode=pl.Buffered(1))


def _inproj_body(x_ref, g_ref, w_ref, *rest, w_is_nk, n_main, gate_cols, has_tail):
    has_gates = gate_cols > 0
    outs = list(rest[:1 + has_gates + has_tail])
    h_ref = rest[-1]
    o_ref = outs.pop(0)
    gates_ref = outs.pop(0) if has_gates else None
    tail_ref = outs.pop(0) if has_tail else None
    c = pl.program_id(1)

    @pl.when(c == 0)
    def _():
        for rows in _row_slabs(x_ref.shape[0]):
            x = x_ref[rows, :]
            ms = jnp.mean(x * x, axis=-1, keepdims=True)
            h_ref[rows, :] = (x * lax.rsqrt(ms + EPS) * g_ref[...]).astype(BF16)

    w = w_ref[...].astype(BF16)
    contract = (((1,), (1,)), ((), ())) if w_is_nk else (((1,), (0,)), ((), ()))
    acc = lax.dot_general(h_ref[...], w, contract, preferred_element_type=F32)

    def write_main():
        o_ref[...] = acc.astype(o_ref.dtype)
        if has_tail:
            tail_ref[...] = acc[acc.shape[0] - TAIL_ROWS:, :]

    if has_gates:
        pl.when(c < n_main)(write_main)

        @pl.when(c == n_main)
        def _():
            lane = _iota2(acc.shape[0], LANES, 1)
            gates_ref[...] = jnp.where(lane < gate_cols, acc[:, :LANES], 0.0)
    else:
        write_main()


def _inproj(x2d, g_all, layer, w_all, j, tm, tn, out_dtype, want_tail=False):
    m, d = x2d.shape
    n = w_all.shape[-1]
    n_main = n // tn
    has_gates = n % tn != 0
    assert n - n_main * tn <= LANES
    w_is_nk = n % LANES != 0
    if w_is_nk:
        w_all = jnp.swapaxes(w_all, 1, 2)
        w_spec = pl.BlockSpec((None, tn, d), lambda i, c: (j, c, 0))
    else:
        w_spec = pl.BlockSpec((None, d, tn), lambda i, c: (j, 0, c))

    def main_col(c):
        return jnp.minimum(c, n_main - 1)

    out_specs = [pl.BlockSpec((tm, tn), lambda i, c: (i, main_col(c)))]
    out_shape = [jax.ShapeDtypeStruct((m, n_main * tn), out_dtype)]
    if has_gates:
        out_specs.append(pl.BlockSpec((tm, LANES), lambda i, c: (i, 0)))
        out_shape.append(jax.ShapeDtypeStruct((m, LANES), F32))
    if want_tail:
        out_specs.append(pl.BlockSpec((None, TAIL_ROWS, tn), lambda i, c: (i, 0, main_col(c))))
        out_shape.append(jax.ShapeDtypeStruct((m // tm, TAIL_ROWS, n_main * tn), F32))
    res = pl.pallas_call(
        functools.partial(_inproj_body, w_is_nk=w_is_nk, n_main=n_main, gate_cols=n - n_main * tn,
                          has_tail=want_tail),
        grid=(m // tm, n_main + has_gates),
        in_specs=[_once_per_row_tile((tm, d), lambda i, c: (i, 0)),
                  pl.BlockSpec((None, 1, d), lambda i, c: (layer, 0, 0)),
                  w_spec],
        out_specs=out_specs,
        out_shape=out_shape,
        scratch_shapes=[pltpu.VMEM((tm, d), BF16)],
        compiler_params=_params(("arbitrary", "arbitrary"), VMEM_LIMIT_MATMUL_MIB),
        name="inproj",
    )(x2d, _rows3(g_all), w_all)
    res = list(res)
    proj = res.pop(0)
    gates = res.pop(0) if has_gates else None
    tail = res.pop(0) if want_tail else None
    return proj, gates, tail


def _outproj_body(y_ref, w_ref, o_ref):
    o_ref[...] = jnp.dot(y_ref[...].astype(BF16), w_ref[...].astype(BF16), preferred_element_type=F32)


def _outproj(y2d, w_all, j, tm, tn):
    m, k = y2d.shape
    n = w_all.shape[-1]
    return pl.pallas_call(
        _outproj_body,
        grid=(m // tm, n // tn),
        in_specs=[_once_per_row_tile((tm, k), lambda i, c: (i, 0)),
                  pl.BlockSpec((None, k, tn), lambda i, c: (j, 0, c))],
        out_specs=pl.BlockSpec((tm, tn), lambda i, c: (i, c)),
        out_shape=jax.ShapeDtypeStruct((m, n), F32),
        compiler_params=_params(("arbitrary", "arbitrary"), VMEM_LIMIT_MATMUL_MIB),
        name="outproj",
    )(y2d, w_all)


def _post_ple_body(mix_ref, r_ref, g_ref, p_ref, proj_ref, gate_ref, o_ref, rs_ref, r1b_ref, *, tn):
    c = pl.program_id(1)

    @pl.when(c == 0)
    def _():
        for rows in _row_slabs(mix_ref.shape[0]):
            mix = mix_ref[rows, :]
            rs = lax.rsqrt(jnp.mean(mix * mix, axis=-1, keepdims=True) + EPS)
            rs_ref[rows, :] = rs
            r1b_ref[rows, :] = (r_ref[rows, :] + mix * rs * g_ref[...]).astype(BF16)

    gate = jnp.dot(r1b_ref[...], gate_ref[...].astype(BF16), preferred_element_type=F32)
    emb = _bdot(p_ref[...], proj_ref[...])
    cols = pl.ds(pl.multiple_of(c * tn, tn), tn)
    r1 = r_ref[:, cols] + mix_ref[:, cols] * rs_ref[...] * g_ref[:, cols]
    o_ref[...] = r1 + emb * _sigmoid(gate)


def _post_ple(mix, r, g_all, p_all, proj_all, gate_all, layer, tm, tn):
    m, d = r.shape
    pd = p_all.shape[-1]
    return pl.pallas_call(
        functools.partial(_post_ple_body, tn=tn),
        grid=(m // tm, d // tn),
        in_specs=[pl.BlockSpec((tm, d), lambda i, c: (i, 0)),
                  pl.BlockSpec((tm, d), lambda i, c: (i, 0)),
                  pl.BlockSpec((None, 1, d), lambda i, c: (layer, 0, 0)),
                  pl.BlockSpec((None, tm, pd), lambda i, c: (layer, i, 0)),
                  pl.BlockSpec((None, pd, tn), lambda i, c: (layer, 0, c)),
                  pl.BlockSpec((None, d, tn), lambda i, c: (layer, 0, c))],
        out_specs=pl.BlockSpec((tm, tn), lambda i, c: (i, c)),
        out_shape=jax.ShapeDtypeStruct((m, d), F32),
        scratch_shapes=[pltpu.VMEM((tm, 1), F32), pltpu.VMEM((tm, d), BF16)],
        compiler_params=_params(("arbitrary", "arbitrary"), VMEM_LIMIT_MATMUL_MIB),
        name="post_ple",
    )(mix, r, _rows3(g_all), p_all, proj_all, gate_all)


def _iota2(n, m, dim):
    return lax.broadcasted_iota(jnp.int32, (n, m), dim)


def _col_to_row(col, eye):
    return jnp.sum(jnp.where(eye, col, 0.0), axis=0, keepdims=True)


def _cumsum_col_row(col, row_i, col_i, eye):
    row = _col_to_row(col, eye)
    c_col = jnp.sum(jnp.where(col_i <= row_i, row, 0.0), axis=1, keepdims=True)
    c_row = jnp.sum(jnp.where(row_i <= col_i, col, 0.0), axis=0, keepdims=True)
    return c_col, c_row


def _pick_lane(blk, lane_iota, idx):
    return jnp.sum(jnp.where(lane_iota == idx, blk, 0.0), axis=1, keepdims=True)


def _head_norm_gate(o, gain, z, center):
    if center:
        o = o - jnp.mean(o, axis=-1, keepdims=True)
    y = o * lax.rsqrt(jnp.mean(o * o, axis=-1, keepdims=True) + EPS) * gain
    return y * _silu(z.astype(F32))


def _row_block(nc):
    return lambda col: (lambda bi, hi, ci: (bi * nc + ci, col(hi)))


class _State:
    def __init__(self, in_ref, out_ref, scr, nc):
        self.in_ref, self.out_ref, self.scr, self.nc = in_ref, out_ref, scr, nc

    def start(self, chunk):
        if self.nc > 1:
            @pl.when(chunk == 0)
            def _():
                if self.in_ref is None:
                    self.scr[...] = jnp.zeros_like(self.scr)
                else:
                    self.scr[...] = self.in_ref[...]

    def get(self, bb, hh):
        if self.nc > 1:
            return self.scr[bb, hh]
        if self.in_ref is None:
            return jnp.zeros(self.out_ref.shape[2:], F32)
        return self.in_ref[bb, hh]

    def put(self, bb, hh, val):
        if self.nc > 1:
            self.scr[bb, hh] = val
        else:
            self.out_ref[bb, hh] = val

    def finish(self, chunk):
        if self.nc > 1:
            @pl.when(chunk == self.nc - 1)
            def _():
                self.out_ref[...] = self.scr[...]


def _ret_body(*refs, L, BB, HB, NC, DK, DV, has_s0, has_prev):
    lg_ref, q_ref, k_ref, v_ref, z_ref, cos_ref, sin_ref, gain_ref = refs[:8]
    s0_ref = refs[8] if has_s0 else None
    rest = refs[8 + has_s0 + has_prev:]
    y_ref, so_ref = rest[:2]
    st = _State(s0_ref, so_ref, rest[2] if NC > 1 else None, NC)
    hb = pl.program_id(1)
    c = pl.program_id(2)
    st.start(c)

    cos = cos_ref[...]
    sin = sin_ref[...]
    half = DK // 2

    def rot(x):
        x1, x2 = x[:, :half], x[:, half:]
        return jnp.concatenate([x1 * cos - x2 * sin, x2 * cos + x1 * sin], axis=-1)

    row_i = _iota2(L, L, 0)
    col_i = _iota2(L, L, 1)
    rel = (row_i - col_i).astype(F32)
    idx = _iota2(L, 1, 0).astype(F32)
    for hh in range(HB):
        lg = lg_ref[hb * HB + hh]
        decay = jnp.exp(jnp.where(rel >= 0, lg * rel, NEG_INF))
        w_in = jnp.exp(lg * (idx + 1.0))
        w_out = jnp.exp(lg * (L - 1.0 - idx))
        w_all = jnp.exp(jnp.full((1, 1), L, F32) * lg)
        hq = slice(hh * DK, (hh + 1) * DK)
        hv = slice(hh * DV, (hh + 1) * DV)
        for bb in range(BB):
            rows = slice(bb * L, (bb + 1) * L)
            q = rot(q_ref[rows, hq].astype(F32))
            k = rot(k_ref[rows, hq].astype(F32)) * DK ** -0.5
            v = v_ref[rows, hv]
            s = st.get(bb, hh)
            scores = _bdot_nt(q, k) * decay
            o = _bdot(scores, v) + _bdot(q * w_in, s)
            st.put(bb, hh, w_all * s + _bdot_tn(k * w_out, v))
            y_ref[rows, hv] = _head_norm_gate(o, gain_ref[:, hv], z_ref[rows, hv], True).astype(y_ref.dtype)
    st.finish(c)


def _retention(proj, b, t, s0_all, j, cos, sin, gain_all, so_prev, n_layers, L, BB, HB, y_dtype):
    h, dk = QK_HEADS, QK_HEAD_DIM
    dv = gain_all.shape[-1] // h
    nc = t // L
    assert BB == 1 or nc == 1
    hg = h // HB
    rb = _row_block(nc)
    lg = jnp.log1p(-jnp.exp2(-5.0 - jnp.arange(h, dtype=F32)))
    has_s0 = s0_all is not None
    v_blk = (2 * h * dk) // (HB * dv)
    z_blk = (2 * h * dk + h * dv) // (HB * dv)
    in_specs = [pl.BlockSpec(memory_space=pltpu.SMEM),
                pl.BlockSpec((BB * L, HB * dk), rb(lambda hi: hi)),
                pl.BlockSpec((BB * L, HB * dk), rb(lambda hi: hg + hi)),
                pl.BlockSpec((BB * L, HB * dv), rb(lambda hi: v_blk + hi)),
                pl.BlockSpec((BB * L, HB * dv), rb(lambda hi: z_blk + hi)),
                pl.BlockSpec((L, dk // 2), lambda bi, hi, ci: (ci, 0)),
                pl.BlockSpec((L, dk // 2), lambda bi, hi, ci: (ci, 0)),
                pl.BlockSpec((None, 1, HB * dv), lambda bi, hi, ci: (j, 0, hi))]
    args = [lg, proj, proj, proj, proj, cos, sin, _rows3(gain_all)]
    st_spec = pl.BlockSpec((None, BB, HB, dk, dv), lambda bi, hi, ci: (j, bi, hi, 0, 0))
    if has_s0:
        in_specs.append(st_spec)
        args.append(s0_all)
    aliases = {}
    if so_prev is not None:
        in_specs.append(pl.BlockSpec(memory_space=pl.ANY))
        args.append(so_prev)
        aliases = {len(args) - 1: 1}
    return pl.pallas_call(
        functools.partial(_ret_body, L=L, BB=BB, HB=HB, NC=nc, DK=dk, DV=dv, has_s0=has_s0,
                          has_prev=so_prev is not None),
        grid=(b // BB, hg, nc),
        in_specs=in_specs,
        out_specs=[pl.BlockSpec((BB * L, HB * dv), rb(lambda hi: hi)), st_spec],
        out_shape=[jax.ShapeDtypeStruct((b * t, h * dv), y_dtype),
                   jax.ShapeDtypeStruct((n_layers, b, h, dk, dv), F32)],
        scratch_shapes=[pltpu.VMEM((BB, HB, dk, dv), F32)] if nc > 1 else [],
        input_output_aliases=aliases,
        compiler_params=_params(("arbitrary", "arbitrary", "arbitrary"), VMEM_LIMIT_MIXER_MIB),
        name="retention",
    )(*args)


def _mlstm_body(*refs, L, BB, HB, NC, DK, DV, H, has_s0):
    bg_ref, q_ref, k_ref, v_ref, og_ref, z_ref, gt_ref, gain_ref = refs[:8]
    c0_ref, n0_ref, m0_ref = refs[8:11] if has_s0 else (None, None, None)
    rest = refs[8 + 3 * has_s0:]
    y_ref, co_ref, no_ref, mo_ref = rest[:4]
    scr = rest[4:] if NC > 1 else (None, None, None)
    st_c = _State(c0_ref, co_ref, scr[0], NC)
    st_n = _State(n0_ref, no_ref, scr[1], NC)
    st_m = _State(m0_ref, mo_ref, scr[2], NC)
    hb = pl.program_id(1)
    c = pl.program_id(2)
    for st in (st_c, st_n, st_m):
        st.start(c)

    row_i = _iota2(L, L, 0)
    col_i = _iota2(L, L, 1)
    eye = row_i == col_i
    causal = row_i >= col_i
    lane = _iota2(L, 128, 1)
    for bb in range(BB):
        rows = slice(bb * L, (bb + 1) * L)
        gt = gt_ref[rows, :]
        for hh in range(HB):
            head = hb * HB + hh
            hq = slice(hh * DK, (hh + 1) * DK)
            hv = slice(hh * DV, (hh + 1) * DV)
            ig = _pick_lane(gt, lane, head) + bg_ref[head]
            fg = _pick_lane(gt, lane, H + head) + bg_ref[H + head]
            lf = jax.nn.log_sigmoid(fg)
            b_col, b_row = _cumsum_col_row(lf, row_i, col_i, eye)
            i_row = _col_to_row(ig, eye)
            q = q_ref[rows, hq].astype(F32) * DK ** -0.5
            k = k_ref[rows, hq].astype(F32)
            v = v_ref[rows, hv]
            cm = st_c.get(bb, hh)
            nv = st_n.get(bb, hh)
            m_prev = st_m.get(bb, hh)
            dlog = jnp.where(causal, b_col - b_row + i_row, NEG_INF)
            inter = b_col + m_prev
            mt = jnp.maximum(inter, jnp.max(dlog, axis=1, keepdims=True))
            s = _bdot_nt(q, k) * jnp.exp(dlog - mt)
            wi = jnp.exp(inter - mt)
            num = _bdot(s, v) + wi * _bdot(q, cm)
            den = jnp.sum(s, axis=1, keepdims=True) + wi * jnp.sum(q * nv, axis=1, keepdims=True)
            ht = num / jnp.maximum(jnp.abs(den), jnp.exp(-mt))
            m_new = mt[L - 1:L, :]
            b_last = b_col[L - 1:L, :]
            w_last = jnp.exp(b_last - b_col + ig - m_new)
            dec = jnp.exp(b_last + m_prev - m_new)
            kw = k * w_last
            st_c.put(bb, hh, dec * cm + _bdot_tn(kw, v))
            st_n.put(bb, hh, dec * nv + jnp.sum(kw, axis=0, keepdims=True))
            st_m.put(bb, hh, m_new)
            hcell = ht * _sigmoid(og_ref[rows, hv].astype(F32))
            y_ref[rows, hv] = _head_norm_gate(hcell, gain_ref[:, hv], z_ref[rows, hv], True).astype(y_ref.dtype)
    for st in (st_c, st_n, st_m):
        st.finish(c)


def _mlstm(proj, gates, b, t, c0_all, n0_all, m0_all, j, bgate_all, gain_all, L, BB, HB, y_dtype):
    h, dk = QK_HEADS, QK_HEAD_DIM
    dv = gain_all.shape[-1] // h
    nc = t // L
    assert BB == 1 or nc == 1
    hg = h // HB
    rb = _row_block(nc)
    has_s0 = c0_all is not None
    v_off = 2 * h * dk
    wv = HB * dv
    in_specs = [pl.BlockSpec(memory_space=pltpu.SMEM),
                pl.BlockSpec((BB * L, HB * dk), rb(lambda hi: hi)),
                pl.BlockSpec((BB * L, HB * dk), rb(lambda hi: hg + hi)),
                pl.BlockSpec((BB * L, wv), rb(lambda hi: v_off // wv + hi)),
                pl.BlockSpec((BB * L, wv), rb(lambda hi: (v_off + h * dv) // wv + hi)),
                pl.BlockSpec((BB * L, wv), rb(lambda hi: (v_off + 2 * h * dv) // wv + hi)),
                pl.BlockSpec((BB * L, LANES), rb(lambda hi: 0)),
                pl.BlockSpec((None, 1, wv), lambda bi, hi, ci: (j, 0, hi))]
    args = [bgate_all[j], proj, proj, proj, proj, proj, gates, _rows3(gain_all)]
    c_spec = pl.BlockSpec((None, BB, HB, dk, dv), lambda bi, hi, ci: (j, bi, hi, 0, 0))
    n_spec = pl.BlockSpec((None, BB, HB, 1, dk), lambda bi, hi, ci: (j, bi, hi, 0, 0))
    m_spec = pl.BlockSpec((None, BB, HB, 1, 1), lambda bi, hi, ci: (j, bi, hi, 0, 0))
    nl = 1
    if has_s0:
        assert c0_all.shape[0] == nl
        in_specs += [c_spec, n_spec, m_spec]
        args += [c0_all, n0_all.reshape(nl, b, h, 1, dk), m0_all.reshape(nl, b, h, 1, 1)]
    scratch = [pltpu.VMEM((BB, HB, dk, dv), F32), pltpu.VMEM((BB, HB, 1, dk), F32),
               pltpu.VMEM((BB, HB, 1, 1), F32)] if nc > 1 else []
    y, co, no, mo = pl.pallas_call(
        functools.partial(_mlstm_body, L=L, BB=BB, HB=HB, NC=nc, DK=dk, DV=dv, H=h, has_s0=has_s0),
        grid=(b // BB, hg, nc),
        in_specs=in_specs,
        out_specs=[pl.BlockSpec((BB * L, wv), rb(lambda hi: hi)), c_spec, n_spec, m_spec],
        out_shape=[jax.ShapeDtypeStruct((b * t, h * dv), y_dtype),
                   jax.ShapeDtypeStruct((nl, b, h, dk, dv), F32),
                   jax.ShapeDtypeStruct((nl, b, h, 1, dk), F32),
                   jax.ShapeDtypeStruct((nl, b, h, 1, 1), F32)],
        scratch_shapes=scratch,
        compiler_params=_params(("arbitrary", "arbitrary", "arbitrary"), VMEM_LIMIT_MIXER_MIB),
        name="mlstm",
    )(*args)
    return y, co, no.reshape(nl, b, h, dk), mo.reshape(nl, b, h)


def _merge_masks(row_i, col_i, L):
    masks = []
    s = 1
    while s < L:
        masks.append(((row_i // (2 * s)) == (col_i // (2 * s))) & ((row_i // s) > (col_i // s)))
        s *= 2
    return masks


def _unit_lower_inverse(a_strict, eye, masks, mm):
    x = jnp.where(eye, 1.0, 0.0) - jnp.where(masks[0], a_strict, 0.0)
    for mask in masks[1:]:
        e = jnp.where(mask, a_strict, 0.0)
        x = x - mm(x, mm(e, x))
    return x


def _vpu_mm(a, b):
    out = a[:, 0:1] * b[0:1, :]
    for kk in range(1, a.shape[1]):
        out = out + a[:, kk:kk + 1] * b[kk:kk + 1, :]
    return out


def _bmm(a, b):
    return jnp.einsum("gik,gkj->gij", a.astype(BF16), b.astype(BF16), preferred_element_type=F32)


def _bmm_nt(a, b):
    return jnp.einsum("gik,gjk->gij", a.astype(BF16), b.astype(BF16), preferred_element_type=F32)


def _bmm_tn(a, b):
    return jnp.einsum("gki,gkj->gij", a.astype(BF16), b.astype(BF16), preferred_element_type=F32)


def _gdn_dims(conv_w_all, a_log_all, gain_all):
    dk = dv = gain_all.shape[-1]
    hv = a_log_all.shape[-1]
    cdim = conv_w_all.shape[-1]
    hk = (cdim - hv * dv) // (2 * dk)
    rep = hv // hk
    assert 2 * hv <= 128 and rep * hk == hv
    return dk, dv, hv, hk, rep, cdim


def _gdn_step_body(alv_ref, dtv_ref, qx_ref, kx_ref, vx_ref, z_ref, gt_ref, cwq_ref, cwk_ref, cwv_ref,
                   gain_ref, cq0_ref, ck0_ref, cv0_ref, s0_ref, y_ref, so_ref,
                   eq_scr, ek_scr, ev_scr, q_st, k_st, v_st, z_st, qs_st, ks_st, kw_st, u_st, y_st,
                   *, L, BB, HK, REP, DK, DV):
    HV = HK * REP
    R = HV * L
    taps = CONV_W - 1

    def conv(x_ref, c0_ref, scr, cw_ref, bb):
        scr[CONV_PAD - taps:CONV_PAD, :] = c0_ref[bb]
        scr[CONV_PAD:CONV_PAD + L, :] = x_ref[bb * L:(bb + 1) * L, :].astype(F32)
        acc = scr[pl.ds(CONV_PAD - taps, L), :] * cw_ref[0:1, :]
        for w in range(1, CONV_W):
            acc = acc + scr[pl.ds(CONV_PAD - taps + w, L), :] * cw_ref[w:w + 1, :]
        return _silu(acc)

    row_i = _iota2(R, R, 0)
    col_i = _iota2(R, R, 1)
    same_head = (row_i // L) == (col_i // L)
    eye = row_i == col_i
    incl = same_head & (row_i >= col_i)
    strict = same_head & (row_i > col_i)
    masks = [same_head & m for m in _merge_masks(row_i % L, col_i % L, L)]
    lane = _iota2(R, LANES, 1)
    head_of_row = _iota2(R, LANES, 0) // L
    sel_beta = lane == head_of_row
    sel_decay = lane == head_of_row + HV
    tril = (_iota2(L, L, 0) >= _iota2(L, L, 1)).astype(F32)

    def stack_cols(x, sel):
        x8 = jnp.concatenate([x] * (8 // L), axis=0)
        tiled = jnp.concatenate([x8] * (R // 8), axis=0)
        return jnp.sum(jnp.where(sel, tiled, 0.0), axis=1, keepdims=True)

    for bb in range(BB):
        rows = slice(bb * L, (bb + 1) * L)
        cq = conv(qx_ref, cq0_ref, eq_scr, cwq_ref, bb)
        ck = conv(kx_ref, ck0_ref, ek_scr, cwk_ref, bb)
        cv = conv(vx_ref, cv0_ref, ev_scr, cwv_ref, bb)
        for vh in range(HV):
            st = slice(vh * L, (vh + 1) * L)
            kh = vh // REP
            q_st[st, :] = cq[:, kh * DK:(kh + 1) * DK]
            k_st[st, :] = ck[:, kh * DK:(kh + 1) * DK]
            v_st[st, :] = cv[:, vh * DV:(vh + 1) * DV]
            z_st[st, :] = z_ref[rows, vh * DV:(vh + 1) * DV].astype(F32)
        q = q_st[...]
        k = k_st[...]
        q = q * lax.rsqrt(jnp.sum(q * q, axis=-1, keepdims=True) + EPS) * DK ** -0.5
        k = k * lax.rsqrt(jnp.sum(k * k, axis=-1, keepdims=True) + EPS)
        q_st[...] = q
        k_st[...] = k

        gt = gt_ref[rows, :]
        beta = stack_cols(_sigmoid(gt), sel_beta)
        g = -jnp.exp(alv_ref[...]) * jax.nn.softplus(gt + dtv_ref[...])
        g_cum = _vpu_mm(tril, g)
        g_col = stack_cols(g_cum, sel_decay)
        g_last = stack_cols(jnp.broadcast_to(g_cum[L - 1:L, :], (L, LANES)), sel_decay)
        g_row = _col_to_row(g_col, eye)
        decay = jnp.exp(jnp.where(incl, g_col - g_row, NEG_INF))
        kq = _bdot_nt(jnp.concatenate([k, q], axis=0), k)
        kk, qk = kq[:R], kq[R:]
        a = jnp.where(strict, beta * kk * decay, 0.0)
        x = _unit_lower_inverse(a, eye, masks, _bdot)

        for vh in range(HV):
            st = slice(vh * L, (vh + 1) * L)
            qk_rows = jnp.concatenate([q_st[st, :], k_st[st, :]], axis=0)
            both = _bdot(qk_rows, s0_ref[bb, vh])
            qs_st[st, :] = both[:L]
            ks_st[st, :] = both[L:]
        eg = jnp.exp(g_col)
        rhs = beta * v_st[...] - (beta * eg) * ks_st[...]
        u = _bdot(x, rhs)
        o = eg * qs_st[...] + _bdot(qk * decay, u)
        y_st[...] = _head_norm_gate(o, gain_ref[...], z_st[...], False)
        kw_st[...] = k * jnp.exp(g_last - g_col)
        u_st[...] = u
        eg_last = jnp.exp(g_last)
        for vh in range(HV):
            st = slice(vh * L, (vh + 1) * L)
            y_ref[rows, vh * DV:(vh + 1) * DV] = y_st[st, :].astype(y_ref.dtype)
            so_ref[bb, vh] = (eg_last[vh * L:vh * L + 1, :] * s0_ref[bb, vh]
                              + _bdot_tn(kw_st[st, :], u_st[st, :]))


def _gdn_step(proj, gates, b, t, s0_all, conv0_all, j, conv_w_all, a_log_all, dt_bias_all, gain_all,
              BB, y_dtype):
    dk, dv, hv, hk, rep, cdim = _gdn_dims(conv_w_all, a_log_all, gain_all)
    L = t
    R = hv * L
    assert 8 % L == 0 and R % 8 == 0 and dk == dv
    hg = 1
    wq, wv = hk * dk, hv * dv
    k_blk = 1
    v_blk = (2 * hk * dk) // wv
    z_blk = cdim // wv
    rb = _row_block(1)
    lanes_of = lambda vec: jnp.zeros((1, LANES), F32).at[0, hv:2 * hv].set(vec.astype(F32))

    def cspec(width, off):
        return pl.BlockSpec((None, CONV_W, width), lambda bi, hi, ci: (j, 0, off + hi))

    def c0spec(width, off):
        return pl.BlockSpec((None, BB, CONV_W - 1, width), lambda bi, hi, ci: (j, bi, 0, off + hi))

    st_spec = pl.BlockSpec((None, BB, hv, dk, dv), lambda bi, hi, ci: (j, bi, hi, 0, 0))
    in_specs = [pl.BlockSpec((1, LANES), lambda bi, hi, ci: (0, 0)),
                pl.BlockSpec((1, LANES), lambda bi, hi, ci: (0, 0)),
                pl.BlockSpec((BB * L, wq), rb(lambda hi: hi)),
                pl.BlockSpec((BB * L, wq), rb(lambda hi: k_blk + hi)),
                pl.BlockSpec((BB * L, wv), rb(lambda hi: v_blk + hi)),
                pl.BlockSpec((BB * L, wv), rb(lambda hi: z_blk + hi)),
                pl.BlockSpec((BB * L, LANES), rb(lambda hi: 0)),
                cspec(wq, 0), cspec(wq, k_blk), cspec(wv, v_blk),
                pl.BlockSpec((None, 1, dv), lambda bi, hi, ci: (j, 0, 0)),
                c0spec(wq, 0), c0spec(wq, k_blk), c0spec(wv, v_blk), st_spec]
    args = [lanes_of(a_log_all[j]), lanes_of(dt_bias_all[j]), proj, proj, proj, proj, gates,
            conv_w_all, conv_w_all, conv_w_all, _rows3(gain_all),
            conv0_all, conv0_all, conv0_all, s0_all]
    stacked = [pltpu.VMEM((R, dk), F32)] * 9
    y, so = pl.pallas_call(
        functools.partial(_gdn_step_body, L=L, BB=BB, HK=hk, REP=rep, DK=dk, DV=dv),
        grid=(b // BB, hg, 1),
        in_specs=in_specs,
        out_specs=[pl.BlockSpec((BB * L, wv), rb(lambda hi: hi)), st_spec],
        out_shape=[jax.ShapeDtypeStruct((b * t, hv * dv), y_dtype),
                   jax.ShapeDtypeStruct((1, b, hv, dk, dv), F32)],
        scratch_shapes=[pltpu.VMEM((L + CONV_PAD, wq), F32), pltpu.VMEM((L + CONV_PAD, wq), F32),
                        pltpu.VMEM((L + CONV_PAD, wv), F32)] + stacked,
        compiler_params=_params(("arbitrary", "arbitrary", "arbitrary"), VMEM_LIMIT_MIXER_MIB),
        name="gdn_step",
    )(*args)
    return y, so


def _gdn_seq_body(al_ref, dt_ref, qx_ref, kx_ref, vx_ref, z_ref, gt_ref, cwq_ref, cwk_ref, cwv_ref,
                  gain_ref, y_ref, so_ref, s_scr, eq_scr, ek_scr, ev_scr,
                  *, TB, C, HBK, NTB, DK, DV, REP, HV):
    hb = pl.program_id(1)
    tb = pl.program_id(2)
    G = TB // C
    taps = CONV_W - 1

    @pl.when(tb == 0)
    def _():
        for scr in (eq_scr, ek_scr, ev_scr):
            scr[0:CONV_PAD, :] = jnp.zeros((CONV_PAD, scr.shape[1]), F32)
        s_scr[...] = jnp.zeros_like(s_scr)

    def conv(x_ref, scr, cw_ref):
        scr[CONV_PAD:CONV_PAD + TB, :] = x_ref[...].astype(F32)
        acc = scr[pl.ds(CONV_PAD - taps, TB), :] * cw_ref[0:1, :]
        for w in range(1, CONV_W):
            acc = acc + scr[pl.ds(CONV_PAD - taps + w, TB), :] * cw_ref[w:w + 1, :]
        if NTB > 1:
            scr[0:CONV_PAD, :] = scr[TB:TB + CONV_PAD, :]
        return _silu(acc)

    cq = conv(qx_ref, eq_scr, cwq_ref)
    ck = conv(kx_ref, ek_scr, cwk_ref)
    cv = conv(vx_ref, ev_scr, cwv_ref)

    row_i = _iota2(C, C, 0)
    col_i = _iota2(C, C, 1)
    eye = row_i == col_i
    incl = row_i >= col_i
    PW = REP * C
    prow = _iota2(C, PW, 0)
    plane = _iota2(C, PW, 1)
    pcol = plane % C
    phead = plane // C
    p_eye = prow == pcol
    p_incl = prow >= pcol
    p_strict = prow > pcol
    p_masks = _merge_masks(prow, pcol, C)
    bd_mask = (_iota2(PW, PW, 0) // C) == (_iota2(PW, PW, 1) // C)

    def block_diag(xp):
        return jnp.where(bd_mask, jnp.concatenate([xp] * REP, axis=1), 0.0)

    def packed_mm(ap, bp):
        return _bmm(ap, block_diag(bp))

    def pack_cols(cols):
        out = cols[0]
        for r in range(1, REP):
            out = jnp.where(phead >= r, cols[r], out)
        return out

    lane = _iota2(TB, 128, 1)
    gt = gt_ref[...]
    for kh in range(HBK):
        q = cq[:, kh * DK:(kh + 1) * DK]
        k = ck[:, kh * DK:(kh + 1) * DK]
        q = q * lax.rsqrt(jnp.sum(q * q, axis=-1, keepdims=True) + EPS) * DK ** -0.5
        k = k * lax.rsqrt(jnp.sum(k * k, axis=-1, keepdims=True) + EPS)
        q3 = q.reshape(G, C, DK)
        k3 = k.reshape(G, C, DK)
        kq = _bmm_nt(jnp.concatenate([k3, q3], axis=1), k3)
        kk, qk = kq[:, :C, :], kq[:, C:, :]
        betas, g_cols, g_rows = [], [], []
        for r in range(REP):
            head = (hb * HBK + kh) * REP + r
            betas.append(_sigmoid(_pick_lane(gt, lane, head)).reshape(G, C, 1))
            a_neg = -jnp.exp(jnp.full((1, 1), al_ref[head], F32))
            g = (a_neg * jax.nn.softplus(_pick_lane(gt, lane, HV + head) + dt_ref[head])).reshape(G, C, 1)
            g_lanes = jnp.sum(jnp.where(eye, g, 0.0), axis=1, keepdims=True)
            g_cols.append(jnp.sum(jnp.where(incl, g_lanes, 0.0), axis=2, keepdims=True))
            g_rows.append(jnp.sum(jnp.where(row_i <= col_i, g, 0.0), axis=1, keepdims=True))
        decay_p = jnp.exp(jnp.where(p_incl, pack_cols(g_cols) - jnp.concatenate(g_rows, axis=-1), NEG_INF))
        a_p = jnp.where(p_strict, pack_cols(betas) * jnp.concatenate([kk] * REP, axis=-1) * decay_p, 0.0)
        x_p = _unit_lower_inverse(a_p, p_eye, p_masks, packed_mm)
        for r in range(REP):
            vh = kh * REP + r
            hv = slice(vh * DV, (vh + 1) * DV)
            beta, g_col = betas[r], g_cols[r]
            x = x_p[:, :, r * C:(r + 1) * C]
            decay = decay_p[:, :, r * C:(r + 1) * C]
            v3 = cv[:, hv].reshape(G, C, DV)
            eg = jnp.exp(g_col)
            wu = _bmm(x, jnp.concatenate([(beta * eg) * k3, beta * v3], axis=-1))
            qo = _bmm(qk * decay, wu)
            o0 = qo[:, :, DK:]
            g_last = g_col[:, C - 1:C, :]
            mb = _bmm_tn(k3 * jnp.exp(g_last - g_col), wu)
            b_eff = mb[:, :, DK:]
            lhs = jnp.concatenate([eg * q3 - qo[:, :, :DK], mb[:, :, :DK]], axis=1).astype(BF16)
            eg_last = jnp.exp(g_last)
            s = s_scr[vh]
            outs = []
            for c in range(G):
                both = jnp.dot(lhs[c], s.astype(BF16), preferred_element_type=F32)
                outs.append(both[:C] + o0[c])
                s = eg_last[c] * s - both[C:] + b_eff[c]
            s_scr[vh] = s
            o = jnp.concatenate(outs, axis=0) if G > 1 else outs[0]
            y_ref[:, hv] = _head_norm_gate(o, gain_ref[...], z_ref[:, hv], False).astype(y_ref.dtype)

    @pl.when(tb == NTB - 1)
    def _():
        so_ref[...] = s_scr[...]


def _gdn_seq(proj, gates, b, t, j, conv_w_all, a_log_all, dt_bias_all, gain_all, TB, C, HBK, y_dtype):
    dk, dv, hv, hk, rep, cdim = _gdn_dims(conv_w_all, a_log_all, gain_all)
    assert dk == dv
    ntb = t // TB
    hg = hk // HBK
    wq, wv = HBK * dk, HBK * rep * dv
    k_blk = (hk * dk) // wq
    v_blk = (2 * hk * dk) // wv
    z_blk = cdim // wv
    rb = _row_block(ntb)

    def cspec(width, off):
        return pl.BlockSpec((None, CONV_W, width), lambda bi, hi, ti: (j, 0, off + hi))

    in_specs = [pl.BlockSpec(memory_space=pltpu.SMEM),
                pl.BlockSpec(memory_space=pltpu.SMEM),
                pl.BlockSpec((TB, wq), rb(lambda hi: hi)),
                pl.BlockSpec((TB, wq), rb(lambda hi: k_blk + hi)),
                pl.BlockSpec((TB, wv), rb(lambda hi: v_blk + hi)),
                pl.BlockSpec((TB, wv), rb(lambda hi: z_blk + hi)),
                pl.BlockSpec((TB, LANES), rb(lambda hi: 0)),
                cspec(wq, 0), cspec(wq, k_blk), cspec(wv, v_blk),
                pl.BlockSpec((None, 1, dv), lambda bi, hi, ti: (j, 0, 0))]
    args = [a_log_all[j], dt_bias_all[j], proj, proj, proj, proj, gates,
            conv_w_all, conv_w_all, conv_w_all, _rows3(gain_all)]
    st_spec = pl.BlockSpec((None, None, HBK * rep, dk, dv), lambda bi, hi, ti: (0, bi, hi, 0, 0))
    return pl.pallas_call(
        functools.partial(_gdn_seq_body, TB=TB, C=C, HBK=HBK, NTB=ntb, DK=dk, DV=dv, REP=rep, HV=hv),
        grid=(b, hg, ntb),
        in_specs=in_specs,
        out_specs=[pl.BlockSpec((TB, wv), rb(lambda hi: hi)), st_spec],
        out_shape=[jax.ShapeDtypeStruct((b * t, hv * dv), y_dtype),
                   jax.ShapeDtypeStruct((1, b, hv, dk, dv), F32)],
        scratch_shapes=[pltpu.VMEM((HBK * rep, dk, dv), F32),
                        pltpu.VMEM((TB + CONV_PAD, wq), F32), pltpu.VMEM((TB + CONV_PAD, wq), F32),
                        pltpu.VMEM((TB + CONV_PAD, wv), F32)],
        compiler_params=_params(("arbitrary", "arbitrary", "arbitrary"), VMEM_LIMIT_MIXER_MIB),
        name="gdn_seq",
    )(*args)


def _rope_tables(pos, dk):
    half = dk // 2
    inv = ROPE_BASE ** (-jnp.arange(half, dtype=F32) / half)
    ang = pos.astype(F32)[:, None] * inv[None, :]
    return jnp.cos(ang), jnp.sin(ang)


def _trunk(x, p, states, pos, w, cfg):
    (norm_pre, norm_post, ple_proj, ple_gate, ret_w_in, ret_head_norm, ret_w_out,
     mlstm_w_in, mlstm_b_gate, mlstm_head_norm, mlstm_w_out,
     gdn_w_in, gdn_conv_w, gdn_a_log, gdn_dt_bias, gdn_head_norm, gdn_w_out) = w
    ret_s, ml_c, ml_n, ml_m, gdn_s, gdn_conv = states
    b, t, d = x.shape
    depth = norm_pre.shape[0]
    n_ret = ret_w_in.shape[0]
    m = b * t
    tm, ydt, pdt = cfg["tm"], cfg["y_dtype"], cfg["proj_dtype"]
    cos, sin = _rope_tables(pos, QK_HEAD_DIM)
    r = x.reshape(m, d)
    p2 = p.reshape(depth, m, p.shape[-1])
    ret_out = None
    outs = {}
    keep = CONV_W - 1
    for i in range(depth):
        kind, j = i % 3, i // 3
        if kind == 0:
            proj, _, _ = _inproj(r, norm_pre, i, ret_w_in, j, tm, cfg["tn_in"], pdt)
            y, ret_out = _retention(proj, b, t, ret_s, j, cos, sin, ret_head_norm, ret_out, n_ret,
                                    cfg["ret_L"], cfg["ret_BB"], cfg["ret_HB"], ydt)
            w_out = ret_w_out
        elif kind == 1:
            proj, gates, _ = _inproj(r, norm_pre, i, mlstm_w_in, j, tm, cfg["tn_in"], pdt)
            y, outs["c"], outs["n"], outs["m"] = _mlstm(
                proj, gates, b, t, ml_c, ml_n, ml_m, j, mlstm_b_gate, mlstm_head_norm,
                cfg["ml_L"], cfg["ml_BB"], cfg["ml_HB"], ydt)
            w_out = mlstm_w_out
        else:
            cdim = gdn_conv_w.shape[-1]
            from_tail = pdt != F32
            assert t >= keep and (not from_tail or (t % tm == 0 and keep <= TAIL_ROWS))
            proj, gates, tail = _inproj(r, norm_pre, i, gdn_w_in, j, tm, cfg["tn_in"], pdt, want_tail=from_tail)
            if gdn_s is None:
                y, outs["gs"] = _gdn_seq(proj, gates, b, t, j, gdn_conv_w, gdn_a_log, gdn_dt_bias,
                                         gdn_head_norm, cfg["gdn_TB"], cfg["gdn_L"], cfg["gdn_HBK"], ydt)
            else:
                y, outs["gs"] = _gdn_step(proj, gates, b, t, gdn_s, gdn_conv, j, gdn_conv_w, gdn_a_log,
                                          gdn_dt_bias, gdn_head_norm, cfg["gdn_BB"], ydt)
            if from_tail:
                last_tiles = tail.reshape(b, t // tm, TAIL_ROWS, -1)[:, -1]
                outs["gc"] = last_tiles[:, TAIL_ROWS - keep:, :cdim][None]
            else:
                outs["gc"] = proj.reshape(b, t, -1)[:, t - keep:, :cdim][None]
            w_out = gdn_w_out
        mix = _outproj(y, w_out, j, tm, cfg["tn_out"])
        r = _post_ple(mix, r, norm_post, p2, ple_proj, ple_gate, i, cfg["tm_ple"], cfg["tn_ple"])
    return (r.reshape(b, t, d), ret_out, outs["c"], outs["n"], outs["m"], outs["gs"], outs["gc"])


_PROMPT_CFG = dict(tm=2048, tn_in=512, tn_out=512, tm_ple=1024, tn_ple=512, y_dtype=BF16, proj_dtype=BF16,
                   ret_L=256, ret_BB=1, ret_HB=2, ml_L=512, ml_BB=1, ml_HB=1,
                   gdn_TB=1024, gdn_L=64, gdn_HBK=1)
_SAMPLE_CFG = dict(tm=512, tn_in=1024, tn_out=512, tm_ple=512, tn_ple=512, y_dtype=F32, proj_dtype=F32,
                   ret_L=4, ret_BB=2, ret_HB=8, ml_L=4, ml_BB=2, ml_HB=8, gdn_BB=4)


def kernel(x_prompt, x_sample, state_ret_S, state_mlstm_C, state_mlstm_n, state_mlstm_m, state_gdn_S, state_gdn_conv, p_prompt, p_sample, norm_pre, norm_post, ple_proj, ple_gate, ret_w_in, ret_head_norm, ret_w_out, mlstm_w_in, mlstm_b_gate, mlstm_head_norm, mlstm_w_out, gdn_w_in, gdn_conv_w, gdn_a_log, gdn_dt_bias, gdn_head_norm, gdn_w_out):
    w = (norm_pre, norm_post, ple_proj, ple_gate, ret_w_in, ret_head_norm, ret_w_out,
         mlstm_w_in, mlstm_b_gate, mlstm_head_norm, mlstm_w_out,
         gdn_w_in, gdn_conv_w, gdn_a_log, gdn_dt_bias, gdn_head_norm, gdn_w_out)
    pos_p = jnp.arange(x_prompt.shape[1])
    yp, ret_p, mc_p, mn_p, mm_p, gs_p, gc_p = _trunk(
        x_prompt, p_prompt, (None,) * 6, pos_p, w, _PROMPT_CFG)
    pos_s = PAST_LEN + jnp.arange(x_sample.shape[1])
    ys, ret_s, mc_s, mn_s, mm_s, gs_s, gc_s = _trunk(
        x_sample, p_sample,
        (state_ret_S, state_mlstm_C, state_mlstm_n, state_mlstm_m, state_gdn_S, state_gdn_conv),
        pos_s, w, _SAMPLE_CFG)
    return (yp, ys, ret_p, mc_p, mn_p, mm_p, gs_p, gc_p, ret_s, mc_s, mn_s, mm_s, gs_s, gc_s)
```

```python
import functools

import jax
import jax.numpy as jnp
from jax import lax
from jax.experimental import pallas as pl
from jax.experimental.pallas import tpu as pltpu

F32 = jnp.float32
BF16 = jnp.bfloat16
EPS = 1e-6
ROPE_BASE = 10000.0
CONV_W = 4
PAST_LEN = 16384
QK_HEADS = 8
QK_HEAD_DIM = 256
MIB = 1024 * 1024
NEG_INF = float("-inf")
CONV_PAD = 8
LANES = 128
TAIL_ROWS = 8
VMEM_LIMIT_MATMUL_MIB = 58
VMEM_LIMIT_MIXER_MIB = 48


def _params(sem, vmem_mib):
    return pltpu.CompilerParams(dimension_semantics=sem, vmem_limit_bytes=vmem_mib * MIB)


def _bdot(a, b):
    return jnp.dot(a.astype(BF16), b.astype(BF16), preferred_element_type=F32)


def _bdot_nt(a, b):
    return lax.dot_general(a.astype(BF16), b.astype(BF16), (((1,), (1,)), ((), ())),
                           preferred_element_type=F32)


def _bdot_tn(a, b):
    return lax.dot_general(a.astype(BF16), b.astype(BF16), (((0,), (0,)), ((), ())),
                           preferred_element_type=F32)


def _sigmoid(x):
    return 0.5 * jnp.tanh(0.5 * x) + 0.5


def _silu(x):
    h = 0.5 * x
    return h * jnp.tanh(h) + h


def _rows3(table):
    return table.reshape(table.shape[0], 1, table.shape[1])


SLAB = 256


def _row_slabs(tm):
    step = min(SLAB, tm)
    return [slice(s, s + step) for s in range(0, tm, step)]


def _once_per_row_tile(block_shape, index_map):
    return pl.BlockSpec(block_shape, index_map, pipeline_mode=pl.Buffered(1))


def _inproj_body(x_ref, g_ref, w_ref, *rest, w_is_nk, n_main, gate_cols, has_tail):
    has_gates = gate_cols > 0
    outs = list(rest[:1 + has_gates + has_tail])
    h_ref = rest[-1]
    o_ref = outs.pop(0)
    gates_ref = outs.pop(0) if has_gates else None
    tail_ref = outs.pop(0) if has_tail else None
    c = pl.program_id(1)

    @pl.when(c == 0)
    def _():
        for rows in _row_slabs(x_ref.shape[0]):
            x = x_ref[rows, :]
            ms = jnp.mean(x * x, axis=-1, keepdims=True)
            h_ref[rows, :] = (x * lax.rsqrt(ms + EPS) * g_ref[...]).astype(BF16)

    w = w_ref[...].astype(BF16)
    contract = (((1,), (1,)), ((), ())) if w_is_nk else (((1,), (0,)), ((), ()))
    acc = lax.dot_general(h_ref[...], w, contract, preferred_element_type=F32)

    def write_main():
        o_ref[...] = acc.astype(o_ref.dtype)
        if has_tail:
            tail_ref[...] = acc[acc.shape[0] - TAIL_ROWS:, :]

    if has_gates:
        pl.when(c < n_main)(write_main)

        @pl.when(c == n_main)
        def _():
            lane = _iota2(acc.shape[0], LANES, 1)
            gates_ref[...] = jnp.where(lane < gate_cols, acc[:, :LANES], 0.0)
    else:
        write_main()


def _inproj(x2d, g_all, layer, w_all, j, tm, tn, out_dtype, want_tail=False):
    m, d = x2d.shape
    n = w_all.shape[-1]
    n_main = n // tn
    has_gates = n % tn != 0
    assert n - n_main * tn <= LANES
    w_is_nk = n % LANES != 0
    if w_is_nk:
        w_all = jnp.swapaxes(w_all, 1, 2)
        w_spec = pl.BlockSpec((None, tn, d), lambda i, c: (j, c, 0))
    else:
        w_spec = pl.BlockSpec((None, d, tn), lambda i, c: (j, 0, c))

    def main_col(c):
        return jnp.minimum(c, n_main - 1)

    out_specs = [pl.BlockSpec((tm, tn), lambda i, c: (i, main_col(c)))]
    out_shape = [jax.ShapeDtypeStruct((m, n_main * tn), out_dtype)]
    if has_gates:
        out_specs.append(pl.BlockSpec((tm, LANES), lambda i, c: (i, 0)))
        out_shape.append(jax.ShapeDtypeStruct((m, LANES), F32))
    if want_tail:
        out_specs.append(pl.BlockSpec((None, TAIL_ROWS, tn), lambda i, c: (i, 0, main_col(c))))
        out_shape.append(jax.ShapeDtypeStruct((m // tm, TAIL_ROWS, n_main * tn), F32))
    res = pl.pallas_call(
        functools.partial(_inproj_body, w_is_nk=w_is_nk, n_main=n_main, gate_cols=n - n_main * tn,
                          has_tail=want_tail),
        grid=(m // tm, n_main + has_gates),
        in_specs=[_once_per_row_tile((tm, d), lambda i, c: (i, 0)),
                  pl.BlockSpec((None, 1, d), lambda i, c: (layer, 0, 0)),
                  w_spec],
        out_specs=out_specs,
        out_shape=out_shape,
        scratch_shapes=[pltpu.VMEM((tm, d), BF16)],
        compiler_params=_params(("arbitrary", "arbitrary"), VMEM_LIMIT_MATMUL_MIB),
        name="inproj",
    )(x2d, _rows3(g_all), w_all)
    res = list(res)
    proj = res.pop(0)
    gates = res.pop(0) if has_gates else None
    tail = res.pop(0) if want_tail else None
    return proj, gates, tail


def _outproj_body(y_ref, w_ref, o_ref):
    o_ref[...] = jnp.dot(y_ref[...].astype(BF16), w_ref[...].astype(BF16), preferred_element_type=F32)


def _outproj(y2d, w_all, j, tm, tn):
    m, k = y2d.shape
    n = w_all.shape[-1]
    return pl.pallas_call(
        _outproj_body,
        grid=(m // tm, n // tn),
        in_specs=[_once_per_row_tile((tm, k), lambda i, c: (i, 0)),
                  pl.BlockSpec((None, k, tn), lambda i, c: (j, 0, c))],
        out_specs=pl.BlockSpec((tm, tn), lambda i, c: (i, c)),
        out_shape=jax.ShapeDtypeStruct((m, n), F32),
        compiler_params=_params(("arbitrary", "arbitrary"), VMEM_LIMIT_MATMUL_MIB),
        name="outproj",
    )(y2d, w_all)


def _post_ple_body(mix_ref, r_ref, g_ref, p_ref, proj_ref, gate_ref, o_ref, rs_ref, r1b_ref, *, tn):
    c = pl.program_id(1)

    @pl.when(c == 0)
    def _():
        for rows in _row_slabs(mix_ref.shape[0]):
            mix = mix_ref[rows, :]
            rs = lax.rsqrt(jnp.mean(mix * mix, axis=-1, keepdims=True) + EPS)
            rs_ref[rows, :] = rs
            r1b_ref[rows, :] = (r_ref[rows, :] + mix * rs * g_ref[...]).astype(BF16)

    gate = jnp.dot(r1b_ref[...], gate_ref[...].astype(BF16), preferred_element_type=F32)
    emb = _bdot(p_ref[...], proj_ref[...])
    cols = pl.ds(pl.multiple_of(c * tn, tn), tn)
    r1 = r_ref[:, cols] + mix_ref[:, cols] * rs_ref[...] * g_ref[:, cols]
    o_ref[...] = r1 + emb * _sigmoid(gate)


def _post_ple(mix, r, g_all, p_all, proj_all, gate_all, layer, tm, tn):
    m, d = r.shape
    pd = p_all.shape[-1]
    return pl.pallas_call(
        functools.partial(_post_ple_body, tn=tn),
        grid=(m // tm, d // tn),
        in_specs=[pl.BlockSpec((tm, d), lambda i, c: (i, 0)),
                  pl.BlockSpec((tm, d), lambda i, c: (i, 0)),
                  pl.BlockSpec((None, 1, d), lambda i, c: (layer, 0, 0)),
                  pl.BlockSpec((None, tm, pd), lambda i, c: (layer, i, 0)),
                  pl.BlockSpec((None, pd, tn), lambda i, c: (layer, 0, c)),
                  pl.BlockSpec((None, d, tn), lambda i, c: (layer, 0, c))],
        out_specs=pl.BlockSpec((tm, tn), lambda i, c: (i, c)),
        out_shape=jax.ShapeDtypeStruct((m, d), F32),
        scratch_shapes=[pltpu.VMEM((tm, 1), F32), pltpu.VMEM((tm, d), BF16)],
        compiler_params=_params(("arbitrary", "arbitrary"), VMEM_LIMIT_MATMUL_MIB),
        name="post_ple",
    )(mix, r, _rows3(g_all), p_all, proj_all, gate_all)


def _iota2(n, m, dim):
    return lax.broadcasted_iota(jnp.int32, (n, m), dim)


def _col_to_row(col, eye):
    return jnp.sum(jnp.where(eye, col, 0.0), axis=0, keepdims=True)


def _cumsum_col_row(col, row_i, col_i, eye):
    row = _col_to_row(col, eye)
    c_col = jnp.sum(jnp.where(col_i <= row_i, row, 0.0), axis=1, keepdims=True)
    c_row = jnp.sum(jnp.where(row_i <= col_i, col, 0.0), axis=0, keepdims=True)
    return c_col, c_row


def _pick_lane(blk, lane_iota, idx):
    return jnp.sum(jnp.where(lane_iota == idx, blk, 0.0), axis=1, keepdims=True)


def _head_norm_gate(o, gain, z, center):
    if center:
        o = o - jnp.mean(o, axis=-1, keepdims=True)
    y = o * lax.rsqrt(jnp.mean(o * o, axis=-1, keepdims=True) + EPS) * gain
    return y * _silu(z.astype(F32))


def _row_block(nc):
    return lambda col: (lambda bi, hi, ci: (bi * nc + ci, col(hi)))


class _State:
    def __init__(self, in_ref, out_ref, scr, nc):
        self.in_ref, self.out_ref, self.scr, self.nc = in_ref, out_ref, scr, nc

    def start(self, chunk):
        if self.nc > 1:
            @pl.when(chunk == 0)
            def _():
                if self.in_ref is None:
                    self.scr[...] = jnp.zeros_like(self.scr)
                else:
                    self.scr[...] = self.in_ref[...]

    def get(self, bb, hh):
        if self.nc > 1:
            return self.scr[bb, hh]
        if self.in_ref is None:
            return jnp.zeros(self.out_ref.shape[2:], F32)
        return self.in_ref[bb, hh]

    def put(self, bb, hh, val):
        if self.nc > 1:
            self.scr[bb, hh] = val
        else:
            self.out_ref[bb, hh] = val

    def finish(self, chunk):
        if self.nc > 1:
            @pl.when(chunk == self.nc - 1)
            def _():
                self.out_ref[...] = self.scr[...]


def _ret_body(*refs, L, BB, HB, NC, DK, DV, has_s0, has_prev):
    lg_ref, q_ref, k_ref, v_ref, z_ref, cos_ref, sin_ref, gain_ref = refs[:8]
    s0_ref = refs[8] if has_s0 else None
    rest = refs[8 + has_s0 + has_prev:]
    y_ref, so_ref = rest[:2]
    st = _State(s0_ref, so_ref, rest[2] if NC > 1 else None, NC)
    hb = pl.program_id(1)
    c = pl.program_id(2)
    st.start(c)

    cos = cos_ref[...]
    sin = sin_ref[...]
    half = DK // 2

    def rot(x):
        x1, x2 = x[:, :half], x[:, half:]
        return jnp.concatenate([x1 * cos - x2 * sin, x2 * cos + x1 * sin], axis=-1)

    row_i = _iota2(L, L, 0)
    col_i = _iota2(L, L, 1)
    rel = (row_i - col_i).astype(F32)
    idx = _iota2(L, 1, 0).astype(F32)
    for hh in range(HB):
        lg = lg_ref[hb * HB + hh]
        decay = jnp.exp(jnp.where(rel >= 0, lg * rel, NEG_INF))
        w_in = jnp.exp(lg * (idx + 1.0))
        w_out = jnp.exp(lg * (L - 1.0 - idx))
        w_all = jnp.exp(jnp.full((1, 1), L, F32) * lg)
        hq = slice(hh * DK, (hh + 1) * DK)
        hv = slice(hh * DV, (hh + 1) * DV)
        for bb in range(BB):
            rows = slice(bb * L, (bb + 1) * L)
            q = rot(q_ref[rows, hq].astype(F32))
            k = rot(k_ref[rows, hq].astype(F32)) * DK ** -0.5
            v = v_ref[rows, hv]
            s = st.get(bb, hh)
            scores = _bdot_nt(q, k) * decay
            o = _bdot(scores, v) + _bdot(q * w_in, s)
            st.put(bb, hh, w_all * s + _bdot_tn(k * w_out, v))
            y_ref[rows, hv] = _head_norm_gate(o, gain_ref[:, hv], z_ref[rows, hv], True).astype(y_ref.dtype)
    st.finish(c)


def _retention(proj, b, t, s0_all, j, cos, sin, gain_all, so_prev, n_layers, L, BB, HB, y_dtype):
    h, dk = QK_HEADS, QK_HEAD_DIM
    dv = gain_all.shape[-1] // h
    nc = t // L
    assert BB == 1 or nc == 1
    hg = h // HB
    rb = _row_block(nc)
    lg = jnp.log1p(-jnp.exp2(-5.0 - jnp.arange(h, dtype=F32)))
    has_s0 = s0_all is not None
    v_blk = (2 * h * dk) // (HB * dv)
    z_blk = (2 * h * dk + h * dv) // (HB * dv)
    in_specs = [pl.BlockSpec(memory_space=pltpu.SMEM),
                pl.BlockSpec((BB * L, HB * dk), rb(lambda hi: hi)),
                pl.BlockSpec((BB * L, HB * dk), rb(lambda hi: hg + hi)),
                pl.BlockSpec((BB * L, HB * dv), rb(lambda hi: v_blk + hi)),
                pl.BlockSpec((BB * L, HB * dv), rb(lambda hi: z_blk + hi)),
                pl.BlockSpec((L, dk // 2), lambda bi, hi, ci: (ci, 0)),
                pl.BlockSpec((L, dk // 2), lambda bi, hi, ci: (ci, 0)),
                pl.BlockSpec((None, 1, HB * dv), lambda bi, hi, ci: (j, 0, hi))]
    args = [lg, proj, proj, proj, proj, cos, sin, _rows3(gain_all)]
    st_spec = pl.BlockSpec((None, BB, HB, dk, dv), lambda bi, hi, ci: (j, bi, hi, 0, 0))
    if has_s0:
        in_specs.append(st_spec)
        args.append(s0_all)
    aliases = {}
    if so_prev is not None:
        in_specs.append(pl.BlockSpec(memory_space=pl.ANY))
        args.append(so_prev)
        aliases = {len(args) - 1: 1}
    return pl.pallas_call(
        functools.partial(_ret_body, L=L, BB=BB, HB=HB, NC=nc, DK=dk, DV=dv, has_s0=has_s0,
                          has_prev=so_prev is not None),
        grid=(b // BB, hg, nc),
        in_specs=in_specs,
        out_specs=[pl.BlockSpec((BB * L, HB * dv), rb(lambda hi: hi)), st_spec],
        out_shape=[jax.ShapeDtypeStruct((b * t, h * dv), y_dtype),
                   jax.ShapeDtypeStruct((n_layers, b, h, dk, dv), F32)],
        scratch_shapes=[pltpu.VMEM((BB, HB, dk, dv), F32)] if nc > 1 else [],
        input_output_aliases=aliases,
        compiler_params=_params(("arbitrary", "arbitrary", "arbitrary"), VMEM_LIMIT_MIXER_MIB),
        name="retention",
    )(*args)


def _mlstm_body(*refs, L, BB, HB, NC, DK, DV, H, has_s0):
    bg_ref, q_ref, k_ref, v_ref, og_ref, z_ref, gt_ref, gain_ref = refs[:8]
    c0_ref, n0_ref, m0_ref = refs[8:11] if has_s0 else (None, None, None)
    rest = refs[8 + 3 * has_s0:]
    y_ref, co_ref, no_ref, mo_ref = rest[:4]
    scr = rest[4:] if NC > 1 else (None, None, None)
    st_c = _State(c0_ref, co_ref, scr[0], NC)
    st_n = _State(n0_ref, no_ref, scr[1], NC)
    st_m = _State(m0_ref, mo_ref, scr[2], NC)
    hb = pl.program_id(1)
    c = pl.program_id(2)
    for st in (st_c, st_n, st_m):
        st.start(c)

    row_i = _iota2(L, L, 0)
    col_i = _iota2(L, L, 1)
    eye = row_i == col_i
    causal = row_i >= col_i
    lane = _iota2(L, 128, 1)
    for bb in range(BB):
        rows = slice(bb * L, (bb + 1) * L)
        gt = gt_ref[rows, :]
        for hh in range(HB):
            head = hb * HB + hh
            hq = slice(hh * DK, (hh + 1) * DK)
            hv = slice(hh * DV, (hh + 1) * DV)
            ig = _pick_lane(gt, lane, head) + bg_ref[head]
            fg = _pick_lane(gt, lane, H + head) + bg_ref[H + head]
            lf = jax.nn.log_sigmoid(fg)
            b_col, b_row = _cumsum_col_row(lf, row_i, col_i, eye)
            i_row = _col_to_row(ig, eye)
            q = q_ref[rows, hq].astype(F32) * DK ** -0.5
            k = k_ref[rows, hq].astype(F32)
            v = v_ref[rows, hv]
            cm = st_c.get(bb, hh)
            nv = st_n.get(bb, hh)
            m_prev = st_m.get(bb, hh)
            dlog = jnp.where(causal, b_col - b_row + i_row, NEG_INF)
            inter = b_col + m_prev
            mt = jnp.maximum(inter, jnp.max(dlog, axis=1, keepdims=True))
            s = _bdot_nt(q, k) * jnp.exp(dlog - mt)
            wi = jnp.exp(inter - mt)
            num = _bdot(s, v) + wi * _bdot(q, cm)
            den = jnp.sum(s, axis=1, keepdims=True) + wi * jnp.sum(q * nv, axis=1, keepdims=True)
            ht = num / jnp.maximum(jnp.abs(den), jnp.exp(-mt))
            m_new = mt[L - 1:L, :]
            b_last = b_col[L - 1:L, :]
            w_last = jnp.exp(b_last - b_col + ig - m_new)
            dec = jnp.exp(b_last + m_prev - m_new)
            kw = k * w_last
            st_c.put(bb, hh, dec * cm + _bdot_tn(kw, v))
            st_n.put(bb, hh, dec * nv + jnp.sum(kw, axis=0, keepdims=True))
            st_m.put(bb, hh, m_new)
            hcell = ht * _sigmoid(og_ref[rows, hv].astype(F32))
            y_ref[rows, hv] = _head_norm_gate(hcell, gain_ref[:, hv], z_ref[rows, hv], True).astype(y_ref.dtype)
    for st in (st_c, st_n, st_m):
        st.finish(c)


def _mlstm(proj, gates, b, t, c0_all, n0_all, m0_all, j, bgate_all, gain_all, L, BB, HB, y_dtype):
    h, dk = QK_HEADS, QK_HEAD_DIM
    dv = gain_all.shape[-1] // h
    nc = t // L
    assert BB == 1 or nc == 1
    hg = h // HB
    rb = _row_block(nc)
    has_s0 = c0_all is not None
    v_off = 2 * h * dk
    wv = HB * dv
    in_specs = [pl.BlockSpec(memory_space=pltpu.SMEM),
                pl.BlockSpec((BB * L, HB * dk), rb(lambda hi: hi)),
                pl.BlockSpec((BB * L, HB * dk), rb(lambda hi: hg + hi)),
                pl.BlockSpec((BB * L, wv), rb(lambda hi: v_off // wv + hi)),
                pl.BlockSpec((BB * L, wv), rb(lambda hi: (v_off + h * dv) // wv + hi)),
                pl.BlockSpec((BB * L, wv), rb(lambda hi: (v_off + 2 * h * dv) // wv + hi)),
                pl.BlockSpec((BB * L, LANES), rb(lambda hi: 0)),
                pl.BlockSpec((None, 1, wv), lambda bi, hi, ci: (j, 0, hi))]
    args = [bgate_all[j], proj, proj, proj, proj, proj, gates, _rows3(gain_all)]
    c_spec = pl.BlockSpec((None, BB, HB, dk, dv), lambda bi, hi, ci: (j, bi, hi, 0, 0))
    n_spec = pl.BlockSpec((None, BB, HB, 1, dk), lambda bi, hi, ci: (j, bi, hi, 0, 0))
    m_spec = pl.BlockSpec((None, BB, HB, 1, 1), lambda bi, hi, ci: (j, bi, hi, 0, 0))
    nl = 1
    if has_s0:
        assert c0_all.shape[0] == nl
        in_specs += [c_spec, n_spec, m_spec]
        args += [c0_all, n0_all.reshape(nl, b, h, 1, dk), m0_all.reshape(nl, b, h, 1, 1)]
    scratch = [pltpu.VMEM((BB, HB, dk, dv), F32), pltpu.VMEM((BB, HB, 1, dk), F32),
               pltpu.VMEM((BB, HB, 1, 1), F32)] if nc > 1 else []
    y, co, no, mo = pl.pallas_call(
        functools.partial(_mlstm_body, L=L, BB=BB, HB=HB, NC=nc, DK=dk, DV=dv, H=h, has_s0=has_s0),
        grid=(b // BB, hg, nc),
        in_specs=in_specs,
        out_specs=[pl.BlockSpec((BB * L, wv), rb(lambda hi: hi)), c_spec, n_spec, m_spec],
        out_shape=[jax.ShapeDtypeStruct((b * t, h * dv), y_dtype),
                   jax.ShapeDtypeStruct((nl, b, h, dk, dv), F32),
                   jax.ShapeDtypeStruct((nl, b, h, 1, dk), F32),
                   jax.ShapeDtypeStruct((nl, b, h, 1, 1), F32)],
        scratch_shapes=scratch,
        compiler_params=_params(("arbitrary", "arbitrary", "arbitrary"), VMEM_LIMIT_MIXER_MIB),
        name="mlstm",
    )(*args)
    return y, co, no.reshape(nl, b, h, dk), mo.reshape(nl, b, h)


def _merge_masks(row_i, col_i, L):
    masks = []
    s = 1
    while s < L:
        masks.append(((row_i // (2 * s)) == (col_i // (2 * s))) & ((row_i // s) > (col_i // s)))
        s *= 2
    return masks


def _unit_lower_inverse(a_strict, eye, masks, mm):
    x = jnp.where(eye, 1.0, 0.0) - jnp.where(masks[0], a_strict, 0.0)
    for mask in masks[1:]:
        e = jnp.where(mask, a_strict, 0.0)
        x = x - mm(x, mm(e, x))
    return x


def _vpu_mm(a, b):
    out = a[:, 0:1] * b[0:1, :]
    for kk in range(1, a.shape[1]):
        out = out + a[:, kk:kk + 1] * b[kk:kk + 1, :]
    return out


def _bmm(a, b):
    return jnp.einsum("gik,gkj->gij", a.astype(BF16), b.astype(BF16), preferred_element_type=F32)


def _bmm_nt(a, b):
    return jnp.einsum("gik,gjk->gij", a.astype(BF16), b.astype(BF16), preferred_element_type=F32)


def _bmm_tn(a, b):
    return jnp.einsum("gki,gkj->gij", a.astype(BF16), b.astype(BF16), preferred_element_type=F32)


def _gdn_dims(conv_w_all, a_log_all, gain_all):
    dk = dv = gain_all.shape[-1]
    hv = a_log_all.shape[-1]
    cdim = conv_w_all.shape[-1]
    hk = (cdim - hv * dv) // (2 * dk)
    rep = hv // hk
    assert 2 * hv <= 128 and rep * hk == hv
    return dk, dv, hv, hk, rep, cdim


def _gdn_step_body(alv_ref, dtv_ref, qx_ref, kx_ref, vx_ref, z_ref, gt_ref, cwq_ref, cwk_ref, cwv_ref,
                   gain_ref, cq0_ref, ck0_ref, cv0_ref, s0_ref, y_ref, so_ref,
                   eq_scr, ek_scr, ev_scr, q_st, k_st, v_st, z_st, qs_st, ks_st, kw_st, u_st, y_st,
                   *, L, BB, HK, REP, DK, DV):
    HV = HK * REP
    R = HV * L
    taps = CONV_W - 1

    def conv(x_ref, c0_ref, scr, cw_ref, bb):
        scr[CONV_PAD - taps:CONV_PAD, :] = c0_ref[bb]
        scr[CONV_PAD:CONV_PAD + L, :] = x_ref[bb * L:(bb + 1) * L, :].astype(F32)
        acc = scr[pl.ds(CONV_PAD - taps, L), :] * cw_ref[0:1, :]
        for w in range(1, CONV_W):
            acc = acc + scr[pl.ds(CONV_PAD - taps + w, L), :] * cw_ref[w:w + 1, :]
        return _silu(acc)

    row_i = _iota2(R, R, 0)
    col_i = _iota2(R, R, 1)
    same_head = (row_i // L) == (col_i // L)
    eye = row_i == col_i
    incl = same_head & (row_i >= col_i)
    strict = same_head & (row_i > col_i)
    masks = [same_head & m for m in _merge_masks(row_i % L, col_i % L, L)]
    lane = _iota2(R, LANES, 1)
    head_of_row = _iota2(R, LANES, 0) // L
    sel_beta = lane == head_of_row
    sel_decay = lane == head_of_row + HV
    tril = (_iota2(L, L, 0) >= _iota2(L, L, 1)).astype(F32)

    def stack_cols(x, sel):
        x8 = jnp.concatenate([x] * (8 // L), axis=0)
        tiled = jnp.concatenate([x8] * (R // 8), axis=0)
        return jnp.sum(jnp.where(sel, tiled, 0.0), axis=1, keepdims=True)

    for bb in range(BB):
        rows = slice(bb * L, (bb + 1) * L)
        cq = conv(qx_ref, cq0_ref, eq_scr, cwq_ref, bb)
        ck = conv(kx_ref, ck0_ref, ek_scr, cwk_ref, bb)
        cv = conv(vx_ref, cv0_ref, ev_scr, cwv_ref, bb)
        for vh in range(HV):
            st = slice(vh * L, (vh + 1) * L)
            kh = vh // REP
            q_st[st, :] = cq[:, kh * DK:(kh + 1) * DK]
            k_st[st, :] = ck[:, kh * DK:(kh + 1) * DK]
            v_st[st, :] = cv[:, vh * DV:(vh + 1) * DV]
            z_st[st, :] = z_ref[rows, vh * DV:(vh + 1) * DV].astype(F32)
        q = q_st[...]
        k = k_st[...]
        q = q * lax.rsqrt(jnp.sum(q * q, axis=-1, keepdims=True) + EPS) * DK ** -0.5
        k = k * lax.rsqrt(jnp.sum(k * k, axis=-1, keepdims=True) + EPS)
        q_st[...] = q
        k_st[...] = k

        gt = gt_ref[rows, :]
        beta = stack_cols(_sigmoid(gt), sel_beta)
        g = -jnp.exp(alv_ref[...]) * jax.nn.softplus(gt + dtv_ref[...])
        g_cum = _vpu_mm(tril, g)
        g_col = stack_cols(g_cum, sel_decay)
        g_last = stack_cols(jnp.broadcast_to(g_cum[L - 1:L, :], (L, LANES)), sel_decay)
        g_row = _col_to_row(g_col, eye)
        decay = jnp.exp(jnp.where(incl, g_col - g_row, NEG_INF))
        kq = _bdot_nt(jnp.concatenate([k, q], axis=0), k)
        kk, qk = kq[:R], kq[R:]
        a = jnp.where(strict, beta * kk * decay, 0.0)
        x = _unit_lower_inverse(a, eye, masks, _bdot)

        for vh in range(HV):
            st = slice(vh * L, (vh + 1) * L)
            qk_rows = jnp.concatenate([q_st[st, :], k_st[st, :]], axis=0)
            both = _bdot(qk_rows, s0_ref[bb, vh])
            qs_st[st, :] = both[:L]
            ks_st[st, :] = both[L:]
        eg = jnp.exp(g_col)
        rhs = beta * v_st[...] - (beta * eg) * ks_st[...]
        u = _bdot(x, rhs)
        o = eg * qs_st[...] + _bdot(qk * decay, u)
        y_st[...] = _head_norm_gate(o, gain_ref[...], z_st[...], False)
        kw_st[...] = k * jnp.exp(g_last - g_col)
        u_st[...] = u
        eg_last = jnp.exp(g_last)
        for vh in range(HV):
            st = slice(vh * L, (vh + 1) * L)
            y_ref[rows, vh * DV:(vh + 1) * DV] = y_st[st, :].astype(y_ref.dtype)
            so_ref[bb, vh] = (eg_last[vh * L:vh * L + 1, :] * s0_ref[bb, vh]
                              + _bdot_tn(kw_st[st, :], u_st[st, :]))


def _gdn_step(proj, gates, b, t, s0_all, conv0_all, j, conv_w_all, a_log_all, dt_bias_all, gain_all,
              BB, y_dtype):
    dk, dv, hv, hk, rep, cdim = _gdn_dims(conv_w_all, a_log_all, gain_all)
    L = t
    R = hv * L
    assert 8 % L == 0 and R % 8 == 0 and dk == dv
    hg = 1
    wq, wv = hk * dk, hv * dv
    k_blk = 1
    v_blk = (2 * hk * dk) // wv
    z_blk = cdim // wv
    rb = _row_block(1)
    lanes_of = lambda vec: jnp.zeros((1, LANES), F32).at[0, hv:2 * hv].set(vec.astype(F32))

    def cspec(width, off):
        return pl.BlockSpec((None, CONV_W, width), lambda bi, hi, ci: (j, 0, off + hi))

    def c0spec(width, off):
        return pl.BlockSpec((None, BB, CONV_W - 1, width), lambda bi, hi, ci: (j, bi, 0, off + hi))

    st_spec = pl.BlockSpec((None, BB, hv, dk, dv), lambda bi, hi, ci: (j, bi, hi, 0, 0))
    in_specs = [pl.BlockSpec((1, LANES), lambda bi, hi, ci: (0, 0)),
                pl.BlockSpec((1, LANES), lambda bi, hi, ci: (0, 0)),
                pl.BlockSpec((BB * L, wq), rb(lambda hi: hi)),
                pl.BlockSpec((BB * L, wq), rb(lambda hi: k_blk + hi)),
                pl.BlockSpec((BB * L, wv), rb(lambda hi: v_blk + hi)),
                pl.BlockSpec((BB * L, wv), rb(lambda hi: z_blk + hi)),
                pl.BlockSpec((BB * L, LANES), rb(lambda hi: 0)),
                cspec(wq, 0), cspec(wq, k_blk), cspec(wv, v_blk),
                pl.BlockSpec((None, 1, dv), lambda bi, hi, ci: (j, 0, 0)),
                c0spec(wq, 0), c0spec(wq, k_blk), c0spec(wv, v_blk), st_spec]
    args = [lanes_of(a_log_all[j]), lanes_of(dt_bias_all[j]), proj, proj, proj, proj, gates,
            conv_w_all, conv_w_all, conv_w_all, _rows3(gain_all),
            conv0_all, conv0_all, conv0_all, s0_all]
    stacked = [pltpu.VMEM((R, dk), F32)] * 9
    y, so = pl.pallas_call(
        functools.partial(_gdn_step_body, L=L, BB=BB, HK=hk, REP=rep, DK=dk, DV=dv),
        grid=(b // BB, hg, 1),
        in_specs=in_specs,
        out_specs=[pl.BlockSpec((BB * L, wv), rb(lambda hi: hi)), st_spec],
        out_shape=[jax.ShapeDtypeStruct((b * t, hv * dv), y_dtype),
                   jax.ShapeDtypeStruct((1, b, hv, dk, dv), F32)],
        scratch_shapes=[pltpu.VMEM((L + CONV_PAD, wq), F32), pltpu.VMEM((L + CONV_PAD, wq), F32),
                        pltpu.VMEM((L + CONV_PAD, wv), F32)] + stacked,
        compiler_params=_params(("arbitrary", "arbitrary", "arbitrary"), VMEM_LIMIT_MIXER_MIB),
        name="gdn_step",
    )(*args)
    return y, so


def _gdn_seq_body(al_ref, dt_ref, qx_ref, kx_ref, vx_ref, z_ref, gt_ref, cwq_ref, cwk_ref, cwv_ref,
                  gain_ref, y_ref, so_ref, s_scr, eq_scr, ek_scr, ev_scr,
                  *, TB, C, HBK, NTB, DK, DV, REP, HV):
    hb = pl.program_id(1)
    tb = pl.program_id(2)
    G = TB // C
    taps = CONV_W - 1

    @pl.when(tb == 0)
    def _():
        for scr in (eq_scr, ek_scr, ev_scr):
            scr[0:CONV_PAD, :] = jnp.zeros((CONV_PAD, scr.shape[1]), F32)
        s_scr[...] = jnp.zeros_like(s_scr)

    def conv(x_ref, scr, cw_ref):
        scr[CONV_PAD:CONV_PAD + TB, :] = x_ref[...].astype(F32)
        acc = scr[pl.ds(CONV_PAD - taps, TB), :] * cw_ref[0:1, :]
        for w in range(1, CONV_W):
            acc = acc + scr[pl.ds(CONV_PAD - taps + w, TB), :] * cw_ref[w:w + 1, :]
        if NTB > 1:
            scr[0:CONV_PAD, :] = scr[TB:TB + CONV_PAD, :]
        return _silu(acc)

    cq = conv(qx_ref, eq_scr, cwq_ref)
    ck = conv(kx_ref, ek_scr, cwk_ref)
    cv = conv(vx_ref, ev_scr, cwv_ref)

    row_i = _iota2(C, C, 0)
    col_i = _iota2(C, C, 1)
    eye = row_i == col_i
    incl = row_i >= col_i
    PW = REP * C
    prow = _iota2(C, PW, 0)
    plane = _iota2(C, PW, 1)
    pcol = plane % C
    phead = plane // C
    p_eye = prow == pcol
    p_incl = prow >= pcol
    p_strict = prow > pcol
    p_masks = _merge_masks(prow, pcol, C)
    bd_mask = (_iota2(PW, PW, 0) // C) == (_iota2(PW, PW, 1) // C)

    def block_diag(xp):
        return jnp.where(bd_mask, jnp.concatenate([xp] * REP, axis=1), 0.0)

    def packed_mm(ap, bp):
        return _bmm(ap, block_diag(bp))

    def pack_cols(cols):
        out = cols[0]
        for r in range(1, REP):
            out = jnp.where(phead >= r, cols[r], out)
        return out

    lane = _iota2(TB, 128, 1)
    gt = gt_ref[...]
    for kh in range(HBK):
        q = cq[:, kh * DK:(kh + 1) * DK]
        k = ck[:, kh * DK:(kh + 1) * DK]
        q = q * lax.rsqrt(jnp.sum(q * q, axis=-1, keepdims=True) + EPS) * DK ** -0.5
        k = k * lax.rsqrt(jnp.sum(k * k, axis=-1, keepdims=True) + EPS)
        q3 = q.reshape(G, C, DK)
        k3 = k.reshape(G, C, DK)
        kq = _bmm_nt(jnp.concatenate([k3, q3], axis=1), k3)
        kk, qk = kq[:, :C, :], kq[:, C:, :]
        betas, g_cols, g_rows = [], [], []
        for r in range(REP):
            head = (hb * HBK + kh) * REP + r
            betas.append(_sigmoid(_pick_lane(gt, lane, head)).reshape(G, C, 1))
            a_neg = -jnp.exp(jnp.full((1, 1), al_ref[head], F32))
            g = (a_neg * jax.nn.softplus(_pick_lane(gt, lane, HV + head) + dt_ref[head])).reshape(G, C, 1)
            g_lanes = jnp.sum(jnp.where(eye, g, 0.0), axis=1, keepdims=True)
            g_cols.append(jnp.sum(jnp.where(incl, g_lanes, 0.0), axis=2, keepdims=True))
            g_rows.append(jnp.sum(jnp.where(row_i <= col_i, g, 0.0), axis=1, keepdims=True))
        decay_p = jnp.exp(jnp.where(p_incl, pack_cols(g_cols) - jnp.concatenate(g_rows, axis=-1), NEG_INF))
        a_p = jnp.where(p_strict, pack_cols(betas) * jnp.concatenate([kk] * REP, axis=-1) * decay_p, 0.0)
        x_p = _unit_lower_inverse(a_p, p_eye, p_masks, packed_mm)
        for r in range(REP):
            vh = kh * REP + r
            hv = slice(vh * DV, (vh + 1) * DV)
            beta, g_col = betas[r], g_cols[r]
            x = x_p[:, :, r * C:(r + 1) * C]
            decay = decay_p[:, :, r * C:(r + 1) * C]
            v3 = cv[:, hv].reshape(G, C, DV)
            eg = jnp.exp(g_col)
            wu = _bmm(x, jnp.concatenate([(beta * eg) * k3, beta * v3], axis=-1))
            qo = _bmm(qk * decay, wu)
            o0 = qo[:, :, DK:]
            g_last = g_col[:, C - 1:C, :]
            mb = _bmm_tn(k3 * jnp.exp(g_last - g_col), wu)
            b_eff = mb[:, :, DK:]
            lhs = jnp.concatenate([eg * q3 - qo[:, :, :DK], mb[:, :, :DK]], axis=1).astype(BF16)
            eg_last = jnp.exp(g_last)
            s = s_scr[vh]
            outs = []
            for c in range(G):
                both = jnp.dot(lhs[c], s.astype(BF16), preferred_element_type=F32)
                outs.append(both[:C] + o0[c])
                s = eg_last[c] * s - both[C:] + b_eff[c]
            s_scr[vh] = s
            o = jnp.concatenate(outs, axis=0) if G > 1 else outs[0]
            y_ref[:, hv] = _head_norm_gate(o, gain_ref[...], z_ref[:, hv], False).astype(y_ref.dtype)

    @pl.when(tb == NTB - 1)
    def _():
        so_ref[...] = s_scr[...]


def _gdn_seq(proj, gates, b, t, j, conv_w_all, a_log_all, dt_bias_all, gain_all, TB, C, HBK, y_dtype):
    dk, dv, hv, hk, rep, cdim = _gdn_dims(conv_w_all, a_log_all, gain_all)
    assert dk == dv
    ntb = t // TB
    hg = hk // HBK
    wq, wv = HBK * dk, HBK * rep * dv
    k_blk = (hk * dk) // wq
    v_blk = (2 * hk * dk) // wv
    z_blk = cdim // wv
    rb = _row_block(ntb)

    def cspec(width, off):
        return pl.BlockSpec((None, CONV_W, width), lambda bi, hi, ti: (j, 0, off + hi))

    in_specs = [pl.BlockSpec(memory_space=pltpu.SMEM),
                pl.BlockSpec(memory_space=pltpu.SMEM),
                pl.BlockSpec((TB, wq), rb(lambda hi: hi)),
                pl.BlockSpec((TB, wq), rb(lambda hi: k_blk + hi)),
                pl.BlockSpec((TB, wv), rb(lambda hi: v_blk + hi)),
                pl.BlockSpec((TB, wv), rb(lambda hi: z_blk + hi)),
                pl.BlockSpec((TB, LANES), rb(lambda hi: 0)),
                cspec(wq, 0), cspec(wq, k_blk), cspec(wv, v_blk),
                pl.BlockSpec((None, 1, dv), lambda bi, hi, ti: (j, 0, 0))]
    args = [a_log_all[j], dt_bias_all[j], proj, proj, proj, proj, gates,
            conv_w_all, conv_w_all, conv_w_all, _rows3(gain_all)]
    st_spec = pl.BlockSpec((None, None, HBK * rep, dk, dv), lambda bi, hi, ti: (0, bi, hi, 0, 0))
    return pl.pallas_call(
        functools.partial(_gdn_seq_body, TB=TB, C=C, HBK=HBK, NTB=ntb, DK=dk, DV=dv, REP=rep, HV=hv),
        grid=(b, hg, ntb),
        in_specs=in_specs,
        out_specs=[pl.BlockSpec((TB, wv), rb(lambda hi: hi)), st_spec],
        out_shape=[jax.ShapeDtypeStruct((b * t, hv * dv), y_dtype),
                   jax.ShapeDtypeStruct((1, b, hv, dk, dv), F32)],
        scratch_shapes=[pltpu.VMEM((HBK * rep, dk, dv), F32),
                        pltpu.VMEM((TB + CONV_PAD, wq), F32), pltpu.VMEM((TB + CONV_PAD, wq), F32),
                        pltpu.VMEM((TB + CONV_PAD, wv), F32)],
        compiler_params=_params(("arbitrary", "arbitrary", "arbitrary"), VMEM_LIMIT_MIXER_MIB),
        name="gdn_seq",
    )(*args)


def _rope_tables(pos, dk):
    half = dk // 2
    inv = ROPE_BASE ** (-jnp.arange(half, dtype=F32) / half)
    ang = pos.astype(F32)[:, None] * inv[None, :]
    return jnp.cos(ang), jnp.sin(ang)


def _trunk(x, p, states, pos, w, cfg):
    (norm_pre, norm_post, ple_proj, ple_gate, ret_w_in, ret_head_norm, ret_w_out,
     mlstm_w_in, mlstm_b_gate, mlstm_head_norm, mlstm_w_out,
     gdn_w_in, gdn_conv_w, gdn_a_log, gdn_dt_bias, gdn_head_norm, gdn_w_out) = w
    ret_s, ml_c, ml_n, ml_m, gdn_s, gdn_conv = states
    b, t, d = x.shape
    depth = norm_pre.shape[0]
    n_ret = ret_w_in.shape[0]
    m = b * t
    tm, ydt, pdt = cfg["tm"], cfg["y_dtype"], cfg["proj_dtype"]
    cos, sin = _rope_tables(pos, QK_HEAD_DIM)
    r = x.reshape(m, d)
    p2 = p.reshape(depth, m, p.shape[-1])
    ret_out = None
    outs = {}
    keep = CONV_W - 1
    for i in range(depth):
        kind, j = i % 3, i // 3
        if kind == 0:
            proj, _, _ = _inproj(r, norm_pre, i, ret_w_in, j, tm, cfg["tn_in"], pdt)
            y, ret_out = _retention(proj, b, t, ret_s, j, cos, sin, ret_head_norm, ret_out, n_ret,
                                    cfg["ret_L"], cfg["ret_BB"], cfg["ret_HB"], ydt)
            w_out = ret_w_out
        elif kind == 1:
            proj, gates, _ = _inproj(r, norm_pre, i, mlstm_w_in, j, tm, cfg["tn_in"], pdt)
            y, outs["c"], outs["n"], outs["m"] = _mlstm(
                proj, gates, b, t, ml_c, ml_n, ml_m, j, mlstm_b_gate, mlstm_head_norm,
                cfg["ml_L"], cfg["ml_BB"], cfg["ml_HB"], ydt)
            w_out = mlstm_w_out
        else:
            cdim = gdn_conv_w.shape[-1]
            from_tail = pdt != F32
            assert t >= keep and (not from_tail or (t % tm == 0 and keep <= TAIL_ROWS))
            proj, gates, tail = _inproj(r, norm_pre, i, gdn_w_in, j, tm, cfg["tn_in"], pdt, want_tail=from_tail)
            if gdn_s is None:
                y, outs["gs"] = _gdn_seq(proj, gates, b, t, j, gdn_conv_w, gdn_a_log, gdn_dt_bias,
                                         gdn_head_norm, cfg["gdn_TB"], cfg["gdn_L"], cfg["gdn_HBK"], ydt)
            else:
                y, outs["gs"] = _gdn_step(proj, gates, b, t, gdn_s, gdn_conv, j, gdn_conv_w, gdn_a_log,
                                          gdn_dt_bias, gdn_head_norm, cfg["gdn_BB"], ydt)
            if from_tail:
                last_tiles = tail.reshape(b, t // tm, TAIL_ROWS, -1)[:, -1]
                outs["gc"] = last_tiles[:, TAIL_ROWS - keep:, :cdim][None]
            else:
                outs["gc"] = proj.reshape(b, t, -1)[:, t - keep:, :cdim][None]
            w_out = gdn_w_out
        mix = _outproj(y, w_out, j, tm, cfg["tn_out"])
        r = _post_ple(mix, r, norm_post, p2, ple_proj, ple_gate, i, cfg["tm_ple"], cfg["tn_ple"])
    return (r.reshape(b, t, d), ret_out, outs["c"], outs["n"], outs["m"], outs["gs"], outs["gc"])


_PROMPT_CFG = dict(tm=2048, tn_in=512, tn_out=512, tm_ple=1024, tn_ple=512, y_dtype=BF16, proj_dtype=BF16,
                   ret_L=256, ret_BB=1, ret_HB=4, ml_L=512, ml_BB=1, ml_HB=1,
                   gdn_TB=1024, gdn_L=64, gdn_HBK=1)
_SAMPLE_CFG = dict(tm=512, tn_in=1024, tn_out=512, tm_ple=512, tn_ple=512, y_dtype=F32, proj_dtype=F32,
                   ret_L=4, ret_BB=2, ret_HB=8, ml_L=4, ml_BB=2, ml_HB=8, gdn_BB=4)


def kernel(x_prompt, x_sample, state_ret_S, state_mlstm_C, state_mlstm_n, state_mlstm_m, state_gdn_S, state_gdn_conv, p_prompt, p_sample, norm_pre, norm_post, ple_proj, ple_gate, ret_w_in, ret_head_norm, ret_w_out, mlstm_w_in, mlstm_b_gate, mlstm_head_norm, mlstm_w_out, gdn_w_in, gdn_conv_w, gdn_a_log, gdn_dt_bias, gdn_head_norm, gdn_w_out):
    w = (norm_pre, norm_post, ple_proj, ple_gate, ret_w_in, ret_head_norm, ret_w_out,
         mlstm_w_in, mlstm_b_gate, mlstm_head_norm, mlstm_w_out,
         gdn_w_in, gdn_conv_w, gdn_a_log, gdn_dt_bias, gdn_head_norm, gdn_w_out)
    pos_p = jnp.arange(x_prompt.shape[1])
    yp, ret_p, mc_p, mn_p, mm_p, gs_p, gc_p = _trunk(
        x_prompt, p_prompt, (None,) * 6, pos_p, w, _PROMPT_CFG)
    pos_s = PAST_LEN + jnp.arange(x_sample.shape[1])
    ys, ret_s, mc_s, mn_s, mm_s, gs_s, gc_s = _trunk(
        x_sample, p_sample,
        (state_ret_S, state_mlstm_C, state_mlstm_n, state_mlstm_m, state_gdn_S, state_gdn_conv),
        pos_s, w, _SAMPLE_CFG)
    return (yp, ys, ret_p, mc_p, mn_p, mm_p, gs_p, gc_p, ret_s, mc_s, mn_s, mm_s, gs_s, gc_s)
```

```python
import functools

import jax
import jax.numpy as jnp
from jax import lax
from jax.experimental import pallas as pl
from jax.experimental.pallas import tpu as pltpu

F32 = jnp.float32
BF16 = jnp.bfloat16
EPS = 1e-6
ROPE_BASE = 10000.0
CONV_W = 4
PAST_LEN = 16384
QK_HEADS = 8
QK_HEAD_DIM = 256
MIB = 1024 * 1024
NEG_INF = float("-inf")
CONV_PAD = 8
LANES = 128
TAIL_ROWS = 8
VMEM_LIMIT_MATMUL_MIB = 58
VMEM_LIMIT_MIXER_MIB = 48


def _params(sem, vmem_mib):
    return pltpu.CompilerParams(dimension_semantics=sem, vmem_limit_bytes=vmem_mib * MIB)


def _bdot(a, b):
    return jnp.dot(a.astype(BF16), b.astype(BF16), preferred_element_type=F32)


def _bdot_nt(a, b):
    return lax.dot_general(a.astype(BF16), b.astype(BF16), (((1,), (1,)), ((), ())),
                           preferred_element_type=F32)


def _bdot_tn(a, b):
    return lax.dot_general(a.astype(BF16), b.astype(BF16), (((0,), (0,)), ((), ())),
                           preferred_element_type=F32)


def _sigmoid(x):
    return 0.5 * jnp.tanh(0.5 * x) + 0.5


def _silu(x):
    h = 0.5 * x
    return h * jnp.tanh(h) + h


def _rows3(table):
    return table.reshape(table.shape[0], 1, table.shape[1])


SLAB = 256


def _row_slabs(tm):
    step = min(SLAB, tm)
    return [slice(s, s + step) for s in range(0, tm, step)]


def _once_per_row_tile(block_shape, index_map):
    return pl.BlockSpec(block_shape, index_map, pipeline_mode=pl.Buffered(1))


def _inproj_body(x_ref, g_ref, w_ref, *rest, w_is_nk, n_main, gate_cols, has_tail):
    has_gates = gate_cols > 0
    outs = list(rest[:1 + has_gates + has_tail])
    h_ref = rest[-1]
    o_ref = outs.pop(0)
    gates_ref = outs.pop(0) if has_gates else None
    tail_ref = outs.pop(0) if has_tail else None
    c = pl.program_id(1)

    @pl.when(c == 0)
    def _():
        for rows in _row_slabs(x_ref.shape[0]):
            x = x_ref[rows, :]
            ms = jnp.mean(x * x, axis=-1, keepdims=True)
            h_ref[rows, :] = (x * lax.rsqrt(ms + EPS) * g_ref[...]).astype(BF16)

    w = w_ref[...].astype(BF16)
    contract = (((1,), (1,)), ((), ())) if w_is_nk else (((1,), (0,)), ((), ()))
    acc = lax.dot_general(h_ref[...], w, contract, preferred_element_type=F32)

    def write_main():
        o_ref[...] = acc.astype(o_ref.dtype)
        if has_tail:
            tail_ref[...] = acc[acc.shape[0] - TAIL_ROWS:, :]

    if has_gates:
        pl.when(c < n_main)(write_main)

        @pl.when(c == n_main)
        def _():
            lane = _iota2(acc.shape[0], LANES, 1)
            gates_ref[...] = jnp.where(lane < gate_cols, acc[:, :LANES], 0.0)
    else:
        write_main()


def _inproj(x2d, g_all, layer, w_all, j, tm, tn, out_dtype, want_tail=False):
    m, d = x2d.shape
    n = w_all.shape[-1]
    n_main = n // tn
    has_gates = n % tn != 0
    assert n - n_main * tn <= LANES
    w_is_nk = n % LANES != 0
    if w_is_nk:
        w_all = jnp.swapaxes(w_all, 1, 2)
        w_spec = pl.BlockSpec((None, tn, d), lambda i, c: (j, c, 0))
    else:
        w_spec = pl.BlockSpec((None, d, tn), lambda i, c: (j, 0, c))

    def main_col(c):
        return jnp.minimum(c, n_main - 1)

    out_specs = [pl.BlockSpec((tm, tn), lambda i, c: (i, main_col(c)))]
    out_shape = [jax.ShapeDtypeStruct((m, n_main * tn), out_dtype)]
    if has_gates:
        out_specs.append(pl.BlockSpec((tm, LANES), lambda i, c: (i, 0)))
        out_shape.append(jax.ShapeDtypeStruct((m, LANES), F32))
    if want_tail:
        out_specs.append(pl.BlockSpec((None, TAIL_ROWS, tn), lambda i, c: (i, 0, main_col(c))))
        out_shape.append(jax.ShapeDtypeStruct((m // tm, TAIL_ROWS, n_main * tn), F32))
    res = pl.pallas_call(
        functools.partial(_inproj_body, w_is_nk=w_is_nk, n_main=n_main, gate_cols=n - n_main * tn,
                          has_tail=want_tail),
        grid=(m // tm, n_main + has_gates),
        in_specs=[_once_per_row_tile((tm, d), lambda i, c: (i, 0)),
                  pl.BlockSpec((None, 1, d), lambda i, c: (layer, 0, 0)),
                  w_spec],
        out_specs=out_specs,
        out_shape=out_shape,
        scratch_shapes=[pltpu.VMEM((tm, d), BF16)],
        compiler_params=_params(("arbitrary", "arbitrary"), VMEM_LIMIT_MATMUL_MIB),
        name="inproj",
    )(x2d, _rows3(g_all), w_all)
    res = list(res)
    proj = res.pop(0)
    gates = res.pop(0) if has_gates else None
    tail = res.pop(0) if want_tail else None
    return proj, gates, tail


def _outproj_body(y_ref, w_ref, o_ref):
    o_ref[...] = jnp.dot(y_ref[...].astype(BF16), w_ref[...].astype(BF16), preferred_element_type=F32)


def _outproj(y2d, w_all, j, tm, tn):
    m, k = y2d.shape
    n = w_all.shape[-1]
    return pl.pallas_call(
        _outproj_body,
        grid=(m // tm, n // tn),
        in_specs=[_once_per_row_tile((tm, k), lambda i, c: (i, 0)),
                  pl.BlockSpec((None, k, tn), lambda i, c: (j, 0, c))],
        out_specs=pl.BlockSpec((tm, tn), lambda i, c: (i, c)),
        out_shape=jax.ShapeDtypeStruct((m, n), F32),
        compiler_params=_params(("arbitrary", "arbitrary"), VMEM_LIMIT_MATMUL_MIB),
        name="outproj",
    )(y2d, w_all)


def _post_ple_body(mix_ref, r_ref, g_ref, p_ref, proj_ref, gate_ref, o_ref, rs_ref, r1b_ref, *, tn):
    c = pl.program_id(1)

    @pl.when(c == 0)
    def _():
        for rows in _row_slabs(mix_ref.shape[0]):
            mix = mix_ref[rows, :]
            rs = lax.rsqrt(jnp.mean(mix * mix, axis=-1, keepdims=True) + EPS)
            rs_ref[rows, :] = rs
            r1b_ref[rows, :] = (r_ref[rows, :] + mix * rs * g_ref[...]).astype(BF16)

    gate = jnp.dot(r1b_ref[...], gate_ref[...].astype(BF16), preferred_element_type=F32)
    emb = _bdot(p_ref[...], proj_ref[...])
    cols = pl.ds(pl.multiple_of(c * tn, tn), tn)
    r1 = r_ref[:, cols] + mix_ref[:, cols] * rs_ref[...] * g_ref[:, cols]
    o_ref[...] = r1 + emb * _sigmoid(gate)


def _post_ple(mix, r, g_all, p_all, proj_all, gate_all, layer, tm, tn):
    m, d = r.shape
    pd = p_all.shape[-1]
    return pl.pallas_call(
        functools.partial(_post_ple_body, tn=tn),
        grid=(m // tm, d // tn),
        in_specs=[pl.BlockSpec((tm, d), lambda i, c: (i, 0)),
                  pl.BlockSpec((tm, d), lambda i, c: (i, 0)),
                  pl.BlockSpec((None, 1, d), lambda i, c: (layer, 0, 0)),
                  pl.BlockSpec((None, tm, pd), lambda i, c: (layer, i, 0)),
                  pl.BlockSpec((None, pd, tn), lambda i, c: (layer, 0, c)),
                  pl.BlockSpec((None, d, tn), lambda i, c: (layer, 0, c))],
        out_specs=pl.BlockSpec((tm, tn), lambda i, c: (i, c)),
        out_shape=jax.ShapeDtypeStruct((m, d), F32),
        scratch_shapes=[pltpu.VMEM((tm, 1), F32), pltpu.VMEM((tm, d), BF16)],
        compiler_params=_params(("arbitrary", "arbitrary"), VMEM_LIMIT_MATMUL_MIB),
        name="post_ple",
    )(mix, r, _rows3(g_all), p_all, proj_all, gate_all)


def _iota2(n, m, dim):
    return lax.broadcasted_iota(jnp.int32, (n, m), dim)


def _col_to_row(col, eye):
    return jnp.sum(jnp.where(eye, col, 0.0), axis=0, keepdims=True)


def _cumsum_col_row(col, row_i, col_i, eye):
    row = _col_to_row(col, eye)
    c_col = jnp.sum(jnp.where(col_i <= row_i, row, 0.0), axis=1, keepdims=True)
    c_row = jnp.sum(jnp.where(row_i <= col_i, col, 0.0), axis=0, keepdims=True)
    return c_col, c_row


def _pick_lane(blk, lane_iota, idx):
    return jnp.sum(jnp.where(lane_iota == idx, blk, 0.0), axis=1, keepdims=True)


def _head_norm_gate(o, gain, z, center):
    if center:
        o = o - jnp.mean(o, axis=-1, keepdims=True)
    y = o * lax.rsqrt(jnp.mean(o * o, axis=-1, keepdims=True) + EPS) * gain
    return y * _silu(z.astype(F32))


def _row_block(nc):
    return lambda col: (lambda bi, hi, ci: (bi * nc + ci, col(hi)))


class _State:
    def __init__(self, in_ref, out_ref, scr, nc):
        self.in_ref, self.out_ref, self.scr, self.nc = in_ref, out_ref, scr, nc

    def start(self, chunk):
        if self.nc > 1:
            @pl.when(chunk == 0)
            def _():
                if self.in_ref is None:
                    self.scr[...] = jnp.zeros_like(self.scr)
                else:
                    self.scr[...] = self.in_ref[...]

    def get(self, bb, hh):
        if self.nc > 1:
            return self.scr[bb, hh]
        if self.in_ref is None:
            return jnp.zeros(self.out_ref.shape[2:], F32)
        return self.in_ref[bb, hh]

    def put(self, bb, hh, val):
        if self.nc > 1:
            self.scr[bb, hh] = val
        else:
            self.out_ref[bb, hh] = val

    def finish(self, chunk):
        if self.nc > 1:
            @pl.when(chunk == self.nc - 1)
            def _():
                self.out_ref[...] = self.scr[...]


def _ret_body(*refs, L, BB, HB, NC, DK, DV, has_s0, has_prev):
    lg_ref, q_ref, k_ref, v_ref, z_ref, cos_ref, sin_ref, gain_ref = refs[:8]
    s0_ref = refs[8] if has_s0 else None
    rest = refs[8 + has_s0 + has_prev:]
    y_ref, so_ref = rest[:2]
    st = _State(s0_ref, so_ref, rest[2] if NC > 1 else None, NC)
    hb = pl.program_id(1)
    c = pl.program_id(2)
    st.start(c)

    cos = cos_ref[...]
    sin = sin_ref[...]
    half = DK // 2

    def rot(x):
        x1, x2 = x[:, :half], x[:, half:]
        return jnp.concatenate([x1 * cos - x2 * sin, x2 * cos + x1 * sin], axis=-1)

    row_i = _iota2(L, L, 0)
    col_i = _iota2(L, L, 1)
    rel = (row_i - col_i).astype(F32)
    idx = _iota2(L, 1, 0).astype(F32)
    for hh in range(HB):
        lg = lg_ref[hb * HB + hh]
        decay = jnp.exp(jnp.where(rel >= 0, lg * rel, NEG_INF))
        w_in = jnp.exp(lg * (idx + 1.0))
        w_out = jnp.exp(lg * (L - 1.0 - idx))
        w_all = jnp.exp(jnp.full((1, 1), L, F32) * lg)
        hq = slice(hh * DK, (hh + 1) * DK)
        hv = slice(hh * DV, (hh + 1) * DV)
        for bb in range(BB):
            rows = slice(bb * L, (bb + 1) * L)
            q = rot(q_ref[rows, hq].astype(F32))
            k = rot(k_ref[rows, hq].astype(F32)) * DK ** -0.5
            v = v_ref[rows, hv]
            s = st.get(bb, hh)
            scores = _bdot_nt(q, k) * decay
            o = _bdot(scores, v) + _bdot(q * w_in, s)
            st.put(bb, hh, w_all * s + _bdot_tn(k * w_out, v))
            y_ref[rows, hv] = _head_norm_gate(o, gain_ref[:, hv], z_ref[rows, hv], True).astype(y_ref.dtype)
    st.finish(c)


def _retention(proj, b, t, s0_all, j, cos, sin, gain_all, so_prev, n_layers, L, BB, HB, y_dtype):
    h, dk = QK_HEADS, QK_HEAD_DIM
    dv = gain_all.shape[-1] // h
    nc = t // L
    assert BB == 1 or nc == 1
    hg = h // HB
    rb = _row_block(nc)
    lg = jnp.log1p(-jnp.exp2(-5.0 - jnp.arange(h, dtype=F32)))
    has_s0 = s0_all is not None
    v_blk = (2 * h * dk) // (HB * dv)
    z_blk = (2 * h * dk + h * dv) // (HB * dv)
    in_specs = [pl.BlockSpec(memory_space=pltpu.SMEM),
                pl.BlockSpec((BB * L, HB * dk), rb(lambda hi: hi)),
                pl.BlockSpec((BB * L, HB * dk), rb(lambda hi: hg + hi)),
                pl.BlockSpec((BB * L, HB * dv), rb(lambda hi: v_blk + hi)),
                pl.BlockSpec((BB * L, HB * dv), rb(lambda hi: z_blk + hi)),
                pl.BlockSpec((L, dk // 2), lambda bi, hi, ci: (ci, 0)),
                pl.BlockSpec((L, dk // 2), lambda bi, hi, ci: (ci, 0)),
                pl.BlockSpec((None, 1, HB * dv), lambda bi, hi, ci: (j, 0, hi))]
    args = [lg, proj, proj, proj, proj, cos, sin, _rows3(gain_all)]
    st_spec = pl.BlockSpec((None, BB, HB, dk, dv), lambda bi, hi, ci: (j, bi, hi, 0, 0))
    if has_s0:
        in_specs.append(st_spec)
        args.append(s0_all)
    aliases = {}
    if so_prev is not None:
        in_specs.append(pl.BlockSpec(memory_space=pl.ANY))
        args.append(so_prev)
        aliases = {len(args) - 1: 1}
    return pl.pallas_call(
        functools.partial(_ret_body, L=L, BB=BB, HB=HB, NC=nc, DK=dk, DV=dv, has_s0=has_s0,
                          has_prev=so_prev is not None),
        grid=(b // BB, hg, nc),
        in_specs=in_specs,
        out_specs=[pl.BlockSpec((BB * L, HB * dv), rb(lambda hi: hi)), st_spec],
        out_shape=[jax.ShapeDtypeStruct((b * t, h * dv), y_dtype),
                   jax.ShapeDtypeStruct((n_layers, b, h, dk, dv), F32)],
        scratch_shapes=[pltpu.VMEM((BB, HB, dk, dv), F32)] if nc > 1 else [],
        input_output_aliases=aliases,
        compiler_params=_params(("arbitrary", "arbitrary", "arbitrary"), VMEM_LIMIT_MIXER_MIB),
        name="retention",
    )(*args)


def _mlstm_body(*refs, L, BB, HB, NC, DK, DV, H, has_s0):
    bg_ref, q_ref, k_ref, v_ref, og_ref, z_ref, gt_ref, gain_ref = refs[:8]
    c0_ref, n0_ref, m0_ref = refs[8:11] if has_s0 else (None, None, None)
    rest = refs[8 + 3 * has_s0:]
    y_ref, co_ref, no_ref, mo_ref = rest[:4]
    scr = rest[4:] if NC > 1 else (None, None, None)
    st_c = _State(c0_ref, co_ref, scr[0], NC)
    st_n = _State(n0_ref, no_ref, scr[1], NC)
    st_m = _State(m0_ref, mo_ref, scr[2], NC)
    hb = pl.program_id(1)
    c = pl.program_id(2)
    for st in (st_c, st_n, st_m):
        st.start(c)

    row_i = _iota2(L, L, 0)
    col_i = _iota2(L, L, 1)
    eye = row_i == col_i
    causal = row_i >= col_i
    lane = _iota2(L, 128, 1)
    for bb in range(BB):
        rows = slice(bb * L, (bb + 1) * L)
        gt = gt_ref[rows, :]
        for hh in range(HB):
            head = hb * HB + hh
            hq = slice(hh * DK, (hh + 1) * DK)
            hv = slice(hh * DV, (hh + 1) * DV)
            ig = _pick_lane(gt, lane, head) + bg_ref[head]
            fg = _pick_lane(gt, lane, H + head) + bg_ref[H + head]
            lf = jax.nn.log_sigmoid(fg)
            b_col, b_row = _cumsum_col_row(lf, row_i, col_i, eye)
            i_row = _col_to_row(ig, eye)
            q = q_ref[rows, hq].astype(F32) * DK ** -0.5
            k = k_ref[rows, hq].astype(F32)
            v = v_ref[rows, hv]
            cm = st_c.get(bb, hh)
            nv = st_n.get(bb, hh)
            m_prev = st_m.get(bb, hh)
            dlog = jnp.where(causal, b_col - b_row + i_row, NEG_INF)
            inter = b_col + m_prev
            mt = jnp.maximum(inter, jnp.max(dlog, axis=1, keepdims=True))
            s = _bdot_nt(q, k) * jnp.exp(dlog - mt)
            wi = jnp.exp(inter - mt)
            num = _bdot(s, v) + wi * _bdot(q, cm)
            den = jnp.sum(s, axis=1, keepdims=True) + wi * jnp.sum(q * nv, axis=1, keepdims=True)
            ht = num / jnp.maximum(jnp.abs(den), jnp.exp(-mt))
            m_new = mt[L - 1:L, :]
            b_last = b_col[L - 1:L, :]
            w_last = jnp.exp(b_last - b_col + ig - m_new)
            dec = jnp.exp(b_last + m_prev - m_new)
            kw = k * w_last
            st_c.put(bb, hh, dec * cm + _bdot_tn(kw, v))
            st_n.put(bb, hh, dec * nv + jnp.sum(kw, axis=0, keepdims=True))
            st_m.put(bb, hh, m_new)
            hcell = ht * _sigmoid(og_ref[rows, hv].astype(F32))
            y_ref[rows, hv] = _head_norm_gate(hcell, gain_ref[:, hv], z_ref[rows, hv], True).astype(y_ref.dtype)
    for st in (st_c, st_n, st_m):
        st.finish(c)


def _mlstm(proj, gates, b, t, c0_all, n0_all, m0_all, j, bgate_all, gain_all, L, BB, HB, y_dtype):
    h, dk = QK_HEADS, QK_HEAD_DIM
    dv = gain_all.shape[-1] // h
    nc = t // L
    assert BB == 1 or nc == 1
    hg = h // HB
    rb = _row_block(nc)
    has_s0 = c0_all is not None
    v_off = 2 * h * dk
    wv = HB * dv
    in_specs = [pl.BlockSpec(memory_space=pltpu.SMEM),
                pl.BlockSpec((BB * L, HB * dk), rb(lambda hi: hi)),
                pl.BlockSpec((BB * L, HB * dk), rb(lambda hi: hg + hi)),
                pl.BlockSpec((BB * L, wv), rb(lambda hi: v_off // wv + hi)),
                pl.BlockSpec((BB * L, wv), rb(lambda hi: (v_off + h * dv) // wv + hi)),
                pl.BlockSpec((BB * L, wv), rb(lambda hi: (v_off + 2 * h * dv) // wv + hi)),
                pl.BlockSpec((BB * L, LANES), rb(lambda hi: 0)),
                pl.BlockSpec((None, 1, wv), lambda bi, hi, ci: (j, 0, hi))]
    args = [bgate_all[j], proj, proj, proj, proj, proj, gates, _rows3(gain_all)]
    c_spec = pl.BlockSpec((None, BB, HB, dk, dv), lambda bi, hi, ci: (j, bi, hi, 0, 0))
    n_spec = pl.BlockSpec((None, BB, HB, 1, dk), lambda bi, hi, ci: (j, bi, hi, 0, 0))
    m_spec = pl.BlockSpec((None, BB, HB, 1, 1), lambda bi, hi, ci: (j, bi, hi, 0, 0))
    nl = 1
    if has_s0:
        assert c0_all.shape[0] == nl
        in_specs += [c_spec, n_spec, m_spec]
        args += [c0_all, n0_all.reshape(nl, b, h, 1, dk), m0_all.reshape(nl, b, h, 1, 1)]
    scratch = [pltpu.VMEM((BB, HB, dk, dv), F32), pltpu.VMEM((BB, HB, 1, dk), F32),
               pltpu.VMEM((BB, HB, 1, 1), F32)] if nc > 1 else []
    y, co, no, mo = pl.pallas_call(
        functools.partial(_mlstm_body, L=L, BB=BB, HB=HB, NC=nc, DK=dk, DV=dv, H=h, has_s0=has_s0),
        grid=(b // BB, hg, nc),
        in_specs=in_specs,
        out_specs=[pl.BlockSpec((BB * L, wv), rb(lambda hi: hi)), c_spec, n_spec, m_spec],
        out_shape=[jax.ShapeDtypeStruct((b * t, h * dv), y_dtype),
                   jax.ShapeDtypeStruct((nl, b, h, dk, dv), F32),
                   jax.ShapeDtypeStruct((nl, b, h, 1, dk), F32),
                   jax.ShapeDtypeStruct((nl, b, h, 1, 1), F32)],
        scratch_shapes=scratch,
        compiler_params=_params(("arbitrary", "arbitrary", "arbitrary"), VMEM_LIMIT_MIXER_MIB),
        name="mlstm",
    )(*args)
    return y, co, no.reshape(nl, b, h, dk), mo.reshape(nl, b, h)


def _merge_masks(row_i, col_i, L):
    masks = []
    s = 1
    while s < L:
        masks.append(((row_i // (2 * s)) == (col_i // (2 * s))) & ((row_i // s) > (col_i // s)))
        s *= 2
    return masks


def _unit_lower_inverse(a_strict, eye, masks, mm):
    x = jnp.where(eye, 1.0, 0.0) - jnp.where(masks[0], a_strict, 0.0)
    for mask in masks[1:]:
        e = jnp.where(mask, a_strict, 0.0)
        x = x - mm(x, mm(e, x))
    return x


def _vpu_mm(a, b):
    out = a[:, 0:1] * b[0:1, :]
    for kk in range(1, a.shape[1]):
        out = out + a[:, kk:kk + 1] * b[kk:kk + 1, :]
    return out


def _bmm(a, b):
    return jnp.einsum("gik,gkj->gij", a.astype(BF16), b.astype(BF16), preferred_element_type=F32)


def _bmm_nt(a, b):
    return jnp.einsum("gik,gjk->gij", a.astype(BF16), b.astype(BF16), preferred_element_type=F32)


def _bmm_tn(a, b):
    return jnp.einsum("gki,gkj->gij", a.astype(BF16), b.astype(BF16), preferred_element_type=F32)


def _gdn_dims(conv_w_all, a_log_all, gain_all):
    dk = dv = gain_all.shape[-1]
    hv = a_log_all.shape[-1]
    cdim = conv_w_all.shape[-1]
    hk = (cdim - hv * dv) // (2 * dk)
    rep = hv // hk
    assert 2 * hv <= 128 and rep * hk == hv
    return dk, dv, hv, hk, rep, cdim


def _gdn_step_body(alv_ref, dtv_ref, qx_ref, kx_ref, vx_ref, z_ref, gt_ref, cwq_ref, cwk_ref, cwv_ref,
                   gain_ref, cq0_ref, ck0_ref, cv0_ref, s0_ref, y_ref, so_ref,
                   eq_scr, ek_scr, ev_scr, q_st, k_st, v_st, z_st, qs_st, ks_st, kw_st, u_st, y_st,
                   *, L, BB, HK, REP, DK, DV):
    HV = HK * REP
    R = HV * L
    taps = CONV_W - 1

    def conv(x_ref, c0_ref, scr, cw_ref, bb):
        scr[CONV_PAD - taps:CONV_PAD, :] = c0_ref[bb]
        scr[CONV_PAD:CONV_PAD + L, :] = x_ref[bb * L:(bb + 1) * L, :].astype(F32)
        acc = scr[pl.ds(CONV_PAD - taps, L), :] * cw_ref[0:1, :]
        for w in range(1, CONV_W):
            acc = acc + scr[pl.ds(CONV_PAD - taps + w, L), :] * cw_ref[w:w + 1, :]
        return _silu(acc)

    row_i = _iota2(R, R, 0)
    col_i = _iota2(R, R, 1)
    same_head = (row_i // L) == (col_i // L)
    eye = row_i == col_i
    incl = same_head & (row_i >= col_i)
    strict = same_head & (row_i > col_i)
    masks = [same_head & m for m in _merge_masks(row_i % L, col_i % L, L)]
    lane = _iota2(R, LANES, 1)
    head_of_row = _iota2(R, LANES, 0) // L
    sel_beta = lane == head_of_row
    sel_decay = lane == head_of_row + HV
    tril = (_iota2(L, L, 0) >= _iota2(L, L, 1)).astype(F32)

    def stack_cols(x, sel):
        x8 = jnp.concatenate([x] * (8 // L), axis=0)
        tiled = jnp.concatenate([x8] * (R // 8), axis=0)
        return jnp.sum(jnp.where(sel, tiled, 0.0), axis=1, keepdims=True)

    for bb in range(BB):
        rows = slice(bb * L, (bb + 1) * L)
        cq = conv(qx_ref, cq0_ref, eq_scr, cwq_ref, bb)
        ck = conv(kx_ref, ck0_ref, ek_scr, cwk_ref, bb)
        cv = conv(vx_ref, cv0_ref, ev_scr, cwv_ref, bb)
        for vh in range(HV):
            st = slice(vh * L, (vh + 1) * L)
            kh = vh // REP
            q_st[st, :] = cq[:, kh * DK:(kh + 1) * DK]
            k_st[st, :] = ck[:, kh * DK:(kh + 1) * DK]
            v_st[st, :] = cv[:, vh * DV:(vh + 1) * DV]
            z_st[st, :] = z_ref[rows, vh * DV:(vh + 1) * DV].astype(F32)
        q = q_st[...]
        k = k_st[...]
        q = q * lax.rsqrt(jnp.sum(q * q, axis=-1, keepdims=True) + EPS) * DK ** -0.5
        k = k * lax.rsqrt(jnp.sum(k * k, axis=-1, keepdims=True) + EPS)
        q_st[...] = q
        k_st[...] = k

        gt = gt_ref[rows, :]
        beta = stack_cols(_sigmoid(gt), sel_beta)
        g = -jnp.exp(alv_ref[...]) * jax.nn.softplus(gt + dtv_ref[...])
        g_cum = _vpu_mm(tril, g)
        g_col = stack_cols(g_cum, sel_decay)
        g_last = stack_cols(jnp.broadcast_to(g_cum[L - 1:L, :], (L, LANES)), sel_decay)
        g_row = _col_to_row(g_col, eye)
        decay = jnp.exp(jnp.where(incl, g_col - g_row, NEG_INF))
        kq = _bdot_nt(jnp.concatenate([k, q], axis=0), k)
        kk, qk = kq[:R], kq[R:]
        a = jnp.where(strict, beta * kk * decay, 0.0)
        x = _unit_lower_inverse(a, eye, masks, _bdot)

        for vh in range(HV):
            st = slice(vh * L, (vh + 1) * L)
            qk_rows = jnp.concatenate([q_st[st, :], k_st[st, :]], axis=0)
            both = _bdot(qk_rows, s0_ref[bb, vh])
            qs_st[st, :] = both[:L]
            ks_st[st, :] = both[L:]
        eg = jnp.exp(g_col)
        rhs = beta * v_st[...] - (beta * eg) * ks_st[...]
        u = _bdot(x, rhs)
        o = eg * qs_st[...] + _bdot(qk * decay, u)
        y_st[...] = _head_norm_gate(o, gain_ref[...], z_st[...], False)
        kw_st[...] = k * jnp.exp(g_last - g_col)
        u_st[...] = u
        eg_last = jnp.exp(g_last)
        for vh in range(HV):
            st = slice(vh * L, (vh + 1) * L)
            y_ref[rows, vh * DV:(vh + 1) * DV] = y_st[st, :].astype(y_ref.dtype)
            so_ref[bb, vh] = (eg_last[vh * L:vh * L + 1, :] * s0_ref[bb, vh]
                              + _bdot_tn(kw_st[st, :], u_st[st, :]))


def _gdn_step(proj, gates, b, t, s0_all, conv0_all, j, conv_w_all, a_log_all, dt_bias_all, gain_all,
              BB, y_dtype):
    dk, dv, hv, hk, rep, cdim = _gdn_dims(conv_w_all, a_log_all, gain_all)
    L = t
    R = hv * L
    assert 8 % L == 0 and R % 8 == 0 and dk == dv
    hg = 1
    wq, wv = hk * dk, hv * dv
    k_blk = 1
    v_blk = (2 * hk * dk) // wv
    z_blk = cdim // wv
    rb = _row_block(1)
    lanes_of = lambda vec: jnp.zeros((1, LANES), F32).at[0, hv:2 * hv].set(vec.astype(F32))

    def cspec(width, off):
        return pl.BlockSpec((None, CONV_W, width), lambda bi, hi, ci: (j, 0, off + hi))

    def c0spec(width, off):
        return pl.BlockSpec((None, BB, CONV_W - 1, width), lambda bi, hi, ci: (j, bi, 0, off + hi))

    st_spec = pl.BlockSpec((None, BB, hv, dk, dv), lambda bi, hi, ci: (j, bi, hi, 0, 0))
    in_specs = [pl.BlockSpec((1, LANES), lambda bi, hi, ci: (0, 0)),
                pl.BlockSpec((1, LANES), lambda bi, hi, ci: (0, 0)),
                pl.BlockSpec((BB * L, wq), rb(lambda hi: hi)),
                pl.BlockSpec((BB * L, wq), rb(lambda hi: k_blk + hi)),
                pl.BlockSpec((BB * L, wv), rb(lambda hi: v_blk + hi)),
                pl.BlockSpec((BB * L, wv), rb(lambda hi: z_blk + hi)),
                pl.BlockSpec((BB * L, LANES), rb(lambda hi: 0)),
                cspec(wq, 0), cspec(wq, k_blk), cspec(wv, v_blk),
                pl.BlockSpec((None, 1, dv), lambda bi, hi, ci: (j, 0, 0)),
                c0spec(wq, 0), c0spec(wq, k_blk), c0spec(wv, v_blk), st_spec]
    args = [lanes_of(a_log_all[j]), lanes_of(dt_bias_all[j]), proj, proj, proj, proj, gates,
            conv_w_all, conv_w_all, conv_w_all, _rows3(gain_all),
            conv0_all, conv0_all, conv0_all, s0_all]
    stacked = [pltpu.VMEM((R, dk), F32)] * 9
    y, so = pl.pallas_call(
        functools.partial(_gdn_step_body, L=L, BB=BB, HK=hk, REP=rep, DK=dk, DV=dv),
        grid=(b // BB, hg, 1),
        in_specs=in_specs,
        out_specs=[pl.BlockSpec((BB * L, wv), rb(lambda hi: hi)), st_spec],
        out_shape=[jax.ShapeDtypeStruct((b * t, hv * dv), y_dtype),
                   jax.ShapeDtypeStruct((1, b, hv, dk, dv), F32)],
        scratch_shapes=[pltpu.VMEM((L + CONV_PAD, wq), F32), pltpu.VMEM((L + CONV_PAD, wq), F32),
                        pltpu.VMEM((L + CONV_PAD, wv), F32)] + stacked,
        compiler_params=_params(("arbitrary", "arbitrary", "arbitrary"), VMEM_LIMIT_MIXER_MIB),
        name="gdn_step",
    )(*args)
    return y, so


def _gdn_seq_body(al_ref, dt_ref, qx_ref, kx_ref, vx_ref, z_ref, gt_ref, cwq_ref, cwk_ref, cwv_ref,
                  gain_ref, y_ref, so_ref, s_scr, eq_scr, ek_scr, ev_scr,
                  *, TB, C, HBK, NTB, DK, DV, REP, HV):
    hb = pl.program_id(1)
    tb = pl.program_id(2)
    G = TB // C
    taps = CONV_W - 1

    @pl.when(tb == 0)
    def _():
        for scr in (eq_scr, ek_scr, ev_scr):
            scr[0:CONV_PAD, :] = jnp.zeros((CONV_PAD, scr.shape[1]), F32)
        s_scr[...] = jnp.zeros_like(s_scr)

    def conv(x_ref, scr, cw_ref):
        scr[CONV_PAD:CONV_PAD + TB, :] = x_ref[...].astype(F32)
        acc = scr[pl.ds(CONV_PAD - taps, TB), :] * cw_ref[0:1, :]
        for w in range(1, CONV_W):
            acc = acc + scr[pl.ds(CONV_PAD - taps + w, TB), :] * cw_ref[w:w + 1, :]
        if NTB > 1:
            scr[0:CONV_PAD, :] = scr[TB:TB + CONV_PAD, :]
        return _silu(acc)

    cq = conv(qx_ref, eq_scr, cwq_ref)
    ck = conv(kx_ref, ek_scr, cwk_ref)
    cv = conv(vx_ref, ev_scr, cwv_ref)

    row_i = _iota2(C, C, 0)
    col_i = _iota2(C, C, 1)
    eye = row_i == col_i
    incl = row_i >= col_i
    PW = REP * C
    prow = _iota2(C, PW, 0)
    plane = _iota2(C, PW, 1)
    pcol = plane % C
    phead = plane // C
    p_eye = prow == pcol
    p_incl = prow >= pcol
    p_strict = prow > pcol
    p_masks = _merge_masks(prow, pcol, C)
    bd_mask = (_iota2(PW, PW, 0) // C) == (_iota2(PW, PW, 1) // C)

    def block_diag(xp):
        return jnp.where(bd_mask, jnp.concatenate([xp] * REP, axis=1), 0.0)

    def packed_mm(ap, bp):
        return _bmm(ap, block_diag(bp))

    def pack_cols(cols):
        out = cols[0]
        for r in range(1, REP):
            out = jnp.where(phead >= r, cols[r], out)
        return out

    lane = _iota2(TB, 128, 1)
    gt = gt_ref[...]
    for kh in range(HBK):
        q = cq[:, kh * DK:(kh + 1) * DK]
        k = ck[:, kh * DK:(kh + 1) * DK]
        q = q * lax.rsqrt(jnp.sum(q * q, axis=-1, keepdims=True) + EPS) * DK ** -0.5
        k = k * lax.rsqrt(jnp.sum(k * k, axis=-1, keepdims=True) + EPS)
        q3 = q.reshape(G, C, DK)
        k3 = k.reshape(G, C, DK)
        kq = _bmm_nt(jnp.concatenate([k3, q3], axis=1), k3)
        kk, qk = kq[:, :C, :], kq[:, C:, :]
        betas, g_cols, g_rows = [], [], []
        for r in range(REP):
            head = (hb * HBK + kh) * REP + r
            betas.append(_sigmoid(_pick_lane(gt, lane, head)).reshape(G, C, 1))
            a_neg = -jnp.exp(jnp.full((1, 1), al_ref[head], F32))
            g = (a_neg * jax.nn.softplus(_pick_lane(gt, lane, HV + head) + dt_ref[head])).reshape(G, C, 1)
            g_lanes = jnp.sum(jnp.where(eye, g, 0.0), axis=1, keepdims=True)
            g_cols.append(jnp.sum(jnp.where(incl, g_lanes, 0.0), axis=2, keepdims=True))
            g_rows.append(jnp.sum(jnp.where(row_i <= col_i, g, 0.0), axis=1, keepdims=True))
        decay_p = jnp.exp(jnp.where(p_incl, pack_cols(g_cols) - jnp.concatenate(g_rows, axis=-1), NEG_INF))
        a_p = jnp.where(p_strict, pack_cols(betas) * jnp.concatenate([kk] * REP, axis=-1) * decay_p, 0.0)
        x_p = _unit_lower_inverse(a_p, p_eye, p_masks, packed_mm)
        for r in range(REP):
            vh = kh * REP + r
            hv = slice(vh * DV, (vh + 1) * DV)
            beta, g_col = betas[r], g_cols[r]
            x = x_p[:, :, r * C:(r + 1) * C]
            decay = decay_p[:, :, r * C:(r + 1) * C]
            v3 = cv[:, hv].reshape(G, C, DV)
            eg = jnp.exp(g_col)
            wu = _bmm(x, jnp.concatenate([(beta * eg) * k3, beta * v3], axis=-1))
            qo = _bmm(qk * decay, wu)
            o0 = qo[:, :, DK:]
            g_last = g_col[:, C - 1:C, :]
            mb = _bmm_tn(k3 * jnp.exp(g_last - g_col), wu)
            b_eff = mb[:, :, DK:]
            lhs = jnp.concatenate([eg * q3 - qo[:, :, :DK], mb[:, :, :DK]], axis=1).astype(BF16)
            eg_last = jnp.exp(g_last)
            s = s_scr[vh]
            outs = []
            for c in range(G):
                both = jnp.dot(lhs[c], s.astype(BF16), preferred_element_type=F32)
                outs.append(both[:C] + o0[c])
                s = eg_last[c] * s - both[C:] + b_eff[c]
            s_scr[vh] = s
            o = jnp.concatenate(outs, axis=0) if G > 1 else outs[0]
            y_ref[:, hv] = _head_norm_gate(o, gain_ref[...], z_ref[:, hv], False).astype(y_ref.dtype)

    @pl.when(tb == NTB - 1)
    def _():
        so_ref[...] = s_scr[...]


def _gdn_seq(proj, gates, b, t, j, conv_w_all, a_log_all, dt_bias_all, gain_all, TB, C, HBK, y_dtype):
    dk, dv, hv, hk, rep, cdim = _gdn_dims(conv_w_all, a_log_all, gain_all)
    assert dk == dv
    ntb = t // TB
    hg = hk // HBK
    wq, wv = HBK * dk, HBK * rep * dv
    k_blk = (hk * dk) // wq
    v_blk = (2 * hk * dk) // wv
    z_blk = cdim // wv
    rb = _row_block(ntb)

    def cspec(width, off):
        return pl.BlockSpec((None, CONV_W, width), lambda bi, hi, ti: (j, 0, off + hi))

    in_specs = [pl.BlockSpec(memory_space=pltpu.SMEM),
                pl.BlockSpec(memory_space=pltpu.SMEM),
                pl.BlockSpec((TB, wq), rb(lambda hi: hi)),
                pl.BlockSpec((TB, wq), rb(lambda hi: k_blk + hi)),
                pl.BlockSpec((TB, wv), rb(lambda hi: v_blk + hi)),
                pl.BlockSpec((TB, wv), rb(lambda hi: z_blk + hi)),
                pl.BlockSpec((TB, LANES), rb(lambda hi: 0)),
                cspec(wq, 0), cspec(wq, k_blk), cspec(wv, v_blk),
                pl.BlockSpec((None, 1, dv), lambda bi, hi, ti: (j, 0, 0))]
    args = [a_log_all[j], dt_bias_all[j], proj, proj, proj, proj, gates,
            conv_w_all, conv_w_all, conv_w_all, _rows3(gain_all)]
    st_spec = pl.BlockSpec((None, None, HBK * rep, dk, dv), lambda bi, hi, ti: (0, bi, hi, 0, 0))
    return pl.pallas_call(
        functools.partial(_gdn_seq_body, TB=TB, C=C, HBK=HBK, NTB=ntb, DK=dk, DV=dv, REP=rep, HV=hv),
        grid=(b, hg, ntb),
        in_specs=in_specs,
        out_specs=[pl.BlockSpec((TB, wv), rb(lambda hi: hi)), st_spec],
        out_shape=[jax.ShapeDtypeStruct((b * t, hv * dv), y_dtype),
                   jax.ShapeDtypeStruct((1, b, hv, dk, dv), F32)],
        scratch_shapes=[pltpu.VMEM((HBK * rep, dk, dv), F32),
                        pltpu.VMEM((TB + CONV_PAD, wq), F32), pltpu.VMEM((TB + CONV_PAD, wq), F32),
                        pltpu.VMEM((TB + CONV_PAD, wv), F32)],
        compiler_params=_params(("arbitrary", "arbitrary", "arbitrary"), VMEM_LIMIT_MIXER_MIB),
        name="gdn_seq",
    )(*args)


def _rope_tables(pos, dk):
    half = dk // 2
    inv = ROPE_BASE ** (-jnp.arange(half, dtype=F32) / half)
    ang = pos.astype(F32)[:, None] * inv[None, :]
    return jnp.cos(ang), jnp.sin(ang)


def _trunk(x, p, states, pos, w, cfg):
    (norm_pre, norm_post, ple_proj, ple_gate, ret_w_in, ret_head_norm, ret_w_out,
     mlstm_w_in, mlstm_b_gate, mlstm_head_norm, mlstm_w_out,
     gdn_w_in, gdn_conv_w, gdn_a_log, gdn_dt_bias, gdn_head_norm, gdn_w_out) = w
    ret_s, ml_c, ml_n, ml_m, gdn_s, gdn_conv = states
    b, t, d = x.shape
    depth = norm_pre.shape[0]
    n_ret = ret_w_in.shape[0]
    m = b * t
    tm, ydt, pdt = cfg["tm"], cfg["y_dtype"], cfg["proj_dtype"]
    cos, sin = _rope_tables(pos, QK_HEAD_DIM)
    r = x.reshape(m, d)
    p2 = p.reshape(depth, m, p.shape[-1])
    ret_out = None
    outs = {}
    keep = CONV_W - 1
    for i in range(depth):
        kind, j = i % 3, i // 3
        if kind == 0:
            proj, _, _ = _inproj(r, norm_pre, i, ret_w_in, j, tm, cfg["tn_in"], pdt)
            y, ret_out = _retention(proj, b, t, ret_s, j, cos, sin, ret_head_norm, ret_out, n_ret,
                                    cfg["ret_L"], cfg["ret_BB"], cfg["ret_HB"], ydt)
            w_out = ret_w_out
        elif kind == 1:
            proj, gates, _ = _inproj(r, norm_pre, i, mlstm_w_in, j, tm, cfg["tn_in"], pdt)
            y, outs["c"], outs["n"], outs["m"] = _mlstm(
                proj, gates, b, t, ml_c, ml_n, ml_m, j, mlstm_b_gate, mlstm_head_norm,
                cfg["ml_L"], cfg["ml_BB"], cfg["ml_HB"], ydt)
            w_out = mlstm_w_out
        else:
            cdim = gdn_conv_w.shape[-1]
            from_tail = pdt != F32
            assert t >= keep and (not from_tail or (t % tm == 0 and keep <= TAIL_ROWS))
            proj, gates, tail = _inproj(r, norm_pre, i, gdn_w_in, j, tm, cfg["tn_in"], pdt, want_tail=from_tail)
            if gdn_s is None:
                y, outs["gs"] = _gdn_seq(proj, gates, b, t, j, gdn_conv_w, gdn_a_log, gdn_dt_bias,
                                         gdn_head_norm, cfg["gdn_TB"], cfg["gdn_L"], cfg["gdn_HBK"], ydt)
            else:
                y, outs["gs"] = _gdn_step(proj, gates, b, t, gdn_s, gdn_conv, j, gdn_conv_w, gdn_a_log,
                                          gdn_dt_bias, gdn_head_norm, cfg["gdn_BB"], ydt)
            if from_tail:
                last_tiles = tail.reshape(b, t // tm, TAIL_ROWS, -1)[:, -1]
                outs["gc"] = last_tiles[:, TAIL_ROWS - keep:, :cdim][None]
            else:
                outs["gc"] = proj.reshape(b, t, -1)[:, t - keep:, :cdim][None]
            w_out = gdn_w_out
        mix = _outproj(y, w_out, j, tm, cfg["tn_out"])
        r = _post_ple(mix, r, norm_post, p2, ple_proj, ple_gate, i, cfg["tm_ple"], cfg["tn_ple"])
    return (r.reshape(b, t, d), ret_out, outs["c"], outs["n"], outs["m"], outs["gs"], outs["gc"])


_PROMPT_CFG = dict(tm=2048, tn_in=512, tn_out=512, tm_ple=1024, tn_ple=512, y_dtype=BF16, proj_dtype=BF16,
                   ret_L=256, ret_BB=1, ret_HB=4, ml_L=512, ml_BB=1, ml_HB=2,
                   gdn_TB=1024, gdn_L=64, gdn_HBK=2)
_SAMPLE_CFG = dict(tm=512, tn_in=1024, tn_out=512, tm_ple=512, tn_ple=512, y_dtype=F32, proj_dtype=F32,
                   ret_L=4, ret_BB=2, ret_HB=8, ml_L=4, ml_BB=2, ml_HB=8, gdn_BB=4)


def kernel(x_prompt, x_sample, state_ret_S, state_mlstm_C, state_mlstm_n, state_mlstm_m, state_gdn_S, state_gdn_conv, p_prompt, p_sample, norm_pre, norm_post, ple_proj, ple_gate, ret_w_in, ret_head_norm, ret_w_out, mlstm_w_in, mlstm_b_gate, mlstm_head_norm, mlstm_w_out, gdn_w_in, gdn_conv_w, gdn_a_log, gdn_dt_bias, gdn_head_norm, gdn_w_out):
    w = (norm_pre, norm_post, ple_proj, ple_gate, ret_w_in, ret_head_norm, ret_w_out,
         mlstm_w_in, mlstm_b_gate, mlstm_head_norm, mlstm_w_out,
         gdn_w_in, gdn_conv_w, gdn_a_log, gdn_dt_bias, gdn_head_norm, gdn_w_out)
    pos_p = jnp.arange(x_prompt.shape[1])
    yp, ret_p, mc_p, mn_p, mm_p, gs_p, gc_p = _trunk(
        x_prompt, p_prompt, (None,) * 6, pos_p, w, _PROMPT_CFG)
    pos_s = PAST_LEN + jnp.arange(x_sample.shape[1])
    ys, ret_s, mc_s, mn_s, mm_s, gs_s, gc_s = _trunk(
        x_sample, p_sample,
        (state_ret_S, state_mlstm_C, state_mlstm_n, state_mlstm_m, state_gdn_S, state_gdn_conv),
        pos_s, w, _SAMPLE_CFG)
    return (yp, ys, ret_p, mc_p, mn_p, mm_p, gs_p, gc_p, ret_s, mc_s, mn_s, mm_s, gs_s, gc_s)
```
